```python
import math
import jax
import jax.numpy as jnp
from jax import lax
import numpy as np

D_MODEL = 1024
BATCH = 8
SEQ = 2048
DEPTH = 4
DEC_BATCH = 32
DEC_SEQ = 16
PAST_LEN = 4096

CHUNK = 64
HEAD_DIM = 64
H_RET = 6
H_SWA = 6
KV_SWA = 2
G_SWA = H_SWA // KV_SWA
H_CA = 4
MIX_WIDTH = (H_RET + H_SWA + H_CA) * HEAD_DIM
SWA_WINDOW = 128
SWA_PREV_CHUNKS = SWA_WINDOW // CHUNK
SWA_ROWS = SWA_PREV_CHUNKS * CHUNK
CA_PREV_CHUNKS = 8
CA_ROWS = CA_PREV_CHUNKS * CHUNK
REL_CLIP = 128
N_REL = 2 * REL_CLIP + 1
D_FF = 2816
CONV_W = 3
EPS = 1e-6
ROPE_BASE = 10000.0
NEG_INF = -1e30
SPLIT_HEADS = (H_RET, H_RET, H_RET, H_RET, H_SWA, KV_SWA, KV_SWA, H_CA, H_CA, H_CA)
SPLIT_SIZES = tuple(h * HEAD_DIM for h in SPLIT_HEADS)
IN_COLS = sum(SPLIT_SIZES)

kernel_name = "hybrid_streaming_encoder_step"


def rms_norm(x, g):
    xf = x.astype(jnp.float32)
    y = xf * lax.rsqrt(jnp.mean(xf * xf, axis=-1, keepdims=True) + EPS)
    return (y * g.astype(jnp.float32)).astype(x.dtype)


def split_proj(p):
    b, l = p.shape[0], p.shape[1]
    outs = []
    start = 0
    for size, heads in zip(SPLIT_SIZES, SPLIT_HEADS):
        outs.append(p[:, :, start:start + size].reshape(b, l, heads, HEAD_DIM))
        start += size
    return outs


def rotary(x, pos):
    half = HEAD_DIM // 2
    inv = 1.0 / (ROPE_BASE ** (jnp.arange(half, dtype=jnp.float32) / half))
    ang = pos.astype(jnp.float32)[:, None] * inv[None, :]
    cos = jnp.cos(ang)[None, :, None, :]
    sin = jnp.sin(ang)[None, :, None, :]
    xf = x.astype(jnp.float32)
    x1, x2 = xf[..., :half], xf[..., half:]
    return jnp.concatenate([x1 * cos - x2 * sin, x2 * cos + x1 * sin], axis=-1).astype(x.dtype)


def ret_log_decay():
    return jnp.log(1.0 - 2.0 ** (-5.0 - jnp.arange(H_RET, dtype=jnp.float32)))


def retention_block(q, k, v, s0):
    t_len = q.shape[1]
    log_g = ret_log_decay()
    t = jnp.arange(t_len, dtype=jnp.float32)
    diff = t[:, None] - t[None, :]
    decay = jnp.where(diff >= 0, jnp.exp(jnp.maximum(diff, 0.0)[None] * log_g[:, None, None]), 0.0)
    scores = jnp.einsum('bihd,bjhd->bhij', q, k) * decay[None]
    inner = jnp.einsum('bhij,bjhd->bihd', scores, v)
    q_decay = jnp.exp((t + 1.0)[:, None] * log_g[None, :])
    cross = jnp.einsum('bihd,bhde->bihe', q, s0) * q_decay[None, :, :, None]
    k_decay = jnp.exp((t_len - 1.0 - t)[:, None] * log_g[None, :])
    s_new = (jnp.exp(t_len * log_g)[None, :, None, None] * s0
             + jnp.einsum('bjhd,bjhe->bhde', k * k_decay[None, :, :, None], v))
    return inner + cross, s_new


def retention_prompt(q, k, v):
    b, l, h, d = q.shape
    nc = l // CHUNK

    def to_chunks(x):
        return jnp.moveaxis(x.astype(jnp.float32).reshape(b, nc, CHUNK, h, d), 1, 0)

    def step(s, qkv):
        o, s_new = retention_block(qkv[0], qkv[1], qkv[2], s)
        return s_new, o

    s0 = jnp.zeros((b, h, d, d), jnp.float32)
    s_fin, o = lax.scan(step, s0, (to_chunks(q), to_chunks(k), to_chunks(v)))
    return jnp.moveaxis(o, 0, 1).reshape(b, l, h, d), s_fin


def retention_out(o, g):
    mu = jnp.mean(o, axis=-1, keepdims=True)
    var = jnp.mean(jnp.square(o - mu), axis=-1, keepdims=True)
    on = (o - mu) * lax.rsqrt(var + EPS)
    return (on * jax.nn.silu(g.astype(jnp.float32))).astype(g.dtype)


def band(x, n_prev):
    b, l = x.shape[0], x.shape[1]
    nc = l // CHUNK
    xc = x.reshape((b, nc, CHUNK) + x.shape[2:])
    xp = jnp.concatenate([jnp.zeros((b, n_prev, CHUNK) + x.shape[2:], x.dtype), xc], axis=1)
    return jnp.concatenate([xp[:, j:j + nc] for j in range(n_prev + 1)], axis=2)


def band_valid(nc, n_prev):
    cidx = jnp.arange(nc)[:, None] - n_prev + (jnp.arange((n_prev + 1) * CHUNK) // CHUNK)[None, :]
    return cidx >= 0


def sink_attention(q, k, v, valid, sinks):
    s = jnp.einsum('bnqkgd,bntkd->bnkgqt', q, k).astype(jnp.float32) * (HEAD_DIM ** -0.5)
    s = jnp.where(valid[None, :, None, None, None, :], s, NEG_INF)
    sk = sinks.astype(jnp.float32)[None, None, :, :, None, None]
    m = jnp.maximum(jnp.max(s, axis=-1, keepdims=True), sk)
    p = jnp.exp(s - m)
    denom = jnp.sum(p, axis=-1, keepdims=True) + jnp.exp(sk - m)
    return jnp.einsum('bnkgqt,bntkd->bnqkgd', (p / denom).astype(v.dtype), v)


def rel_bias_matrix(table, tq, n_before, tk):
    dist = n_before + jnp.arange(tq)[:, None] - jnp.arange(tk)[None, :]
    idx = jnp.clip(dist, -REL_CLIP, REL_CLIP) + REL_CLIP
    return jnp.moveaxis(table[idx], -1, 0)


def bias_attention(q, k, v, valid, bias):
    s = jnp.einsum('bnqhd,bnthd->bnhqt', q, k).astype(jnp.float32) * (HEAD_DIM ** -0.5)
    s = s + bias.astype(jnp.float32)[None, None]
    s = jnp.where(valid[None, :, None, None, :], s, NEG_INF)
    p = jax.nn.softmax(s, axis=-1)
    return jnp.einsum('bnhqt,bnthd->bnqhd', p.astype(v.dtype), v)


def conv_ffn(h, prev, w_gate, w_up, conv_w, conv_b, w_down):
    g = h @ w_gate
    u = h @ w_up
    l = g.shape[1]
    gp = jnp.concatenate([prev.astype(g.dtype), g], axis=1)
    gc = conv_b
    for i in range(CONV_W):
        gc = gc + conv_w[i] * gp[:, i:i + l]
    y = (jax.nn.gelu(gc) * u) @ w_down
    return y, gp[:, -(CONV_W - 1):]


def mixer_prompt(hn, w_in, sinks, rel_table):
    b, l, _ = hn.shape
    nc = l // CHUNK
    qa, ka, va, ga, qb, kb, vb, qc, kc, vc = split_proj(hn @ w_in)
    pos = jnp.arange(l)
    qa = rotary(qa, pos)
    ka = rotary(ka, pos) * (HEAD_DIM ** -0.5)
    oa, s_ret = retention_prompt(qa, ka, va)
    oa = retention_out(oa, ga).reshape(b, l, H_RET * HEAD_DIM)
    qbc = qb.reshape(b, nc, CHUNK, KV_SWA, G_SWA, HEAD_DIM)
    ob = sink_attention(qbc, band(kb, SWA_PREV_CHUNKS), band(vb, SWA_PREV_CHUNKS),
                        band_valid(nc, SWA_PREV_CHUNKS), sinks).reshape(b, l, H_SWA * HEAD_DIM)
    qcc = qc.reshape(b, nc, CHUNK, H_CA, HEAD_DIM)
    bias = rel_bias_matrix(rel_table, CHUNK, CA_ROWS, (CA_PREV_CHUNKS + 1) * CHUNK)
    oc = bias_attention(qcc, band(kc, CA_PREV_CHUNKS), band(vc, CA_PREV_CHUNKS),
                        band_valid(nc, CA_PREV_CHUNKS), bias).reshape(b, l, H_CA * HEAD_DIM)
    mixed = jnp.concatenate([oa, ob, oc], axis=-1)
    rs = min(SWA_ROWS, l)
    rc = min(CA_ROWS, l)
    return mixed, kb[:, -rs:], vb[:, -rs:], kc[:, -rc:], vc[:, -rc:], s_ret.astype(hn.dtype)


def mixer_sample(hn, ck_b, cv_b, ck_c, cv_c, s_ret, w_in, sinks, rel_table):
    b, t, _ = hn.shape
    qa, ka, va, ga, qb, kb, vb, qc, kc, vc = split_proj(hn @ w_in)
    pos = PAST_LEN + jnp.arange(t)
    qa = rotary(qa, pos)
    ka = rotary(ka, pos) * (HEAD_DIM ** -0.5)
    oa, s_new = retention_block(qa.astype(jnp.float32), ka.astype(jnp.float32),
                                va.astype(jnp.float32), s_ret.astype(jnp.float32))
    oa = retention_out(oa, ga).reshape(b, t, H_RET * HEAD_DIM)
    kb_all = jnp.concatenate([ck_b.astype(kb.dtype), kb], axis=1)
    vb_all = jnp.concatenate([cv_b.astype(vb.dtype), vb], axis=1)
    ob = sink_attention(qb.reshape(b, 1, t, KV_SWA, G_SWA, HEAD_DIM), kb_all[:, None], vb_all[:, None],
                        jnp.ones((1, kb_all.shape[1]), bool), sinks).reshape(b, t, H_SWA * HEAD_DIM)
    lc = ck_c.shape[1]
    kc_all = jnp.concatenate([ck_c.astype(kc.dtype), kc], axis=1)
    vc_all = jnp.concatenate([cv_c.astype(vc.dtype), vc], axis=1)
    bias = rel_bias_matrix(rel_table, t, lc, lc + t)
    oc = bias_attention(qc[:, None], kc_all[:, None], vc_all[:, None],
                        jnp.ones((1, lc + t), bool), bias).reshape(b, t, H_CA * HEAD_DIM)
    mixed = jnp.concatenate([oa, ob, oc], axis=-1)
    rs = ck_b.shape[1]
    return (mixed, kb_all[:, -rs:], vb_all[:, -rs:], kc_all[:, -lc:], vc_all[:, -lc:],
            s_new.astype(s_ret.dtype))


def setup_inputs(seed: int = 0) -> dict:
    key = jax.random.key(seed)
    ks = jax.random.split(key, 24)
    f32 = jnp.float32

    def nrm(k, shape, scale):
        return jax.random.normal(k, shape, f32) * scale

    swa_rows = min(SWA_ROWS, PAST_LEN)
    ca_rows = min(CA_ROWS, PAST_LEN)
    return {
        "x_prompt": nrm(ks[0], (BATCH, SEQ, D_MODEL), 1.0),
        "x_sample": nrm(ks[1], (DEC_BATCH, DEC_SEQ, D_MODEL), 1.0),
        "cache_swa_k": nrm(ks[2], (DEPTH, DEC_BATCH, swa_rows, KV_SWA, HEAD_DIM), 1.0),
        "cache_swa_v": nrm(ks[3], (DEPTH, DEC_BATCH, swa_rows, KV_SWA, HEAD_DIM), 1.0),
        "cache_ca_k": nrm(ks[4], (DEPTH, DEC_BATCH, ca_rows, H_CA, HEAD_DIM), 1.0),
        "cache_ca_v": nrm(ks[5], (DEPTH, DEC_BATCH, ca_rows, H_CA, HEAD_DIM), 1.0),
        "state_ret": nrm(ks[6], (DEPTH, DEC_BATCH, H_RET, HEAD_DIM, HEAD_DIM), 1.0),
        "state_ffn_conv": nrm(ks[7], (DEPTH, DEC_BATCH, CONV_W - 1, D_FF), 1.0),
        "w_in": nrm(ks[8], (DEPTH, D_MODEL, IN_COLS), D_MODEL ** -0.5),
        "w_out": nrm(ks[9], (DEPTH, MIX_WIDTH, D_MODEL), MIX_WIDTH ** -0.5),
        "attn_sinks": nrm(ks[10], (DEPTH, H_SWA), 0.5),
        "rel_bias_table": nrm(ks[11], (DEPTH, N_REL, H_CA), 0.1),
        "norm_mix": 1.0 + nrm(ks[12], (DEPTH, D_MODEL), 0.01),
        "norm_ffn": 1.0 + nrm(ks[13], (DEPTH, D_MODEL), 0.01),
        "w_gate": nrm(ks[14], (DEPTH, D_MODEL, D_FF), D_MODEL ** -0.5),
        "w_up": nrm(ks[15], (DEPTH, D_MODEL, D_FF), D_MODEL ** -0.5),
        "conv_w": nrm(ks[16], (DEPTH, CONV_W, D_FF), CONV_W ** -0.5),
        "conv_b": nrm(ks[17], (DEPTH, D_FF), 0.01),
        "w_down": nrm(ks[18], (DEPTH, D_FF, D_MODEL), D_FF ** -0.5),
        "norm_final": 1.0 + nrm(ks[19], (D_MODEL,), 0.01),
    }


def reference(x_prompt, x_sample, cache_swa_k, cache_swa_v, cache_ca_k, cache_ca_v, state_ret,
              state_ffn_conv, w_in, w_out, attn_sinks, rel_bias_table, norm_mix, norm_ffn, w_gate,
              w_up, conv_w, conv_b, w_down, norm_final):
    hp = x_prompt
    hs = x_sample
    sp = [[] for _ in range(6)]
    ss = [[] for _ in range(6)]
    for l in range(DEPTH):
        sinks_l = attn_sinks[l].reshape(KV_SWA, G_SWA)
        mp, kbp, vbp, kcp, vcp, srp = mixer_prompt(rms_norm(hp, norm_mix[l]), w_in[l], sinks_l,
                                                   rel_bias_table[l])
        hp = hp + mp @ w_out[l]
        zero_conv = jnp.zeros((hp.shape[0], CONV_W - 1, D_FF), hp.dtype)
        fp, cvp = conv_ffn(rms_norm(hp, norm_ffn[l]), zero_conv, w_gate[l], w_up[l], conv_w[l],
                           conv_b[l], w_down[l])
        hp = hp + fp
        ms, kbs, vbs, kcs, vcs, srs = mixer_sample(rms_norm(hs, norm_mix[l]), cache_swa_k[l], cache_swa_v[l],
                                                   cache_ca_k[l], cache_ca_v[l], state_ret[l], w_in[l],
                                                   sinks_l, rel_bias_table[l])
        hs = hs + ms @ w_out[l]
        fs, cvs = conv_ffn(rms_norm(hs, norm_ffn[l]), state_ffn_conv[l], w_gate[l], w_up[l], conv_w[l],
                           conv_b[l], w_down[l])
        hs = hs + fs
        for lst, val in zip(sp, (kbp, vbp, kcp, vcp, srp, cvp)):
            lst.append(val)
        for lst, val in zip(ss, (kbs, vbs, kcs, vcs, srs, cvs)):
            lst.append(val)
    y_prompt = rms_norm(hp, norm_final)
    y_sample = rms_norm(hs, norm_final)
    swa_k_p, swa_v_p, ca_k_p, ca_v_p, ret_p, conv_p = [jnp.stack(v, axis=0) for v in sp]
    swa_k_s, swa_v_s, ca_k_s, ca_v_s, ret_s, conv_s = [jnp.stack(v, axis=0) for v in ss]
    return (y_prompt, y_sample, swa_k_p, swa_v_p, ca_k_p, ca_v_p, ret_p, conv_p,
            swa_k_s, swa_v_s, ca_k_s, ca_v_s, ret_s, conv_s)
```

```python
import functools

import jax
import jax.numpy as jnp
from jax import lax
from jax.experimental import pallas as pl
from jax.experimental.pallas import tpu as pltpu

F32 = jnp.float32
BF16 = jnp.bfloat16

D_MODEL = 1024
CHUNK = 64
HEAD_DIM = 64
H_RET = 6
H_SWA = 6
KV_SWA = 2
G_SWA = H_SWA // KV_SWA
H_CA = 4
SWA_ROWS = 128
CA_ROWS = 512
REL_CLIP = 128
N_REL = 2 * REL_CLIP + 1
D_FF = 2816
CONV_W = 3
EPS = 1e-6
ROPE_BASE = 10000.0
NEG_INF = -1e30
PAST_LEN = 4096

RET_W = H_RET * HEAD_DIM
RET_COLS = 4 * RET_W
SWA_QW = H_SWA * HEAD_DIM
SWA_KW = KV_SWA * HEAD_DIM
SWA_COLS = SWA_QW + 2 * SWA_KW
CA_W = H_CA * HEAD_DIM
CA_COLS = 3 * CA_W
IN_COLS = RET_COLS + SWA_COLS + CA_COLS

LANES = 128
SUBLANES = 8
VMEM_LIMIT_BYTES = 56 * 1024 * 1024

QBLK = 128
SWA_WIN = QBLK + SWA_ROWS
CA_WIN = QBLK + CA_ROWS
CA_WIN_TILES = CA_WIN // LANES
CA_EXT_TILES = (CA_WIN + CA_ROWS) // LANES
BIAS_ROW = CA_EXT_TILES * LANES + LANES


def _params(semantics):
    return pltpu.CompilerParams(dimension_semantics=semantics,
                                vmem_limit_bytes=VMEM_LIMIT_BYTES)


def _rms(x, gain):
    ms = jnp.mean(x * x, axis=-1, keepdims=True)
    return (x * lax.rsqrt(ms + EPS)) * gain


def _dot(a, b):
    return jnp.dot(a, b, preferred_element_type=F32)


def _dot_nt(a, b):
    return lax.dot_general(a, b, (((1,), (1,)), ((), ())), preferred_element_type=F32)


def _dot_tn(a, b):
    return lax.dot_general(a, b, (((0,), (0,)), ((), ())), preferred_element_type=F32)


def _inproj_kernel(x_ref, g_ref, w_ref, ret_ref, swa_ref, ca_ref):
    h = _rms(x_ref[...], g_ref[...]).astype(BF16)
    ret_ref[...] = _dot(h, w_ref[:, 0:RET_COLS])
    swa_ref[...] = _dot(h, w_ref[:, RET_COLS:RET_COLS + SWA_COLS])
    ca_ref[...] = _dot(h, w_ref[:, RET_COLS + SWA_COLS:IN_COLS])


def _inproj(x, gain, w_in, layer, tm):
    m = x.shape[0]
    row = lambda i: (i, 0)
    return pl.pallas_call(
        _inproj_kernel,
        grid=(m // tm,),
        in_specs=[
            pl.BlockSpec((tm, D_MODEL), row),
            pl.BlockSpec((None, 1, D_MODEL), lambda i: (layer, 0, 0)),
            pl.BlockSpec((None, D_MODEL, IN_COLS), lambda i: (layer, 0, 0)),
        ],
        out_specs=[
            pl.BlockSpec((tm, RET_COLS), row),
            pl.BlockSpec((tm, SWA_COLS), row),
            pl.BlockSpec((tm, CA_COLS), row),
        ],
        out_shape=[
            jax.ShapeDtypeStruct((m, RET_COLS), F32),
            jax.ShapeDtypeStruct((m, SWA_COLS), F32),
            jax.ShapeDtypeStruct((m, CA_COLS), F32),
        ],
        compiler_params=_params(("parallel",)),
        name="inproj",
    )(x, gain, w_in)


def _ret_log_decay():
    return jnp.log(1.0 - 2.0 ** (-5.0 - jnp.arange(H_RET, dtype=F32)))


def _rope_tables(pos):
    half = HEAD_DIM // 2
    inv = 1.0 / (ROPE_BASE ** (jnp.arange(half, dtype=F32) / half))
    ang = pos.astype(F32)[:, None] * inv[None, :]
    cos = jnp.cos(ang)
    sin = jnp.sin(ang)
    cos_h = jnp.concatenate([cos, cos], axis=-1)
    sin_h = jnp.concatenate([-sin, sin], axis=-1)
    return jnp.tile(cos_h, (1, H_RET)), jnp.tile(sin_h, (1, H_RET))


def _retention_tables(t_len):
    log_g = _ret_log_decay()
    t = jnp.arange(t_len, dtype=F32)
    diff = t[:, None] - t[None, :]
    decay = jnp.where(diff >= 0, jnp.exp(jnp.maximum(diff, 0.0)[None] * log_g[:, None, None]), 0.0)
    q_decay = jnp.exp((t + 1.0)[:, None] * log_g[None, :])
    k_decay = jnp.exp((t_len - 1.0 - t)[:, None] * log_g[None, :])
    blk = jnp.exp(t_len * log_g)
    return (decay, jnp.repeat(q_decay, HEAD_DIM, axis=1), jnp.repeat(k_decay, HEAD_DIM, axis=1), blk)


def _rope(x, cos, sin):
    n = x.shape[-1]
    lane = lax.broadcasted_iota(jnp.int32, x.shape, 1)
    first_half = (lane & (HEAD_DIM - 1)) < (HEAD_DIM // 2)
    partner = jnp.where(first_half,
                        pltpu.roll(x, n - HEAD_DIM // 2, axis=1),
                        pltpu.roll(x, HEAD_DIM // 2, axis=1))
    return x * cos + partner * sin


def _retention_block(q, k, v, gate, cos, sin, dec_ref, q_decay, k_decay, blk_ref, state):
    q = _rope(q, cos, sin)
    k = _rope(k, cos, sin) * (HEAD_DIM ** -0.5)
    qb = q.astype(BF16)
    kb = k.astype(BF16)
    kdb = (k * k_decay).astype(BF16)
    vb = v.astype(BF16)
    outs = []
    new_state = []
    for h in range(H_RET):
        sl = slice(h * HEAD_DIM, (h + 1) * HEAD_DIM)
        qh, kh, vh = qb[:, sl], kb[:, sl], vb[:, sl]
        scores = _dot_nt(qh, kh) * dec_ref[h]
        inner = _dot(scores.astype(BF16), vh)
        cross = _dot(qh, state[h].astype(BF16)) * q_decay[:, sl]
        o = inner + cross
        mu = jnp.mean(o, axis=-1, keepdims=True)
        var = jnp.mean(jnp.square(o - mu), axis=-1, keepdims=True)
        outs.append((o - mu) * lax.rsqrt(var + EPS))
        new_state.append(blk_ref[h] * state[h] + _dot_tn(kdb[:, sl], vh))
    on = jnp.concatenate(outs, axis=-1)
    return on * (gate * jax.nn.sigmoid(gate)), new_state


def _ret_prompt_kernel(blk_ref, q_ref, k_ref, v_ref, g_ref, cos_ref, sin_ref, dec_ref, qd_ref,
                       kd_ref, o_ref, s_ref, state_scr):
    t = pl.program_id(1)

    @pl.when(t == 0)
    def _():
        state_scr[...] = jnp.zeros_like(state_scr)

    state = [state_scr[h] for h in range(H_RET)]
    out, new_state = _retention_block(q_ref[...], k_ref[...], v_ref[...], g_ref[...], cos_ref[...],
                                      sin_ref[...], dec_ref, qd_ref[...], kd_ref[...], blk_ref,
                                      state)
    o_ref[...] = out.astype(o_ref.dtype)
    for h in range(H_RET):
        state_scr[h] = new_state[h]

    @pl.when(t == pl.num_programs(1) - 1)
    def _():
        for h in range(H_RET):
            s_ref[h] = new_state[h]


def _ret_prompt(ret, batch, seq, tables, rope):
    tb = QBLK
    nt = seq // tb
    decay, q_decay, k_decay, blk = tables
    cos, sin = rope
    col = lambda c: pl.BlockSpec((tb, RET_W), lambda b, t: (b * nt + t, c))
    full2 = pl.BlockSpec((tb, RET_W), lambda b, t: (0, 0))
    return pl.pallas_call(
        _ret_prompt_kernel,
        grid=(batch, nt),
        in_specs=[
            pl.BlockSpec(memory_space=pltpu.SMEM),
            col(0), col(1), col(2), col(3),
            pl.BlockSpec((tb, RET_W), lambda b, t: (t, 0)),
            pl.BlockSpec((tb, RET_W), lambda b, t: (t, 0)),
            pl.BlockSpec((H_RET, tb, tb), lambda b, t: (0, 0, 0)),
            full2, full2,
        ],
        out_specs=[
            pl.BlockSpec((tb, RET_W), lambda b, t: (b * nt + t, 0)),
            pl.BlockSpec((None, H_RET, HEAD_DIM, HEAD_DIM), lambda b, t: (b, 0, 0, 0)),
        ],
        out_shape=[
            jax.ShapeDtypeStruct((batch * seq, RET_W), BF16),
            jax.ShapeDtypeStruct((batch, H_RET, HEAD_DIM, HEAD_DIM), F32),
        ],
        scratch_shapes=[pltpu.VMEM((H_RET, HEAD_DIM, HEAD_DIM), F32)],
        compiler_params=_params(("parallel", "arbitrary")),
        name="ret_prompt",
    )(blk, ret, ret, ret, ret, cos, sin, decay, q_decay, k_decay)


def _swa_prompt_kernel(sink_ref, q_ref, k_ref, v_ref, o_ref):
    i = pl.program_id(1)
    start = pl.multiple_of(jnp.maximum(i * QBLK - SWA_ROWS, 0), QBLK)
    kw = k_ref[pl.ds(start, SWA_WIN), :].astype(BF16)
    vw = v_ref[pl.ds(start, SWA_WIN), :].astype(BF16)
    q = q_ref[...].astype(BF16)
    rows = G_SWA * QBLK
    row = lax.broadcasted_iota(jnp.int32, (rows, SWA_WIN), 0)
    col = lax.broadcasted_iota(jnp.int32, (rows, SWA_WIN), 1)
    q_chunk = i * (QBLK // CHUNK) + ((row & (QBLK - 1)) // CHUNK)
    k_chunk = start // CHUNK + col // CHUNK
    behind = q_chunk - k_chunk
    n_prev = SWA_ROWS // CHUNK
    valid = jnp.abs(2 * behind - n_prev) <= n_prev
    row1 = lax.broadcasted_iota(jnp.int32, (rows, 1), 0)
    heads = []
    for kv in range(KV_SWA):
        ksl = slice(kv * HEAD_DIM, (kv + 1) * HEAD_DIM)
        q3 = jnp.concatenate(
            [q[:, (kv * G_SWA + g) * HEAD_DIM:(kv * G_SWA + g + 1) * HEAD_DIM] for g in range(G_SWA)],
            axis=0)
        s = _dot_nt(q3, kw[:, ksl]) * (HEAD_DIM ** -0.5)
        s = jnp.where(valid, s, NEG_INF)
        sink = jnp.full((rows, 1), sink_ref[kv * G_SWA + G_SWA - 1], F32)
        for g in range(G_SWA - 2, -1, -1):
            sink = jnp.where(row1 < (g + 1) * QBLK, sink_ref[kv * G_SWA + g], sink)
        m = jnp.maximum(jnp.max(s, axis=-1, keepdims=True), sink)
        p = jnp.exp(s - m)
        denom = jnp.sum(p, axis=-1, keepdims=True) + jnp.exp(sink - m)
        o = _dot(p.astype(BF16), vw[:, ksl]) / denom
        heads += [o[g * QBLK:(g + 1) * QBLK] for g in range(G_SWA)]
    o_ref[...] = jnp.concatenate(heads, axis=-1).astype(o_ref.dtype)


def _swa_prompt(swa, sinks, batch, seq):
    nq = seq // QBLK
    k_col = SWA_QW // SWA_KW
    return pl.pallas_call(
        _swa_prompt_kernel,
        grid=(batch, nq),
        in_specs=[
            pl.BlockSpec(memory_space=pltpu.SMEM),
            pl.BlockSpec((QBLK, SWA_QW), lambda b, i: (b * nq + i, 0)),
            pl.BlockSpec((seq, SWA_KW), lambda b, i: (b, k_col)),
            pl.BlockSpec((seq, SWA_KW), lambda b, i: (b, k_col + 1)),
        ],
        out_specs=pl.BlockSpec((QBLK, SWA_QW), lambda b, i: (b * nq + i, 0)),
        out_shape=jax.ShapeDtypeStruct((batch * seq, SWA_QW), BF16),
        compiler_params=_params(("parallel", "arbitrary")),
        name="swa_prompt",
    )(sinks, swa, swa, swa)


def _bias_kernel(t_ref, o_ref):
    n_pad = t_ref.shape[-1]
    r = lax.broadcasted_iota(jnp.int32, (n_pad, BIAS_ROW), 0)
    w = lax.broadcasted_iota(jnp.int32, (n_pad, BIAS_ROW), 1)
    idx = jnp.clip(BIAS_ROW // 2 - 1 - w, -REL_CLIP, REL_CLIP) + REL_CLIP
    onehot = jnp.where(r == idx, 1.0, 0.0).astype(BF16)
    t = t_ref[...]
    t_hi = t.astype(BF16)
    rem = t - t_hi.astype(F32)
    t_mid = rem.astype(BF16)
    t_lo = (rem - t_mid.astype(F32)).astype(BF16)
    f = (_dot(t_hi, onehot) + _dot(t_mid, onehot)) + _dot(t_lo, onehot)
    ext = CA_EXT_TILES * LANES
    row = lax.broadcasted_iota(jnp.int32, (QBLK, ext), 0)
    colj = lax.broadcasted_iota(jnp.int32, (QBLK, ext), 1)
    ahead = colj // CHUNK - row // CHUNK
    n_prev = CA_ROWS // CHUNK
    valid = jnp.abs(2 * ahead - n_prev) <= n_prev
    for h in range(H_CA):
        x = jnp.broadcast_to(f[h:h + 1, :], (QBLK, BIAS_ROW))
        y = pltpu.roll(x, BIAS_ROW - (QBLK - 1), axis=1, stride=1, stride_axis=0)
        y = jnp.where(valid, y[:, 0:ext], NEG_INF)
        for c in range(CA_EXT_TILES):
            o_ref[h, c] = y[:, c * LANES:(c + 1) * LANES]


def _build_bias(rel_bias_table):
    depth = rel_bias_table.shape[0]
    n_pad = 3 * LANES
    t = jnp.transpose(rel_bias_table, (0, 2, 1))
    t = jnp.pad(t, ((0, 0), (0, SUBLANES - H_CA), (0, n_pad - N_REL)))
    return pl.pallas_call(
        _bias_kernel,
        grid=(depth,),
        in_specs=[pl.BlockSpec((None, SUBLANES, n_pad), lambda l: (l, 0, 0))],
        out_specs=pl.BlockSpec((None, H_CA, CA_EXT_TILES, QBLK, LANES), lambda l: (l, 0, 0, 0, 0)),
        out_shape=jax.ShapeDtypeStruct((depth, H_CA, CA_EXT_TILES, QBLK, LANES), F32),
        compiler_params=_params(("parallel",)),
        name="ca_bias",
    )(t)


def _ca_prompt_kernel(q_ref, k_ref, v_ref, bias_ref, o_ref):
    i = pl.program_id(1)
    ideal = i * QBLK - CA_ROWS
    start = pl.multiple_of(jnp.maximum(ideal, 0), QBLK)
    off = (start - ideal) // LANES
    kw = k_ref[pl.ds(start, CA_WIN), :].astype(BF16)
    vw = v_ref[pl.ds(start, CA_WIN), :].astype(BF16)
    q = q_ref[...].astype(BF16)
    heads = []
    for h in range(H_CA):
        sl = slice(h * HEAD_DIM, (h + 1) * HEAD_DIM)
        bias = jnp.concatenate([bias_ref[h, off + c] for c in range(CA_WIN_TILES)], axis=-1)
        s = _dot_nt(q[:, sl], kw[:, sl]) * (HEAD_DIM ** -0.5) + bias
        m = jnp.max(s, axis=-1, keepdims=True)
        p = jnp.exp(s - m)
        denom = jnp.sum(p, axis=-1, keepdims=True)
        heads.append(_dot(p.astype(BF16), vw[:, sl]) / denom)
    o_ref[...] = jnp.concatenate(heads, axis=-1).astype(o_ref.dtype)


def _ca_prompt(ca, bias, layer, batch, seq):
    nq = seq // QBLK
    return pl.pallas_call(
        _ca_prompt_kernel,
        grid=(batch, nq),
        in_specs=[
            pl.BlockSpec((QBLK, CA_W), lambda b, i: (b * nq + i, 0)),
            pl.BlockSpec((seq, CA_W), lambda b, i: (b, 1)),
            pl.BlockSpec((seq, CA_W), lambda b, i: (b, 2)),
            pl.BlockSpec((None, H_CA, CA_EXT_TILES, QBLK, LANES), lambda b, i: (layer, 0, 0, 0, 0)),
        ],
        out_specs=pl.BlockSpec((QBLK, CA_W), lambda b, i: (b * nq + i, 0)),
        out_shape=jax.ShapeDtypeStruct((batch * seq, CA_W), BF16),
        compiler_params=_params(("parallel", "arbitrary")),
        name="ca_prompt",
    )(ca, ca, ca, bias)


def _mixer_sample_kernel(sink_ref, blk_ref, ret_ref, swa_ref, ca_ref, cos_ref, sin_ref, dec_ref,
                         qd_ref, kd_ref, sk_ref, sv_ref, ck_ref, cv_ref, st_ref, bias_ref,
                         oa_ref, ob_ref, oc_ref, snew_ref, *, t_len):
    col = lambda c: ret_ref[:, c * RET_W:(c + 1) * RET_W]
    state = [st_ref[h] for h in range(H_RET)]
    out, new_state = _retention_block(col(0), col(1), col(2), col(3), cos_ref[...], sin_ref[...],
                                      dec_ref, qd_ref[...], kd_ref[...], blk_ref, state)
    oa_ref[...] = out.astype(oa_ref.dtype)
    for h in range(H_RET):
        snew_ref[h] = new_state[h]

    q = swa_ref[:, 0:SWA_QW].astype(BF16)
    kn = swa_ref[:, SWA_QW:SWA_QW + SWA_KW].astype(BF16)
    vn = swa_ref[:, SWA_QW + SWA_KW:SWA_COLS].astype(BF16)
    kc = sk_ref[...].astype(BF16)
    vc = sv_ref[...].astype(BF16)
    nrow = G_SWA * t_len
    row1 = lax.broadcasted_iota(jnp.int32, (nrow, 1), 0)
    heads = []
    for kv in range(KV_SWA):
        ksl = slice(kv * HEAD_DIM, (kv + 1) * HEAD_DIM)
        q3 = jnp.concatenate(
            [q[:, (kv * G_SWA + g) * HEAD_DIM:(kv * G_SWA + g + 1) * HEAD_DIM] for g in range(G_SWA)],
            axis=0)
        s_c = _dot_nt(q3, kc[:, ksl]) * (HEAD_DIM ** -0.5)
        s_n = _dot_nt(q3, kn[:, ksl]) * (HEAD_DIM ** -0.5)
        sink = jnp.full((nrow, 1), sink_ref[kv * G_SWA + G_SWA - 1], F32)
        for g in range(G_SWA - 2, -1, -1):
            sink = jnp.where(row1 < (g + 1) * t_len, sink_ref[kv * G_SWA + g], sink)
        m = jnp.maximum(jnp.maximum(jnp.max(s_c, axis=-1, keepdims=True),
                                    jnp.max(s_n, axis=-1, keepdims=True)), sink)
        p_c = jnp.exp(s_c - m)
        p_n = jnp.exp(s_n - m)
        denom = (jnp.sum(p_c, axis=-1, keepdims=True) + jnp.sum(p_n, axis=-1, keepdims=True)
                 + jnp.exp(sink - m))
        o = (_dot(p_c.astype(BF16), vc[:, ksl]) + _dot(p_n.astype(BF16), vn[:, ksl])) / denom
        heads += [o[g * t_len:(g + 1) * t_len] for g in range(G_SWA)]
    ob_ref[...] = jnp.concatenate(heads, axis=-1).astype(ob_ref.dtype)

    q = ca_ref[:, 0:CA_W].astype(BF16)
    kn = ca_ref[:, CA_W:2 * CA_W].astype(BF16)
    vn = ca_ref[:, 2 * CA_W:CA_COLS].astype(BF16)
    kc = ck_ref[...].astype(BF16)
    vc = cv_ref[...].astype(BF16)
    n_tiles = CA_ROWS // LANES
    heads = []
    for h in range(H_CA):
        sl = slice(h * HEAD_DIM, (h + 1) * HEAD_DIM)
        bias_c = jnp.concatenate([bias_ref[h, c, 0:t_len, :] for c in range(n_tiles)], axis=-1)
        bias_n = bias_ref[h, n_tiles, 0:t_len, 0:t_len]
        s_c = _dot_nt(q[:, sl], kc[:, sl]) * (HEAD_DIM ** -0.5) + bias_c
        s_n = _dot_nt(q[:, sl], kn[:, sl]) * (HEAD_DIM ** -0.5) + bias_n
        m = jnp.maximum(jnp.max(s_c, axis=-1, keepdims=True), jnp.max(s_n, axis=-1, keepdims=True))
        p_c = jnp.exp(s_c - m)
        p_n = jnp.exp(s_n - m)
        denom = jnp.sum(p_c, axis=-1, keepdims=True) + jnp.sum(p_n, axis=-1, keepdims=True)
        heads.append((_dot(p_c.astype(BF16), vc[:, sl]) + _dot(p_n.astype(BF16), vn[:, sl])) / denom)
    oc_ref[...] = jnp.concatenate(heads, axis=-1).astype(oc_ref.dtype)


def _mixer_sample(ret, swa, ca, sinks, tables, rope, cache_sk, cache_sv, cache_ck, cache_cv,
                  state_ret, bias, layer, n_seq, t_len):
    m = n_seq * t_len
    decay, q_decay, k_decay, blk = tables
    cos, sin = rope
    whole = lambda shape: pl.BlockSpec(shape, lambda b: (0,) * len(shape))
    seq_rows = lambda width: pl.BlockSpec((t_len, width), lambda b: (b, 0))
    cache = lambda rows, width: pl.BlockSpec((None, None, rows, width), lambda b: (layer, b, 0, 0))
    return pl.pallas_call(
        functools.partial(_mixer_sample_kernel, t_len=t_len),
        grid=(n_seq,),
        in_specs=[
            pl.BlockSpec(memory_space=pltpu.SMEM),
            pl.BlockSpec(memory_space=pltpu.SMEM),
            seq_rows(RET_COLS), seq_rows(SWA_COLS), seq_rows(CA_COLS),
            whole((t_len, RET_W)), whole((t_len, RET_W)),
            whole((H_RET, t_len, t_len)),
            whole((t_len, RET_W)), whole((t_len, RET_W)),
            cache(SWA_ROWS, SWA_KW), cache(SWA_ROWS, SWA_KW),
            cache(CA_ROWS, CA_W), cache(CA_ROWS, CA_W),
            pl.BlockSpec((None, None, H_RET, HEAD_DIM, HEAD_DIM), lambda b: (layer, b, 0, 0, 0)),
            pl.BlockSpec((None, H_CA, CA_EXT_TILES, QBLK, LANES), lambda b: (layer, 0, 0, 0, 0)),
        ],
        out_specs=[
            seq_rows(RET_W), seq_rows(SWA_QW), seq_rows(CA_W),
            pl.BlockSpec((None, H_RET, HEAD_DIM, HEAD_DIM), lambda b: (b, 0, 0, 0)),
        ],
        out_shape=[
            jax.ShapeDtypeStruct((m, RET_W), BF16),
            jax.ShapeDtypeStruct((m, SWA_QW), BF16),
            jax.ShapeDtypeStruct((m, CA_W), BF16),
            jax.ShapeDtypeStruct((n_seq, H_RET, HEAD_DIM, HEAD_DIM), F32),
        ],
        compiler_params=_params(("parallel",)),
        name="mixer_sample",
    )(sinks, blk, ret, swa, ca, cos, sin, decay, q_decay, k_decay, cache_sk, cache_sv, cache_ck,
      cache_cv, state_ret, bias)


def _outproj_kernel(x_ref, a_ref, b_ref, c_ref, w_ref, o_ref):
    acc = x_ref[...] + _dot(a_ref[...], w_ref[0:RET_W, :])
    acc = acc + _dot(b_ref[...], w_ref[RET_W:RET_W + SWA_QW, :])
    o_ref[...] = acc + _dot(c_ref[...], w_ref[RET_W + SWA_QW:D_MODEL, :])


def _outproj(x, oa, ob, oc, w_out, layer, tm):
    m = x.shape[0]
    row = lambda i: (i, 0)
    return pl.pallas_call(
        _outproj_kernel,
        grid=(m // tm,),
        in_specs=[
            pl.BlockSpec((tm, D_MODEL), row),
            pl.BlockSpec((tm, RET_W), row),
            pl.BlockSpec((tm, SWA_QW), row),
            pl.BlockSpec((tm, CA_W), row),
            pl.BlockSpec((None, D_MODEL, D_MODEL), lambda i: (layer, 0, 0)),
        ],
        out_specs=pl.BlockSpec((tm, D_MODEL), row),
        out_shape=jax.ShapeDtypeStruct((m, D_MODEL), F32),
        compiler_params=_params(("parallel",)),
        name="outproj",
    )(x, oa, ob, oc, w_out)


def _ffn_kernel(*refs, tm, pad, shift, tiles_per_seq, final_norm):
    if final_norm:
        (x_ref, gain_ref, wg_ref, wu_ref, wd_ref, cw_ref, cb_ref, prev_ref, gf_ref,
         o_ref, st_ref, hn_scr, acc_scr, g_scr, carry_scr) = refs
    else:
        (x_ref, gain_ref, wg_ref, wu_ref, wd_ref, cw_ref, cb_ref, prev_ref,
         o_ref, st_ref, hn_scr, acc_scr, g_scr, carry_scr) = refs
    i = pl.program_id(0)
    j = pl.program_id(1)

    @pl.when(j == 0)
    def _():
        hn_scr[...] = _rms(x_ref[...], gain_ref[...]).astype(BF16)
        acc_scr[...] = jnp.zeros_like(acc_scr)

    if tiles_per_seq == 1:
        g_scr[0:pad, :] = prev_ref[...]
    else:
        seq_start = (i % tiles_per_seq) == 0

        @pl.when(seq_start)
        def _():
            g_scr[0:pad, :] = prev_ref[...]

        @pl.when(jnp.logical_not(seq_start))
        def _():
            g_scr[0:pad, :] = carry_scr[j]

    hn = hn_scr[...]
    g = _dot(hn, wg_ref[...])
    g_scr[pad:pad + tm, :] = g
    u = _dot(hn, wu_ref[...])
    gc = cb_ref[...] + cw_ref[0:1, :] * g_scr[pad - 2 * shift:pad - 2 * shift + tm, :]
    gc = gc + cw_ref[1:2, :] * g_scr[pad - shift:pad - shift + tm, :]
    gc = gc + cw_ref[2:3, :] * g
    act = (jax.nn.gelu(gc) * u).astype(BF16)
    acc_scr[...] += _dot(act, wd_ref[...])
    tail = g_scr[tm:tm + pad, :]
    if tiles_per_seq == 1:
        st_ref[i, j] = tail
    else:
        carry_scr[j] = tail

        @pl.when((i % tiles_per_seq) == tiles_per_seq - 1)
        def _():
            st_ref[i // tiles_per_seq, j] = tail

    @pl.when(j == pl.num_programs(1) - 1)
    def _():
        y = x_ref[...] + acc_scr[...]
        if final_norm:
            y = _rms(y, gf_ref[...])
        o_ref[...] = y


def _ffn(x, gain, w_gate, w_up, w_down, conv_w, conv_b, prev, layer, tm, tf, shift, tiles_per_seq,
         final_gain=None):
    m = x.shape[0]
    n_groups, pad, _ = prev.shape
    nj = D_FF // tf
    final_norm = final_gain is not None
    in_specs = [
        pl.BlockSpec((tm, D_MODEL), lambda i, j: (i, 0)),
        pl.BlockSpec((None, 1, D_MODEL), lambda i, j: (layer, 0, 0)),
        pl.BlockSpec((None, D_MODEL, tf), lambda i, j: (layer, 0, j)),
        pl.BlockSpec((None, D_MODEL, tf), lambda i, j: (layer, 0, j)),
        pl.BlockSpec((None, tf, D_MODEL), lambda i, j: (layer, j, 0)),
        pl.BlockSpec((None, CONV_W, tf), lambda i, j: (layer, 0, j)),
        pl.BlockSpec((None, 1, tf), lambda i, j: (layer, 0, j)),
        pl.BlockSpec((None, pad, tf), lambda i, j: (i // tiles_per_seq, 0, j)),
    ]
    args = [x, gain, w_gate, w_up, w_down, conv_w, conv_b, prev]
    if final_norm:
        in_specs.append(pl.BlockSpec((1, D_MODEL), lambda i, j: (0, 0)))
        args.append(final_gain)
    y, st = pl.pallas_call(
        functools.partial(_ffn_kernel, tm=tm, pad=pad, shift=shift, tiles_per_seq=tiles_per_seq,
                          final_norm=final_norm),
        grid=(m // tm, nj),
        in_specs=in_specs,
        out_specs=[
            pl.BlockSpec((tm, D_MODEL), lambda i, j: (i, 0)),
            pl.BlockSpec((n_groups, nj, pad, tf), lambda i, j: (0, 0, 0, 0)),
        ],
        out_shape=[
            jax.ShapeDtypeStruct((m, D_MODEL), F32),
            jax.ShapeDtypeStruct((n_groups, nj, pad, tf), F32),
        ],
        scratch_shapes=[
            pltpu.VMEM((tm, D_MODEL), BF16),
            pltpu.VMEM((tm, D_MODEL), F32),
            pltpu.VMEM((pad + tm, tf), F32),
            pltpu.VMEM((nj, pad, tf), F32),
        ],
        compiler_params=_params(("arbitrary", "arbitrary")),
        name="ffn",
    )(*args)
    return y, jnp.transpose(st, (0, 2, 1, 3)).reshape(n_groups, pad, D_FF)


def kernel(x_prompt, x_sample, cache_swa_k, cache_swa_v, cache_ca_k, cache_ca_v, state_ret,
           state_ffn_conv, w_in, w_out, attn_sinks, rel_bias_table, norm_mix, norm_ffn, w_gate,
           w_up, conv_w, conv_b, w_down, norm_final):
    batch, seq, _ = x_prompt.shape
    n_seq, t_len, _ = x_sample.shape
    depth = w_in.shape[0]
    assert seq % QBLK == 0 and seq >= CA_WIN and t_len % SUBLANES == 0
    assert cache_swa_k.shape[2] == SWA_ROWS and cache_ca_k.shape[2] == CA_ROWS

    w_in_b = w_in.astype(BF16)
    w_out_b = w_out.astype(BF16)
    w_gate_b = w_gate.astype(BF16)
    w_up_b = w_up.astype(BF16)
    w_down_b = w_down.astype(BF16)
    g_mix = norm_mix.reshape(depth, 1, D_MODEL)
    g_ffn = norm_ffn.reshape(depth, 1, D_MODEL)
    g_fin = norm_final.reshape(1, D_MODEL)
    cb = conv_b.reshape(depth, 1, D_FF)

    rope_p = _rope_tables(jnp.arange(seq))
    rope_s = _rope_tables(PAST_LEN + jnp.arange(t_len))
    tab_p = _retention_tables(QBLK)
    tab_s = _retention_tables(t_len)
    bias = _build_bias(rel_bias_table)

    csk = cache_swa_k.reshape(depth, n_seq, SWA_ROWS, SWA_KW)
    csv = cache_swa_v.reshape(depth, n_seq, SWA_ROWS, SWA_KW)
    cck = cache_ca_k.reshape(depth, n_seq, CA_ROWS, CA_W)
    ccv = cache_ca_v.reshape(depth, n_seq, CA_ROWS, CA_W)

    hp = x_prompt.reshape(batch * seq, D_MODEL)
    m_s = n_seq * t_len
    hs = x_sample.reshape(m_s, D_MODEL)

    pad_p = SUBLANES
    pad_s = (CONV_W - 1) * n_seq
    prev_p = jnp.zeros((batch, pad_p, D_FF), F32)
    tm_p = 1024
    tiles_per_seq = seq // tm_p

    outs_p = [[] for _ in range(6)]
    outs_s = [[] for _ in range(6)]
    for l in range(depth):
        last = l == depth - 1
        ret, swa, ca = _inproj(hp, g_mix, w_in_b, l, tm=512)
        oa, s_ret = _ret_prompt(ret, batch, seq, tab_p, rope_p)
        ob = _swa_prompt(swa, attn_sinks[l], batch, seq)
        oc = _ca_prompt(ca, bias, l, batch, seq)
        hp = _outproj(hp, oa, ob, oc, w_out_b, l, tm=512)
        hp, conv_st = _ffn(hp, g_ffn, w_gate_b, w_up_b, w_down_b, conv_w, cb, prev_p, l, tm=tm_p,
                           tf=256, shift=1, tiles_per_seq=tiles_per_seq,
                           final_gain=g_fin if last else None)
        swa3 = swa.reshape(batch, seq, SWA_COLS)
        ca3 = ca.reshape(batch, seq, CA_COLS)
        outs_p[0].append(swa3[:, seq - SWA_ROWS:, SWA_QW:SWA_QW + SWA_KW])
        outs_p[1].append(swa3[:, seq - SWA_ROWS:, SWA_QW + SWA_KW:])
        outs_p[2].append(ca3[:, seq - CA_ROWS:, CA_W:2 * CA_W])
        outs_p[3].append(ca3[:, seq - CA_ROWS:, 2 * CA_W:])
        outs_p[4].append(s_ret)
        outs_p[5].append(conv_st[:, pad_p - (CONV_W - 1):, :])

        ret, swa, ca = _inproj(hs, g_mix, w_in_b, l, tm=m_s // 2)
        oa, ob, oc, s_new = _mixer_sample(ret, swa, ca, attn_sinks[l], tab_s, rope_s, csk, csv, cck,
                                          ccv, state_ret, bias, l, n_seq, t_len)
        hs = _outproj(hs, oa, ob, oc, w_out_b, l, tm=m_s // 2)
        hs_tm = jnp.transpose(hs.reshape(n_seq, t_len, D_MODEL), (1, 0, 2)).reshape(m_s, D_MODEL)
        prev_s = jnp.transpose(state_ffn_conv[l], (1, 0, 2)).reshape(1, pad_s, D_FF)
        hs_tm, conv_st = _ffn(hs_tm, g_ffn, w_gate_b, w_up_b, w_down_b, conv_w, cb, prev_s, l,
                              tm=m_s, tf=256, shift=n_seq, tiles_per_seq=1,
                              final_gain=g_fin if last else None)
        hs = jnp.transpose(hs_tm.reshape(t_len, n_seq, D_MODEL), (1, 0, 2)).reshape(m_s, D_MODEL)
        swa3 = swa.reshape(n_seq, t_len, SWA_COLS)
        ca3 = ca.reshape(n_seq, t_len, CA_COLS)
        outs_s[0].append(jnp.concatenate([csk[l, :, t_len:], swa3[:, :, SWA_QW:SWA_QW + SWA_KW]], axis=1))
        outs_s[1].append(jnp.concatenate([csv[l, :, t_len:], swa3[:, :, SWA_QW + SWA_KW:]], axis=1))
        outs_s[2].append(jnp.concatenate([cck[l, :, t_len:], ca3[:, :, CA_W:2 * CA_W]], axis=1))
        outs_s[3].append(jnp.concatenate([ccv[l, :, t_len:], ca3[:, :, 2 * CA_W:]], axis=1))
        outs_s[4].append(s_new)
        outs_s[5].append(jnp.transpose(conv_st.reshape(CONV_W - 1, n_seq, D_FF), (1, 0, 2)))

    y_prompt = hp.reshape(batch, seq, D_MODEL)
    y_sample = hs.reshape(n_seq, t_len, D_MODEL)
    stack = lambda v: jnp.stack(v, axis=0)
    swa_k_p = stack(outs_p[0]).reshape(depth, batch, SWA_ROWS, KV_SWA, HEAD_DIM)
    swa_v_p = stack(outs_p[1]).reshape(depth, batch, SWA_ROWS, KV_SWA, HEAD_DIM)
    ca_k_p = stack(outs_p[2]).reshape(depth, batch, CA_ROWS, H_CA, HEAD_DIM)
    ca_v_p = stack(outs_p[3]).reshape(depth, batch, CA_ROWS, H_CA, HEAD_DIM)
    ret_p = stack(outs_p[4])
    conv_p = stack(outs_p[5])
    swa_k_s = stack(outs_s[0]).reshape(depth, n_seq, SWA_ROWS, KV_SWA, HEAD_DIM)
    swa_v_s = stack(outs_s[1]).reshape(depth, n_seq, SWA_ROWS, KV_SWA, HEAD_DIM)
    ca_k_s = stack(outs_s[2]).reshape(depth, n_seq, CA_ROWS, H_CA, HEAD_DIM)
    ca_v_s = stack(outs_s[3]).reshape(depth, n_seq, CA_ROWS, H_CA, HEAD_DIM)
    ret_s = stack(outs_s[4])
    conv_s = stack(outs_s[5])
    return (y_prompt, y_sample, swa_k_p, swa_v_p, ca_k_p, ca_v_p, ret_p, conv_p,
            swa_k_s, swa_v_s, ca_k_s, ca_v_s, ret_s, conv_s)
```

```python
import functools

import jax
import jax.numpy as jnp
from jax import lax
from jax.experimental import pallas as pl
from jax.experimental.pallas import tpu as pltpu

F32 = jnp.float32
BF16 = jnp.bfloat16

D_MODEL = 1024
CHUNK = 64
HEAD_DIM = 64
H_RET = 6
H_SWA = 6
KV_SWA = 2
G_SWA = H_SWA // KV_SWA
H_CA = 4
SWA_ROWS = 128
CA_ROWS = 512
REL_CLIP = 128
N_REL = 2 * REL_CLIP + 1
D_FF = 2816
CONV_W = 3
EPS = 1e-6
ROPE_BASE = 10000.0
NEG_INF = -1e30
PAST_LEN = 4096

RET_W = H_RET * HEAD_DIM
RET_COLS = 4 * RET_W
SWA_QW = H_SWA * HEAD_DIM
SWA_KW = KV_SWA * HEAD_DIM
SWA_COLS = SWA_QW + 2 * SWA_KW
CA_W = H_CA * HEAD_DIM
CA_COLS = 3 * CA_W
IN_COLS = RET_COLS + SWA_COLS + CA_COLS

LANES = 128
SUBLANES = 8
VMEM_LIMIT_BYTES = 56 * 1024 * 1024

QBLK = 128
SWA_WIN = QBLK + SWA_ROWS
CA_WIN = QBLK + CA_ROWS
CA_WIN_TILES = CA_WIN // LANES
CA_EXT_TILES = (CA_WIN + CA_ROWS) // LANES
BIAS_ROW = CA_EXT_TILES * LANES + LANES


def _params(semantics):
    return pltpu.CompilerParams(dimension_semantics=semantics,
                                vmem_limit_bytes=VMEM_LIMIT_BYTES)


def _rms(x, gain):
    ms = jnp.mean(x * x, axis=-1, keepdims=True)
    return (x * lax.rsqrt(ms + EPS)) * gain


def _dot(a, b):
    return jnp.dot(a, b, preferred_element_type=F32)


def _dot_nt(a, b):
    return lax.dot_general(a, b, (((1,), (1,)), ((), ())), preferred_element_type=F32)


def _dot_tn(a, b):
    return lax.dot_general(a, b, (((0,), (0,)), ((), ())), preferred_element_type=F32)


def _inproj_kernel(x_ref, g_ref, w_ref, ret_ref, swa_ref, ca_ref):
    h = _rms(x_ref[...], g_ref[...]).astype(BF16)
    ret_ref[...] = _dot(h, w_ref[:, 0:RET_COLS])
    swa_ref[...] = _dot(h, w_ref[:, RET_COLS:RET_COLS + SWA_COLS])
    ca_ref[...] = _dot(h, w_ref[:, RET_COLS + SWA_COLS:IN_COLS])


def _inproj(x, gain, w_in, layer, tm):
    m = x.shape[0]
    row = lambda i: (i, 0)
    return pl.pallas_call(
        _inproj_kernel,
        grid=(m // tm,),
        in_specs=[
            pl.BlockSpec((tm, D_MODEL), row),
            pl.BlockSpec((None, 1, D_MODEL), lambda i: (layer, 0, 0)),
            pl.BlockSpec((None, D_MODEL, IN_COLS), lambda i: (layer, 0, 0)),
        ],
        out_specs=[
            pl.BlockSpec((tm, RET_COLS), row),
            pl.BlockSpec((tm, SWA_COLS), row),
            pl.BlockSpec((tm, CA_COLS), row),
        ],
        out_shape=[
            jax.ShapeDtypeStruct((m, RET_COLS), F32),
            jax.ShapeDtypeStruct((m, SWA_COLS), F32),
            jax.ShapeDtypeStruct((m, CA_COLS), F32),
        ],
        compiler_params=_params(("parallel",)),
        name="inproj",
    )(x, gain, w_in)


def _ret_log_decay():
    return jnp.log(1.0 - 2.0 ** (-5.0 - jnp.arange(H_RET, dtype=F32)))


def _rope_tables(pos):
    half = HEAD_DIM // 2
    inv = 1.0 / (ROPE_BASE ** (jnp.arange(half, dtype=F32) / half))
    ang = pos.astype(F32)[:, None] * inv[None, :]
    cos = jnp.cos(ang)
    sin = jnp.sin(ang)
    cos_h = jnp.concatenate([cos, cos], axis=-1)
    sin_h = jnp.concatenate([-sin, sin], axis=-1)
    return jnp.tile(cos_h, (1, H_RET)), jnp.tile(sin_h, (1, H_RET))


def _retention_tables(t_len):
    log_g = _ret_log_decay()
    t = jnp.arange(t_len, dtype=F32)
    diff = t[:, None] - t[None, :]
    decay = jnp.where(diff >= 0, jnp.exp(jnp.maximum(diff, 0.0)[None] * log_g[:, None, None]), 0.0)
    q_decay = jnp.exp((t + 1.0)[:, None] * log_g[None, :])
    k_decay = jnp.exp((t_len - 1.0 - t)[:, None] * log_g[None, :])
    blk = jnp.exp(t_len * log_g)
    return (decay, jnp.repeat(q_decay, HEAD_DIM, axis=1), jnp.repeat(k_decay, HEAD_DIM, axis=1), blk)


def _rope(x, cos, sin):
    n = x.shape[-1]
    lane = lax.broadcasted_iota(jnp.int32, x.shape, 1)
    first_half = (lane & (HEAD_DIM - 1)) < (HEAD_DIM // 2)
    partner = jnp.where(first_half,
                        pltpu.roll(x, n - HEAD_DIM // 2, axis=1),
                        pltpu.roll(x, HEAD_DIM // 2, axis=1))
    return x * cos + partner * sin


def _retention_block(q, k, v, gate, cos, sin, dec_ref, q_decay, k_decay, blk_ref, state):
    q = _rope(q, cos, sin)
    k = _rope(k, cos, sin) * (HEAD_DIM ** -0.5)
    qb = q.astype(BF16)
    kb = k.astype(BF16)
    kdb = (k * k_decay).astype(BF16)
    vb = v.astype(BF16)
    outs = []
    new_state = []
    for h in range(H_RET):
        sl = slice(h * HEAD_DIM, (h + 1) * HEAD_DIM)
        qh, kh, vh = qb[:, sl], kb[:, sl], vb[:, sl]
        scores = _dot_nt(qh, kh) * dec_ref[h]
        inner = _dot(scores.astype(BF16), vh)
        cross = _dot(qh, state[h].astype(BF16)) * q_decay[:, sl]
        o = inner + cross
        mu = jnp.mean(o, axis=-1, keepdims=True)
        var = jnp.mean(jnp.square(o - mu), axis=-1, keepdims=True)
        outs.append((o - mu) * lax.rsqrt(var + EPS))
        new_state.append(blk_ref[h] * state[h] + _dot_tn(kdb[:, sl], vh))
    on = jnp.concatenate(outs, axis=-1)
    return on * (gate * jax.nn.sigmoid(gate)), new_state


def _ret_prompt_kernel(blk_ref, q_ref, k_ref, v_ref, g_ref, cos_ref, sin_ref, dec_ref, qd_ref,
                       kd_ref, o_ref, s_ref, state_scr):
    t = pl.program_id(1)

    @pl.when(t == 0)
    def _():
        state_scr[...] = jnp.zeros_like(state_scr)

    state = [state_scr[h] for h in range(H_RET)]
    out, new_state = _retention_block(q_ref[...], k_ref[...], v_ref[...], g_ref[...], cos_ref[...],
                                      sin_ref[...], dec_ref, qd_ref[...], kd_ref[...], blk_ref,
                                      state)
    o_ref[...] = out.astype(o_ref.dtype)
    for h in range(H_RET):
        state_scr[h] = new_state[h]

    @pl.when(t == pl.num_programs(1) - 1)
    def _():
        for h in range(H_RET):
            s_ref[h] = new_state[h]


def _ret_prompt(ret, batch, seq, tables, rope):
    tb = QBLK
    nt = seq // tb
    decay, q_decay, k_decay, blk = tables
    cos, sin = rope
    col = lambda c: pl.BlockSpec((tb, RET_W), lambda b, t: (b * nt + t, c))
    full2 = pl.BlockSpec((tb, RET_W), lambda b, t: (0, 0))
    return pl.pallas_call(
        _ret_prompt_kernel,
        grid=(batch, nt),
        in_specs=[
            pl.BlockSpec(memory_space=pltpu.SMEM),
            col(0), col(1), col(2), col(3),
            pl.BlockSpec((tb, RET_W), lambda b, t: (t, 0)),
            pl.BlockSpec((tb, RET_W), lambda b, t: (t, 0)),
            pl.BlockSpec((H_RET, tb, tb), lambda b, t: (0, 0, 0)),
            full2, full2,
        ],
        out_specs=[
            pl.BlockSpec((tb, RET_W), lambda b, t: (b * nt + t, 0)),
            pl.BlockSpec((None, H_RET, HEAD_DIM, HEAD_DIM), lambda b, t: (b, 0, 0, 0)),
        ],
        out_shape=[
            jax.ShapeDtypeStruct((batch * seq, RET_W), BF16),
            jax.ShapeDtypeStruct((batch, H_RET, HEAD_DIM, HEAD_DIM), F32),
        ],
        scratch_shapes=[pltpu.VMEM((H_RET, HEAD_DIM, HEAD_DIM), F32)],
        compiler_params=_params(("parallel", "arbitrary")),
        name="ret_prompt",
    )(blk, ret, ret, ret, ret, cos, sin, decay, q_decay, k_decay)


def _swa_prompt_kernel(sink_ref, q_ref, k_ref, v_ref, o_ref):
    i = pl.program_id(1)
    start = pl.multiple_of(jnp.maximum(i * QBLK - SWA_ROWS, 0), QBLK)
    kw = k_ref[pl.ds(start, SWA_WIN), :].astype(BF16)
    vw = v_ref[pl.ds(start, SWA_WIN), :].astype(BF16)
    q = q_ref[...].astype(BF16)
    rows = G_SWA * QBLK
    row = lax.broadcasted_iota(jnp.int32, (rows, SWA_WIN), 0)
    col = lax.broadcasted_iota(jnp.int32, (rows, SWA_WIN), 1)
    q_chunk = i * (QBLK // CHUNK) + ((row & (QBLK - 1)) // CHUNK)
    k_chunk = start // CHUNK + col // CHUNK
    behind = q_chunk - k_chunk
    n_prev = SWA_ROWS // CHUNK
    valid = jnp.abs(2 * behind - n_prev) <= n_prev
    row1 = lax.broadcasted_iota(jnp.int32, (rows, 1), 0)
    heads = []
    for kv in range(KV_SWA):
        ksl = slice(kv * HEAD_DIM, (kv + 1) * HEAD_DIM)
        q3 = jnp.concatenate(
            [q[:, (kv * G_SWA + g) * HEAD_DIM:(kv * G_SWA + g + 1) * HEAD_DIM] for g in range(G_SWA)],
            axis=0)
        s = _dot_nt(q3, kw[:, ksl]) * (HEAD_DIM ** -0.5)
        s = jnp.where(valid, s, NEG_INF)
        sink = jnp.full((rows, 1), sink_ref[kv * G_SWA + G_SWA - 1], F32)
        for g in range(G_SWA - 2, -1, -1):
            sink = jnp.where(row1 < (g + 1) * QBLK, sink_ref[kv * G_SWA + g], sink)
        m = jnp.maximum(jnp.max(s, axis=-1, keepdims=True), sink)
        p = jnp.exp(s - m)
        denom = jnp.sum(p, axis=-1, keepdims=True) + jnp.exp(sink - m)
        o = _dot(p.astype(BF16), vw[:, ksl]) / denom
        heads += [o[g * QBLK:(g + 1) * QBLK] for g in range(G_SWA)]
    o_ref[...] = jnp.concatenate(heads, axis=-1).astype(o_ref.dtype)


def _swa_prompt(swa, sinks, batch, seq):
    nq = seq // QBLK
    k_col = SWA_QW // SWA_KW
    return pl.pallas_call(
        _swa_prompt_kernel,
        grid=(batch, nq),
        in_specs=[
            pl.BlockSpec(memory_space=pltpu.SMEM),
            pl.BlockSpec((QBLK, SWA_QW), lambda b, i: (b * nq + i, 0)),
            pl.BlockSpec((seq, SWA_KW), lambda b, i: (b, k_col)),
            pl.BlockSpec((seq, SWA_KW), lambda b, i: (b, k_col + 1)),
        ],
        out_specs=pl.BlockSpec((QBLK, SWA_QW), lambda b, i: (b * nq + i, 0)),
        out_shape=jax.ShapeDtypeStruct((batch * seq, SWA_QW), BF16),
        compiler_params=_params(("parallel", "arbitrary")),
        name="swa_prompt",
    )(sinks, swa, swa, swa)


def _bias_kernel(t_ref, o_ref):
    n_pad = t_ref.shape[-1]
    r = lax.broadcasted_iota(jnp.int32, (n_pad, BIAS_ROW), 0)
    w = lax.broadcasted_iota(jnp.int32, (n_pad, BIAS_ROW), 1)
    idx = jnp.clip(BIAS_ROW // 2 - 1 - w, -REL_CLIP, REL_CLIP) + REL_CLIP
    onehot = jnp.where(r == idx, 1.0, 0.0).astype(BF16)
    t = t_ref[...]
    t_hi = t.astype(BF16)
    rem = t - t_hi.astype(F32)
    t_mid = rem.astype(BF16)
    t_lo = (rem - t_mid.astype(F32)).astype(BF16)
    f = (_dot(t_hi, onehot) + _dot(t_mid, onehot)) + _dot(t_lo, onehot)
    ext = CA_EXT_TILES * LANES
    row = lax.broadcasted_iota(jnp.int32, (QBLK, ext), 0)
    colj = lax.broadcasted_iota(jnp.int32, (QBLK, ext), 1)
    ahead = colj // CHUNK - row // CHUNK
    n_prev = CA_ROWS // CHUNK
    valid = jnp.abs(2 * ahead - n_prev) <= n_prev
    for h in range(H_CA):
        x = jnp.broadcast_to(f[h:h + 1, :], (QBLK, BIAS_ROW))
        y = pltpu.roll(x, BIAS_ROW - (QBLK - 1), axis=1, stride=1, stride_axis=0)
        y = jnp.where(valid, y[:, 0:ext], NEG_INF)
        for c in range(CA_EXT_TILES):
            o_ref[h, c] = y[:, c * LANES:(c + 1) * LANES]


def _build_bias(rel_bias_table):
    depth = rel_bias_table.shape[0]
    n_pad = 3 * LANES
    t = jnp.transpose(rel_bias_table, (0, 2, 1))
    t = jnp.pad(t, ((0, 0), (0, SUBLANES - H_CA), (0, n_pad - N_REL)))
    return pl.pallas_call(
        _bias_kernel,
        grid=(depth,),
        in_specs=[pl.BlockSpec((None, SUBLANES, n_pad), lambda l: (l, 0, 0))],
        out_specs=pl.BlockSpec((None, H_CA, CA_EXT_TILES, QBLK, LANES), lambda l: (l, 0, 0, 0, 0)),
        out_shape=jax.ShapeDtypeStruct((depth, H_CA, CA_EXT_TILES, QBLK, LANES), F32),
        compiler_params=_params(("parallel",)),
        name="ca_bias",
    )(t)


def _ca_prompt_kernel(q_ref, k_ref, v_ref, bias_ref, o_ref):
    i = pl.program_id(1)
    ideal = i * QBLK - CA_ROWS
    start = pl.multiple_of(jnp.maximum(ideal, 0), QBLK)
    off = (start - ideal) // LANES
    kw = k_ref[pl.ds(start, CA_WIN), :].astype(BF16)
    vw = v_ref[pl.ds(start, CA_WIN), :].astype(BF16)
    q = q_ref[...].astype(BF16)
    heads = []
    for h in range(H_CA):
        sl = slice(h * HEAD_DIM, (h + 1) * HEAD_DIM)
        bias = jnp.concatenate([bias_ref[h, off + c] for c in range(CA_WIN_TILES)], axis=-1)
        s = _dot_nt(q[:, sl], kw[:, sl]) * (HEAD_DIM ** -0.5) + bias
        m = jnp.max(s, axis=-1, keepdims=True)
        p = jnp.exp(s - m)
        denom = jnp.sum(p, axis=-1, keepdims=True)
        heads.append(_dot(p.astype(BF16), vw[:, sl]) / denom)
    o_ref[...] = jnp.concatenate(heads, axis=-1).astype(o_ref.dtype)


def _ca_prompt(ca, bias, layer, batch, seq):
    nq = seq // QBLK
    return pl.pallas_call(
        _ca_prompt_kernel,
        grid=(batch, nq),
        in_specs=[
            pl.BlockSpec((QBLK, CA_W), lambda b, i: (b * nq + i, 0)),
            pl.BlockSpec((seq, CA_W), lambda b, i: (b, 1)),
            pl.BlockSpec((seq, CA_W), lambda b, i: (b, 2)),
            pl.BlockSpec((None, H_CA, CA_EXT_TILES, QBLK, LANES), lambda b, i: (layer, 0, 0, 0, 0)),
        ],
        out_specs=pl.BlockSpec((QBLK, CA_W), lambda b, i: (b * nq + i, 0)),
        out_shape=jax.ShapeDtypeStruct((batch * seq, CA_W), BF16),
        compiler_params=_params(("parallel", "arbitrary")),
        name="ca_prompt",
    )(ca, ca, ca, bias)


def _mixer_sample_kernel(sink_ref, blk_ref, ret_ref, swa_ref, ca_ref, cos_ref, sin_ref, dec_ref,
                         qd_ref, kd_ref, sk_ref, sv_ref, ck_ref, cv_ref, st_ref, bias_ref,
                         oa_ref, ob_ref, oc_ref, snew_ref, *, t_len):
    col = lambda c: ret_ref[:, c * RET_W:(c + 1) * RET_W]
    state = [st_ref[h] for h in range(H_RET)]
    out, new_state = _retention_block(col(0), col(1), col(2), col(3), cos_ref[...], sin_ref[...],
                                      dec_ref, qd_ref[...], kd_ref[...], blk_ref, state)
    oa_ref[...] = out.astype(oa_ref.dtype)
    for h in range(H_RET):
        snew_ref[h] = new_state[h]

    q = swa_ref[:, 0:SWA_QW].astype(BF16)
    kn = swa_ref[:, SWA_QW:SWA_QW + SWA_KW].astype(BF16)
    vn = swa_ref[:, SWA_QW + SWA_KW:SWA_COLS].astype(BF16)
    kc = sk_ref[...].astype(BF16)
    vc = sv_ref[...].astype(BF16)
    nrow = G_SWA * t_len
    row1 = lax.broadcasted_iota(jnp.int32, (nrow, 1), 0)
    heads = []
    for kv in range(KV_SWA):
        ksl = slice(kv * HEAD_DIM, (kv + 1) * HEAD_DIM)
        q3 = jnp.concatenate(
            [q[:, (kv * G_SWA + g) * HEAD_DIM:(kv * G_SWA + g + 1) * HEAD_DIM] for g in range(G_SWA)],
            axis=0)
        s_c = _dot_nt(q3, kc[:, ksl]) * (HEAD_DIM ** -0.5)
        s_n = _dot_nt(q3, kn[:, ksl]) * (HEAD_DIM ** -0.5)
        sink = jnp.full((nrow, 1), sink_ref[kv * G_SWA + G_SWA - 1], F32)
        for g in range(G_SWA - 2, -1, -1):
            sink = jnp.where(row1 < (g + 1) * t_len, sink_ref[kv * G_SWA + g], sink)
        m = jnp.maximum(jnp.maximum(jnp.max(s_c, axis=-1, keepdims=True),
                                    jnp.max(s_n, axis=-1, keepdims=True)), sink)
        p_c = jnp.exp(s_c - m)
        p_n = jnp.exp(s_n - m)
        denom = (jnp.sum(p_c, axis=-1, keepdims=True) + jnp.sum(p_n, axis=-1, keepdims=True)
                 + jnp.exp(sink - m))
        o = (_dot(p_c.astype(BF16), vc[:, ksl]) + _dot(p_n.astype(BF16), vn[:, ksl])) / denom
        heads += [o[g * t_len:(g + 1) * t_len] for g in range(G_SWA)]
    ob_ref[...] = jnp.concatenate(heads, axis=-1).astype(ob_ref.dtype)

    q = ca_ref[:, 0:CA_W].astype(BF16)
    kn = ca_ref[:, CA_W:2 * CA_W].astype(BF16)
    vn = ca_ref[:, 2 * CA_W:CA_COLS].astype(BF16)
    kc = ck_ref[...].astype(BF16)
    vc = cv_ref[...].astype(BF16)
    n_tiles = CA_ROWS // LANES
    heads = []
    for h in range(H_CA):
        sl = slice(h * HEAD_DIM, (h + 1) * HEAD_DIM)
        bias_c = jnp.concatenate([bias_ref[h, c, 0:t_len, :] for c in range(n_tiles)], axis=-1)
        bias_n = bias_ref[h, n_tiles, 0:t_len, 0:t_len]
        s_c = _dot_nt(q[:, sl], kc[:, sl]) * (HEAD_DIM ** -0.5) + bias_c
        s_n = _dot_nt(q[:, sl], kn[:, sl]) * (HEAD_DIM ** -0.5) + bias_n
        m = jnp.maximum(jnp.max(s_c, axis=-1, keepdims=True), jnp.max(s_n, axis=-1, keepdims=True))
        p_c = jnp.exp(s_c - m)
        p_n = jnp.exp(s_n - m)
        denom = jnp.sum(p_c, axis=-1, keepdims=True) + jnp.sum(p_n, axis=-1, keepdims=True)
        heads.append((_dot(p_c.astype(BF16), vc[:, sl]) + _dot(p_n.astype(BF16), vn[:, sl])) / denom)
    oc_ref[...] = jnp.concatenate(heads, axis=-1).astype(oc_ref.dtype)


def _mixer_sample(ret, swa, ca, sinks, tables, rope, cache_sk, cache_sv, cache_ck, cache_cv,
                  state_ret, bias, layer, n_seq, t_len):
    m = n_seq * t_len
    decay, q_decay, k_decay, blk = tables
    cos, sin = rope
    whole = lambda shape: pl.BlockSpec(shape, lambda b: (0,) * len(shape))
    seq_rows = lambda width: pl.BlockSpec((t_len, width), lambda b: (b, 0))
    cache = lambda rows, width: pl.BlockSpec((None, None, rows, width), lambda b: (layer, b, 0, 0))
    return pl.pallas_call(
        functools.partial(_mixer_sample_kernel, t_len=t_len),
        grid=(n_seq,),
        in_specs=[
            pl.BlockSpec(memory_space=pltpu.SMEM),
            pl.BlockSpec(memory_space=pltpu.SMEM),
            seq_rows(RET_COLS), seq_rows(SWA_COLS), seq_rows(CA_COLS),
            whole((t_len, RET_W)), whole((t_len, RET_W)),
            whole((H_RET, t_len, t_len)),
            whole((t_len, RET_W)), whole((t_len, RET_W)),
            cache(SWA_ROWS, SWA_KW), cache(SWA_ROWS, SWA_KW),
            cache(CA_ROWS, CA_W), cache(CA_ROWS, CA_W),
            pl.BlockSpec((None, None, H_RET, HEAD_DIM, HEAD_DIM), lambda b: (layer, b, 0, 0, 0)),
            pl.BlockSpec((None, H_CA, CA_EXT_TILES, QBLK, LANES), lambda b: (layer, 0, 0, 0, 0)),
        ],
        out_specs=[
            seq_rows(RET_W), seq_rows(SWA_QW), seq_rows(CA_W),
            pl.BlockSpec((None, H_RET, HEAD_DIM, HEAD_DIM), lambda b: (b, 0, 0, 0)),
        ],
        out_shape=[
            jax.ShapeDtypeStruct((m, RET_W), BF16),
            jax.ShapeDtypeStruct((m, SWA_QW), BF16),
            jax.ShapeDtypeStruct((m, CA_W), BF16),
            jax.ShapeDtypeStruct((n_seq, H_RET, HEAD_DIM, HEAD_DIM), F32),
        ],
        compiler_params=_params(("parallel",)),
        name="mixer_sample",
    )(sinks, blk, ret, swa, ca, cos, sin, decay, q_decay, k_decay, cache_sk, cache_sv, cache_ck,
      cache_cv, state_ret, bias)


def _outproj_kernel(x_ref, a_ref, b_ref, c_ref, w_ref, o_ref):
    acc = x_ref[...] + _dot(a_ref[...], w_ref[0:RET_W, :])
    acc = acc + _dot(b_ref[...], w_ref[RET_W:RET_W + SWA_QW, :])
    o_ref[...] = acc + _dot(c_ref[...], w_ref[RET_W + SWA_QW:D_MODEL, :])


def _outproj(x, oa, ob, oc, w_out, layer, tm):
    m = x.shape[0]
    row = lambda i: (i, 0)
    return pl.pallas_call(
        _outproj_kernel,
        grid=(m // tm,),
        in_specs=[
            pl.BlockSpec((tm, D_MODEL), row),
            pl.BlockSpec((tm, RET_W), row),
            pl.BlockSpec((tm, SWA_QW), row),
            pl.BlockSpec((tm, CA_W), row),
            pl.BlockSpec((None, D_MODEL, D_MODEL), lambda i: (layer, 0, 0)),
        ],
        out_specs=pl.BlockSpec((tm, D_MODEL), row),
        out_shape=jax.ShapeDtypeStruct((m, D_MODEL), F32),
        compiler_params=_params(("parallel",)),
        name="outproj",
    )(x, oa, ob, oc, w_out)


def _ffn_kernel(*refs, tm, tf, pad, shift, tiles_per_seq, final_norm):
    if final_norm:
        (x_ref, gain_ref, wg_ref, wu_ref, wd_ref, cw_ref, cb_ref, prev_ref, gf_ref,
         o_ref, st_ref, hn_scr, act_scr, hist_scr) = refs
    else:
        (x_ref, gain_ref, wg_ref, wu_ref, wd_ref, cw_ref, cb_ref, prev_ref,
         o_ref, st_ref, hn_scr, act_scr, hist_scr) = refs
    hn_scr[...] = _rms(x_ref[...], gain_ref[...]).astype(BF16)
    if tiles_per_seq == 1:
        hist = prev_ref
    else:
        hist = hist_scr

        @pl.when((pl.program_id(0) % tiles_per_seq) == 0)
        def _():
            hist_scr[...] = prev_ref[...]

    for c in range(D_FF // tf):
        cols = slice(c * tf, (c + 1) * tf)
        hn = hn_scr[...]
        g = _dot(hn, wg_ref[:, cols])
        u = _dot(hn, wu_ref[:, cols])
        g_ext = jnp.concatenate([hist[:, cols], g], axis=0)
        gc = cb_ref[:, cols] + cw_ref[0:1, cols] * pltpu.roll(g_ext, 2 * shift, axis=0)[pad:]
        gc = gc + cw_ref[1:2, cols] * pltpu.roll(g_ext, shift, axis=0)[pad:]
        gc = gc + cw_ref[2:3, cols] * g
        act_scr[:, cols] = (jax.nn.gelu(gc) * u).astype(BF16)
        tail = g_ext[tm:tm + pad]
        if tiles_per_seq > 1:
            hist_scr[:, cols] = tail
        st_ref[:, cols] = tail

    y = x_ref[...] + _dot(act_scr[...], wd_ref[...])
    if final_norm:
        y = _rms(y, gf_ref[...])
    o_ref[...] = y


def _ffn(x, gain, w_gate, w_up, w_down, conv_w, conv_b, prev, layer, tm, tf, shift, tiles_per_seq,
         final_gain=None):
    m = x.shape[0]
    n_groups, pad, _ = prev.shape
    final_norm = final_gain is not None
    resident = lambda shape: pl.BlockSpec((None,) + shape, lambda i: (layer, 0, 0),
                                          pipeline_mode=pl.Buffered(1))
    state = pl.BlockSpec((None, pad, D_FF), lambda i: (i // tiles_per_seq, 0, 0))
    in_specs = [
        pl.BlockSpec((tm, D_MODEL), lambda i: (i, 0)),
        resident((1, D_MODEL)),
        resident((D_MODEL, D_FF)), resident((D_MODEL, D_FF)), resident((D_FF, D_MODEL)),
        resident((CONV_W, D_FF)), resident((1, D_FF)),
        state,
    ]
    args = [x, gain, w_gate, w_up, w_down, conv_w, conv_b, prev]
    if final_norm:
        in_specs.append(pl.BlockSpec((1, D_MODEL), lambda i: (0, 0)))
        args.append(final_gain)
    return pl.pallas_call(
        functools.partial(_ffn_kernel, tm=tm, tf=tf, pad=pad, shift=shift,
                          tiles_per_seq=tiles_per_seq, final_norm=final_norm),
        grid=(m // tm,),
        in_specs=in_specs,
        out_specs=[pl.BlockSpec((tm, D_MODEL), lambda i: (i, 0)), state],
        out_shape=[
            jax.ShapeDtypeStruct((m, D_MODEL), F32),
            jax.ShapeDtypeStruct((n_groups, pad, D_FF), F32),
        ],
        scratch_shapes=[
            pltpu.VMEM((tm, D_MODEL), BF16),
            pltpu.VMEM((tm, D_FF), BF16),
            pltpu.VMEM((pad, D_FF), F32),
        ],
        compiler_params=_params(("arbitrary",)),
        name="ffn",
    )(*args)


def kernel(x_prompt, x_sample, cache_swa_k, cache_swa_v, cache_ca_k, cache_ca_v, state_ret,
           state_ffn_conv, w_in, w_out, attn_sinks, rel_bias_table, norm_mix, norm_ffn, w_gate,
           w_up, conv_w, conv_b, w_down, norm_final):
    batch, seq, _ = x_prompt.shape
    n_seq, t_len, _ = x_sample.shape
    depth = w_in.shape[0]
    assert seq % QBLK == 0 and seq >= CA_WIN and t_len % SUBLANES == 0
    assert cache_swa_k.shape[2] == SWA_ROWS and cache_ca_k.shape[2] == CA_ROWS

    w_in_b = w_in.astype(BF16)
    w_out_b = w_out.astype(BF16)
    w_gate_b = w_gate.astype(BF16)
    w_up_b = w_up.astype(BF16)
    w_down_b = w_down.astype(BF16)
    g_mix = norm_mix.reshape(depth, 1, D_MODEL)
    g_ffn = norm_ffn.reshape(depth, 1, D_MODEL)
    g_fin = norm_final.reshape(1, D_MODEL)
    cb = conv_b.reshape(depth, 1, D_FF)

    rope_p = _rope_tables(jnp.arange(seq))
    rope_s = _rope_tables(PAST_LEN + jnp.arange(t_len))
    tab_p = _retention_tables(QBLK)
    tab_s = _retention_tables(t_len)
    bias = _build_bias(rel_bias_table)

    csk = cache_swa_k.reshape(depth, n_seq, SWA_ROWS, SWA_KW)
    csv = cache_swa_v.reshape(depth, n_seq, SWA_ROWS, SWA_KW)
    cck = cache_ca_k.reshape(depth, n_seq, CA_ROWS, CA_W)
    ccv = cache_ca_v.reshape(depth, n_seq, CA_ROWS, CA_W)

    hp = x_prompt.reshape(batch * seq, D_MODEL)
    m_s = n_seq * t_len
    hs = x_sample.reshape(m_s, D_MODEL)

    pad_p = SUBLANES
    pad_s = (CONV_W - 1) * n_seq
    prev_p = jnp.zeros((batch, pad_p, D_FF), F32)
    tm_p = 512
    tiles_per_seq = seq // tm_p

    outs_p = [[] for _ in range(6)]
    outs_s = [[] for _ in range(6)]
    for l in range(depth):
        last = l == depth - 1
        ret, swa, ca = _inproj(hp, g_mix, w_in_b, l, tm=512)
        oa, s_ret = _ret_prompt(ret, batch, seq, tab_p, rope_p)
        ob = _swa_prompt(swa, attn_sinks[l], batch, seq)
        oc = _ca_prompt(ca, bias, l, batch, seq)
        hp = _outproj(hp, oa, ob, oc, w_out_b, l, tm=512)
        hp, conv_st = _ffn(hp, g_ffn, w_gate_b, w_up_b, w_down_b, conv_w, cb, prev_p, l, tm=tm_p,
                           tf=256, shift=1, tiles_per_seq=tiles_per_seq,
                           final_gain=g_fin if last else None)
        swa3 = swa.reshape(batch, seq, SWA_COLS)
        ca3 = ca.reshape(batch, seq, CA_COLS)
        outs_p[0].append(swa3[:, seq - SWA_ROWS:, SWA_QW:SWA_QW + SWA_KW])
        outs_p[1].append(swa3[:, seq - SWA_ROWS:, SWA_QW + SWA_KW:])
        outs_p[2].append(ca3[:, seq - CA_ROWS:, CA_W:2 * CA_W])
        outs_p[3].append(ca3[:, seq - CA_ROWS:, 2 * CA_W:])
        outs_p[4].append(s_ret)
        outs_p[5].append(conv_st[:, pad_p - (CONV_W - 1):, :])

        ret, swa, ca = _inproj(hs, g_mix, w_in_b, l, tm=m_s // 2)
        oa, ob, oc, s_new = _mixer_sample(ret, swa, ca, attn_sinks[l], tab_s, rope_s, csk, csv, cck,
                                          ccv, state_ret, bias, l, n_seq, t_len)
        hs = _outproj(hs, oa, ob, oc, w_out_b, l, tm=m_s // 2)
        hs_tm = jnp.transpose(hs.reshape(n_seq, t_len, D_MODEL), (1, 0, 2)).reshape(m_s, D_MODEL)
        prev_s = jnp.transpose(state_ffn_conv[l], (1, 0, 2)).reshape(1, pad_s, D_FF)
        hs_tm, conv_st = _ffn(hs_tm, g_ffn, w_gate_b, w_up_b, w_down_b, conv_w, cb, prev_s, l,
                              tm=m_s, tf=256, shift=n_seq, tiles_per_seq=1,
                              final_gain=g_fin if last else None)
        hs = jnp.transpose(hs_tm.reshape(t_len, n_seq, D_MODEL), (1, 0, 2)).reshape(m_s, D_MODEL)
        swa3 = swa.reshape(n_seq, t_len, SWA_COLS)
        ca3 = ca.reshape(n_seq, t_len, CA_COLS)
        outs_s[0].append(jnp.concatenate([csk[l, :, t_len:], swa3[:, :, SWA_QW:SWA_QW + SWA_KW]], axis=1))
        outs_s[1].append(jnp.concatenate([csv[l, :, t_len:], swa3[:, :, SWA_QW + SWA_KW:]], axis=1))
        outs_s[2].append(jnp.concatenate([cck[l, :, t_len:], ca3[:, :, CA_W:2 * CA_W]], axis=1))
        outs_s[3].append(jnp.concatenate([ccv[l, :, t_len:], ca3[:, :, 2 * CA_W:]], axis=1))
        outs_s[4].append(s_new)
        outs_s[5].append(jnp.transpose(conv_st.reshape(CONV_W - 1, n_seq, D_FF), (1, 0, 2)))

    y_prompt = hp.reshape(batch, seq, D_MODEL)
    y_sample = hs.reshape(n_seq, t_len, D_MODEL)
    stack = lambda v: jnp.stack(v, axis=0)
    swa_k_p = stack(outs_p[0]).reshape(depth, batch, SWA_ROWS, KV_SWA, HEAD_DIM)
    swa_v_p = stack(outs_p[1]).reshape(depth, batch, SWA_ROWS, KV_SWA, HEAD_DIM)
    ca_k_p = stack(outs_p[2]).reshape(depth, batch, CA_ROWS, H_CA, HEAD_DIM)
    ca_v_p = stack(outs_p[3]).reshape(depth, batch, CA_ROWS, H_CA, HEAD_DIM)
    ret_p = stack(outs_p[4])
    conv_p = stack(outs_p[5])
    swa_k_s = stack(outs_s[0]).reshape(depth, n_seq, SWA_ROWS, KV_SWA, HEAD_DIM)
    swa_v_s = stack(outs_s[1]).reshape(depth, n_seq, SWA_ROWS, KV_SWA, HEAD_DIM)
    ca_k_s = stack(outs_s[2]).reshape(depth, n_seq, CA_ROWS, H_CA, HEAD_DIM)
    ca_v_s = stack(outs_s[3]).reshape(depth, n_seq, CA_ROWS, H_CA, HEAD_DIM)
    ret_s = stack(outs_s[4])
    conv_s = stack(outs_s[5])
    return (y_prompt, y_sample, swa_k_p, swa_v_p, ca_k_p, ca_v_p, ret_p, conv_p,
            swa_k_s, swa_v_s, ca_k_s, ca_v_s, ret_s, conv_s)
```

```python
import functools

import jax
import jax.numpy as jnp
from jax import lax
from jax.experimental import pallas as pl
from jax.experimental.pallas import tpu as pltpu

F32 = jnp.float32
BF16 = jnp.bfloat16

D_MODEL = 1024
CHUNK = 64
HEAD_DIM = 64
H_RET = 6
H_SWA = 6
KV_SWA = 2
G_SWA = H_SWA // KV_SWA
H_CA = 4
SWA_ROWS = 128
CA_ROWS = 512
REL_CLIP = 128
N_REL = 2 * REL_CLIP + 1
D_FF = 2816
CONV_W = 3
EPS = 1e-6
ROPE_BASE = 10000.0
NEG_INF = -1e30
PAST_LEN = 4096

RET_W = H_RET * HEAD_DIM
RET_COLS = 4 * RET_W
SWA_QW = H_SWA * HEAD_DIM
SWA_KW = KV_SWA * HEAD_DIM
SWA_COLS = SWA_QW + 2 * SWA_KW
CA_W = H_CA * HEAD_DIM
CA_COLS = 3 * CA_W
IN_COLS = RET_COLS + SWA_COLS + CA_COLS

LANES = 128
SUBLANES = 8
VMEM_LIMIT_BYTES = 56 * 1024 * 1024

QBLK = 128
QSTEP = 4
SWA_WIN = QBLK + SWA_ROWS
CA_WIN = QBLK + CA_ROWS
CA_WIN_TILES = CA_WIN // LANES
CA_EXT_TILES = (CA_WIN + CA_ROWS) // LANES
BIAS_ROW = CA_EXT_TILES * LANES + LANES


def _params(semantics):
    return pltpu.CompilerParams(dimension_semantics=semantics,
                                vmem_limit_bytes=VMEM_LIMIT_BYTES)


def _rms(x, gain):
    ms = jnp.mean(x * x, axis=-1, keepdims=True)
    return (x * lax.rsqrt(ms + EPS)) * gain


def _dot(a, b):
    return jnp.dot(a, b, preferred_element_type=F32)


def _dot_nt(a, b):
    return lax.dot_general(a, b, (((1,), (1,)), ((), ())), preferred_element_type=F32)


def _dot_tn(a, b):
    return lax.dot_general(a, b, (((0,), (0,)), ((), ())), preferred_element_type=F32)


def _inproj_kernel(x_ref, g_ref, w_ref, ret_ref, swa_ref, ca_ref):
    h = _rms(x_ref[...], g_ref[...]).astype(BF16)
    ret_ref[...] = _dot(h, w_ref[:, 0:RET_COLS])
    swa_ref[...] = _dot(h, w_ref[:, RET_COLS:RET_COLS + SWA_COLS])
    ca_ref[...] = _dot(h, w_ref[:, RET_COLS + SWA_COLS:IN_COLS])


def _inproj(x, gain, w_in, layer, tm):
    m = x.shape[0]
    row = lambda i: (i, 0)
    return pl.pallas_call(
        _inproj_kernel,
        grid=(m // tm,),
        in_specs=[
            pl.BlockSpec((tm, D_MODEL), row),
            pl.BlockSpec((None, 1, D_MODEL), lambda i: (layer, 0, 0)),
            pl.BlockSpec((None, D_MODEL, IN_COLS), lambda i: (layer, 0, 0)),
        ],
        out_specs=[
            pl.BlockSpec((tm, RET_COLS), row),
            pl.BlockSpec((tm, SWA_COLS), row),
            pl.BlockSpec((tm, CA_COLS), row),
        ],
        out_shape=[
            jax.ShapeDtypeStruct((m, RET_COLS), F32),
            jax.ShapeDtypeStruct((m, SWA_COLS), F32),
            jax.ShapeDtypeStruct((m, CA_COLS), F32),
        ],
        compiler_params=_params(("parallel",)),
        name="inproj",
    )(x, gain, w_in)


def _ret_log_decay():
    return jnp.log(1.0 - 2.0 ** (-5.0 - jnp.arange(H_RET, dtype=F32)))


def _rope_tables(pos):
    half = HEAD_DIM // 2
    inv = 1.0 / (ROPE_BASE ** (jnp.arange(half, dtype=F32) / half))
    ang = pos.astype(F32)[:, None] * inv[None, :]
    cos = jnp.cos(ang)
    sin = jnp.sin(ang)
    cos_h = jnp.concatenate([cos, cos], axis=-1)
    sin_h = jnp.concatenate([-sin, sin], axis=-1)
    return jnp.tile(cos_h, (1, H_RET)), jnp.tile(sin_h, (1, H_RET))


def _retention_tables(t_len):
    log_g = _ret_log_decay()
    t = jnp.arange(t_len, dtype=F32)
    diff = t[:, None] - t[None, :]
    decay = jnp.where(diff >= 0, jnp.exp(jnp.maximum(diff, 0.0)[None] * log_g[:, None, None]), 0.0)
    q_decay = jnp.exp((t + 1.0)[:, None] * log_g[None, :])
    k_decay = jnp.exp((t_len - 1.0 - t)[:, None] * log_g[None, :])
    head = jnp.arange(RET_W) // HEAD_DIM
    same_head = (head[:, None] == head[None, :]).astype(F32)
    state_decay = jnp.broadcast_to(jnp.exp(t_len * log_g)[head][:, None], (RET_W, RET_W))
    return (decay.reshape(H_RET * t_len, t_len), jnp.repeat(q_decay, HEAD_DIM, axis=1),
            jnp.repeat(k_decay, HEAD_DIM, axis=1), (same_head / HEAD_DIM).astype(BF16), same_head,
            state_decay)


def _rope(x, cos, sin):
    n = x.shape[-1]
    lane = lax.broadcasted_iota(jnp.int32, x.shape, 1)
    first_half = (lane & (HEAD_DIM - 1)) < (HEAD_DIM // 2)
    partner = jnp.where(first_half,
                        pltpu.roll(x, n - HEAD_DIM // 2, axis=1),
                        pltpu.roll(x, HEAD_DIM // 2, axis=1))
    return x * cos + partner * sin


def _group_mean(x, head_avg):
    t_len = x.shape[0]
    hi = x.astype(BF16)
    lo = (x - hi.astype(F32)).astype(BF16)
    r = _dot(jnp.concatenate([hi, lo], axis=0), head_avg)
    return r[:t_len] + r[t_len:]


def _retention_block(q, k, v, gate, cos, sin, dec, q_decay, k_decay, head_avg, same_head,
                     state_decay, state):
    t_len = q.shape[0]
    q = _rope(q, cos, sin)
    k = _rope(k, cos, sin) * (HEAD_DIM ** -0.5)
    qb = q.astype(BF16)
    kb = k.astype(BF16)
    kdb = (k * k_decay).astype(BF16)
    vb = v.astype(BF16)
    lane_head = lax.broadcasted_iota(jnp.int32, (t_len, RET_W), 1) // HEAD_DIM
    q_stack = jnp.concatenate([jnp.where(lane_head == h, q, 0.0).astype(BF16) for h in range(H_RET)],
                              axis=0)
    scores = _dot_nt(q_stack, kb) * dec
    inner_all = _dot(scores.astype(BF16), vb)
    inner = inner_all[0:t_len]
    for h in range(1, H_RET):
        inner = jnp.where(lane_head == h, inner_all[h * t_len:(h + 1) * t_len], inner)
    o = inner + _dot(qb, state.astype(BF16)) * q_decay
    d = o - _group_mean(o, head_avg)
    var = _group_mean(d * d, head_avg)
    on = d * lax.rsqrt(var + EPS)
    new_state = state_decay * state + same_head * _dot_tn(kdb, vb)
    return on * (gate * jax.nn.sigmoid(gate)), new_state


def _ret_prompt_kernel(q_ref, k_ref, v_ref, g_ref, cos_ref, sin_ref, dec_ref, qd_ref, kd_ref,
                       avg_ref, same_ref, sdec_ref, o_ref, s_ref, state_scr):
    t = pl.program_id(1)

    @pl.when(t == 0)
    def _():
        state_scr[...] = jnp.zeros_like(state_scr)

    n_blk = q_ref.shape[0]
    for n in range(n_blk):
        out, new_state = _retention_block(q_ref[n], k_ref[n], v_ref[n], g_ref[n], cos_ref[...],
                                          sin_ref[...], dec_ref[...], qd_ref[...], kd_ref[...],
                                          avg_ref[...], same_ref[...], sdec_ref[...], state_scr[n])
        o_ref[n] = out.astype(o_ref.dtype)
        state_scr[n] = new_state

    @pl.when(t == pl.num_programs(1) - 1)
    def _():
        for n in range(n_blk):
            for h in range(H_RET):
                sl = slice(h * HEAD_DIM, (h + 1) * HEAD_DIM)
                s_ref[n, h] = state_scr[n, sl, sl]


def _ret_prompt(ret, batch, seq, tables, rope, n_blk):
    tb = QBLK
    cos, sin = rope
    ret3 = ret.reshape(batch, seq, RET_COLS)
    col = lambda c: pl.BlockSpec((n_blk, tb, RET_W), lambda b, t: (b, t, c))
    pos = pl.BlockSpec((tb, RET_W), lambda b, t: (t, 0))
    whole = lambda a: pl.BlockSpec(a.shape, lambda b, t: (0, 0))
    out, s_fin = pl.pallas_call(
        _ret_prompt_kernel,
        grid=(batch // n_blk, seq // tb),
        in_specs=[col(0), col(1), col(2), col(3), pos, pos] + [whole(a) for a in tables],
        out_specs=[
            pl.BlockSpec((n_blk, tb, RET_W), lambda b, t: (b, t, 0)),
            pl.BlockSpec((n_blk, H_RET, HEAD_DIM, HEAD_DIM), lambda b, t: (b, 0, 0, 0)),
        ],
        out_shape=[
            jax.ShapeDtypeStruct((batch, seq, RET_W), BF16),
            jax.ShapeDtypeStruct((batch, H_RET, HEAD_DIM, HEAD_DIM), F32),
        ],
        scratch_shapes=[pltpu.VMEM((n_blk, RET_W, RET_W), F32)],
        compiler_params=_params(("parallel", "arbitrary")),
        name="ret_prompt",
    )(ret3, ret3, ret3, ret3, cos, sin, *tables)
    return out.reshape(batch * seq, RET_W), s_fin


def _pair_softmax_pv(q_pair, k_halves, v_halves, bias_halves, lower_half, sinks=None):
    outs = []
    for half in range(2):
        q_h = jnp.where(lower_half, q_pair, 0) if half == 0 else jnp.where(lower_half, 0, q_pair)
        s = _dot_nt(q_h, k_halves[half]) + bias_halves[half]
        m = jnp.max(s, axis=-1, keepdims=True)
        if sinks is not None:
            m = jnp.maximum(m, sinks[half])
        r = _dot(jnp.exp(s - m).astype(BF16), v_halves[half])
        denom = r[:, LANES:]
        if sinks is not None:
            denom = denom + jnp.exp(sinks[half] - m)
        outs.append(r[:, :LANES] / denom)
    return jnp.where(lower_half, outs[0], outs[1])


def _swa_prompt_kernel(sink_ref, q_ref, k_ref, v_ref, o_ref, k_scr, v_scr):
    i = pl.program_id(1)

    @pl.when(i == 0)
    def _():
        k = k_ref[...]
        v = v_ref[...]
        ones = jnp.ones((k.shape[0], LANES), BF16)
        for x, (kx, vx) in enumerate(((k, v), (pltpu.roll(k, HEAD_DIM, axis=1),
                                               pltpu.roll(v, HEAD_DIM, axis=1)))):
            k_scr[x] = kx.astype(BF16)
            v_scr[x, :, 0:LANES] = vx.astype(BF16)
            v_scr[x, :, LANES:] = ones

    row = lax.broadcasted_iota(jnp.int32, (QBLK, SWA_WIN), 0)
    col = lax.broadcasted_iota(jnp.int32, (QBLK, SWA_WIN), 1)
    n_prev = SWA_ROWS // CHUNK
    lower_half = lax.broadcasted_iota(jnp.int32, (QBLK, LANES), 1) < HEAD_DIM
    for u in range(QSTEP):
        blk = i * QSTEP + u
        rows = slice(u * QBLK, (u + 1) * QBLK)
        start = pl.multiple_of(jnp.maximum(blk * QBLK - SWA_ROWS, 0), QBLK)
        win = pl.ds(start, SWA_WIN)
        q = (q_ref[rows, :] * (HEAD_DIM ** -0.5)).astype(BF16)
        behind = (blk * (QBLK // CHUNK) + row // CHUNK) - (start // CHUNK + col // CHUNK)
        mask = jnp.where(jnp.abs(2 * behind - n_prev) <= n_prev, 0.0, NEG_INF)
        for c in range(SWA_QW // LANES):
            cols = slice(c * LANES, (c + 1) * LANES)
            heads = (2 * c, 2 * c + 1)
            swap = [(h // G_SWA) ^ half for half, h in enumerate(heads)]
            out = _pair_softmax_pv(q[:, cols], [k_scr[x, win, :] for x in swap],
                                   [v_scr[x, win, :] for x in swap], [mask, mask], lower_half,
                                   sinks=[sink_ref[h] for h in heads])
            o_ref[rows, cols] = out.astype(o_ref.dtype)


def _swa_prompt(swa, sinks, batch, seq):
    tq = QSTEP * QBLK
    nq = seq // tq
    k_col = SWA_QW // SWA_KW
    return pl.pallas_call(
        _swa_prompt_kernel,
        grid=(batch, nq),
        in_specs=[
            pl.BlockSpec(memory_space=pltpu.SMEM),
            pl.BlockSpec((tq, SWA_QW), lambda b, i: (b * nq + i, 0)),
            pl.BlockSpec((seq, SWA_KW), lambda b, i: (b, k_col)),
            pl.BlockSpec((seq, SWA_KW), lambda b, i: (b, k_col + 1)),
        ],
        out_specs=pl.BlockSpec((tq, SWA_QW), lambda b, i: (b * nq + i, 0)),
        out_shape=jax.ShapeDtypeStruct((batch * seq, SWA_QW), BF16),
        scratch_shapes=[
            pltpu.VMEM((2, seq, SWA_KW), BF16),
            pltpu.VMEM((2, seq, 2 * LANES), BF16),
        ],
        compiler_params=_params(("parallel", "arbitrary")),
        name="swa_prompt",
    )(sinks, swa, swa, swa)


def _bias_kernel(t_ref, o_ref):
    n_pad = t_ref.shape[-1]
    r = lax.broadcasted_iota(jnp.int32, (n_pad, BIAS_ROW), 0)
    w = lax.broadcasted_iota(jnp.int32, (n_pad, BIAS_ROW), 1)
    idx = jnp.clip(BIAS_ROW // 2 - 1 - w, -REL_CLIP, REL_CLIP) + REL_CLIP
    onehot = jnp.where(r == idx, 1.0, 0.0).astype(BF16)
    t = t_ref[...]
    t_hi = t.astype(BF16)
    rem = t - t_hi.astype(F32)
    t_mid = rem.astype(BF16)
    t_lo = (rem - t_mid.astype(F32)).astype(BF16)
    f = (_dot(t_hi, onehot) + _dot(t_mid, onehot)) + _dot(t_lo, onehot)
    ext = CA_EXT_TILES * LANES
    row = lax.broadcasted_iota(jnp.int32, (QBLK, ext), 0)
    colj = lax.broadcasted_iota(jnp.int32, (QBLK, ext), 1)
    ahead = colj // CHUNK - row // CHUNK
    n_prev = CA_ROWS // CHUNK
    valid = jnp.abs(2 * ahead - n_prev) <= n_prev
    for h in range(H_CA):
        x = jnp.broadcast_to(f[h:h + 1, :], (QBLK, BIAS_ROW))
        y = pltpu.roll(x, BIAS_ROW - (QBLK - 1), axis=1, stride=1, stride_axis=0)
        y = jnp.where(valid, y[:, 0:ext], NEG_INF)
        for c in range(CA_EXT_TILES):
            o_ref[h, c] = y[:, c * LANES:(c + 1) * LANES]


def _build_bias(rel_bias_table):
    depth = rel_bias_table.shape[0]
    n_pad = 3 * LANES
    t = jnp.transpose(rel_bias_table, (0, 2, 1))
    t = jnp.pad(t, ((0, 0), (0, SUBLANES - H_CA), (0, n_pad - N_REL)))
    return pl.pallas_call(
        _bias_kernel,
        grid=(depth,),
        in_specs=[pl.BlockSpec((None, SUBLANES, n_pad), lambda l: (l, 0, 0))],
        out_specs=pl.BlockSpec((None, H_CA, CA_EXT_TILES, QBLK, LANES), lambda l: (l, 0, 0, 0, 0)),
        out_shape=jax.ShapeDtypeStruct((depth, H_CA, CA_EXT_TILES, QBLK, LANES), F32),
        compiler_params=_params(("parallel",)),
        name="ca_bias",
    )(t)


def _ca_prompt_kernel(q_ref, k_ref, v_ref, bias_ref, o_ref, k_scr, v_scr):
    i = pl.program_id(1)
    n_pairs = CA_W // LANES

    @pl.when(i == 0)
    def _():
        k_scr[...] = k_ref[...].astype(BF16)
        for p in range(n_pairs):
            v_scr[p, :, 0:LANES] = v_ref[:, p * LANES:(p + 1) * LANES].astype(BF16)
            v_scr[p, :, LANES:] = jnp.ones((v_scr.shape[1], LANES), BF16)

    lower_half = lax.broadcasted_iota(jnp.int32, (QBLK, LANES), 1) < HEAD_DIM
    for u in range(QSTEP):
        rows = slice(u * QBLK, (u + 1) * QBLK)
        ideal = (i * QSTEP + u) * QBLK - CA_ROWS
        start = pl.multiple_of(jnp.maximum(ideal, 0), QBLK)
        off = (start - ideal) // LANES
        q = (q_ref[rows, :] * (HEAD_DIM ** -0.5)).astype(BF16)
        for p in range(n_pairs):
            cols = slice(p * LANES, (p + 1) * LANES)
            bias = [jnp.concatenate([bias_ref[2 * p + half, off + c] for c in range(CA_WIN_TILES)],
                                    axis=-1) for half in range(2)]
            k_win = k_scr[pl.ds(start, CA_WIN), cols]
            v_win = v_scr[p, pl.ds(start, CA_WIN), :]
            out = _pair_softmax_pv(q[:, cols], [k_win, k_win], [v_win, v_win], bias, lower_half)
            o_ref[rows, cols] = out.astype(o_ref.dtype)


def _ca_prompt(ca, bias, layer, batch, seq):
    tq = QSTEP * QBLK
    nq = seq // tq
    return pl.pallas_call(
        _ca_prompt_kernel,
        grid=(batch, nq),
        in_specs=[
            pl.BlockSpec((tq, CA_W), lambda b, i: (b * nq + i, 0)),
            pl.BlockSpec((seq, CA_W), lambda b, i: (b, 1)),
            pl.BlockSpec((seq, CA_W), lambda b, i: (b, 2)),
            pl.BlockSpec((None, H_CA, CA_EXT_TILES, QBLK, LANES), lambda b, i: (layer, 0, 0, 0, 0)),
        ],
        out_specs=pl.BlockSpec((tq, CA_W), lambda b, i: (b * nq + i, 0)),
        out_shape=jax.ShapeDtypeStruct((batch * seq, CA_W), BF16),
        scratch_shapes=[
            pltpu.VMEM((seq, CA_W), BF16),
            pltpu.VMEM((CA_W // LANES, seq, 2 * LANES), BF16),
        ],
        compiler_params=_params(("parallel", "arbitrary")),
        name="ca_prompt",
    )(ca, ca, ca, bias)


def _block_diag(blocks):
    z = jnp.zeros_like(blocks[0])
    return jnp.concatenate([jnp.concatenate([b if j == i else z for j in range(len(blocks))], axis=1)
                            for i, b in enumerate(blocks)], axis=0)


def _swap_halves_rows(x):
    half = x.shape[0] // 2
    return jnp.concatenate([x[half:], x[:half]], axis=0)


def _pair_softmax_pv_cached(q_pair, kt_halves, kn_halves, vt_halves, vn_halves, bias_c, bias_n,
                            lower_half, sinks=None):
    outs = []
    for half in range(2):
        q_h = jnp.where(lower_half, q_pair, 0) if half == 0 else jnp.where(lower_half, 0, q_pair)
        s_c = _dot(q_h, kt_halves[half]) + bias_c[half]
        s_n = _dot_nt(q_h, kn_halves[half]) + bias_n[half]
        m = jnp.maximum(jnp.max(s_c, axis=-1, keepdims=True), jnp.max(s_n, axis=-1, keepdims=True))
        if sinks is not None:
            m = jnp.maximum(m, sinks[half])
        r = (_dot_nt(jnp.exp(s_c - m).astype(BF16), vt_halves[half])
             + _dot(jnp.exp(s_n - m).astype(BF16), vn_halves[half]))
        denom = r[:, LANES:]
        if sinks is not None:
            denom = denom + jnp.exp(sinks[half] - m)
        outs.append(r[:, :LANES] / denom)
    return jnp.where(lower_half, outs[0], outs[1])


def _mixer_sample_kernel(sink_ref, ret_ref, swa_ref, ca_ref, cos_ref, sin_ref, dec_ref, qd_ref,
                         kd_ref, avg_ref, same_ref, sdec_ref, skt_ref, svt_ref, ckt_ref, cvt_ref,
                         st_ref, bias_ref, oa_ref, ob_ref, oc_ref, snew_ref, *, n_blk, t_len):
    lower_half = lax.broadcasted_iota(jnp.int32, (t_len, LANES), 1) < HEAD_DIM
    ones_rows = lambda n: jnp.ones((LANES, n), BF16)
    ones_cols = jnp.ones((t_len, LANES), BF16)
    ca_tiles = CA_ROWS // LANES
    for n in range(n_blk):
        rows = slice(n * t_len, (n + 1) * t_len)

        col = lambda c: ret_ref[rows, c * RET_W:(c + 1) * RET_W]
        pairs = [_block_diag([st_ref[n, 2 * p], st_ref[n, 2 * p + 1]]) for p in range(RET_W // LANES)]
        out, new_state = _retention_block(col(0), col(1), col(2), col(3), cos_ref[...], sin_ref[...],
                                          dec_ref[...], qd_ref[...], kd_ref[...], avg_ref[...],
                                          same_ref[...], sdec_ref[...], _block_diag(pairs))
        oa_ref[rows, :] = out.astype(oa_ref.dtype)
        for h in range(H_RET):
            sl = slice(h * HEAD_DIM, (h + 1) * HEAD_DIM)
            snew_ref[n, h] = new_state[sl, sl]

        q = (swa_ref[rows, 0:SWA_QW] * (HEAD_DIM ** -0.5)).astype(BF16)
        k_new = swa_ref[rows, SWA_QW:SWA_QW + SWA_KW]
        v_new = swa_ref[rows, SWA_QW + SWA_KW:SWA_COLS]
        k_cache = skt_ref[n]
        v_cache = svt_ref[n]
        kt = [k_cache.astype(BF16), _swap_halves_rows(k_cache).astype(BF16)]
        vt = [jnp.concatenate([x.astype(BF16), ones_rows(SWA_ROWS)], axis=0)
              for x in (v_cache, _swap_halves_rows(v_cache))]
        kn = [k_new.astype(BF16), pltpu.roll(k_new, HEAD_DIM, axis=1).astype(BF16)]
        vn = [jnp.concatenate([x.astype(BF16), ones_cols], axis=1)
              for x in (v_new, pltpu.roll(v_new, HEAD_DIM, axis=1))]
        for c in range(SWA_QW // LANES):
            cols = slice(c * LANES, (c + 1) * LANES)
            heads = (2 * c, 2 * c + 1)
            swap = [(h // G_SWA) ^ half for half, h in enumerate(heads)]
            out = _pair_softmax_pv_cached(q[:, cols], [kt[x] for x in swap], [kn[x] for x in swap],
                                          [vt[x] for x in swap], [vn[x] for x in swap],
                                          [0.0, 0.0], [0.0, 0.0], lower_half,
                                          sinks=[sink_ref[h] for h in heads])
            ob_ref[rows, cols] = out.astype(ob_ref.dtype)

        q = (ca_ref[rows, 0:CA_W] * (HEAD_DIM ** -0.5)).astype(BF16)
        k_new = ca_ref[rows, CA_W:2 * CA_W]
        v_new = ca_ref[rows, 2 * CA_W:CA_COLS]
        k_cache = ckt_ref[n]
        v_cache = cvt_ref[n]
        for p in range(CA_W // LANES):
            cols = slice(p * LANES, (p + 1) * LANES)
            kt_p = k_cache[cols, :].astype(BF16)
            vt_p = jnp.concatenate([v_cache[cols, :].astype(BF16), ones_rows(CA_ROWS)], axis=0)
            kn_p = k_new[:, cols].astype(BF16)
            vn_p = jnp.concatenate([v_new[:, cols].astype(BF16), ones_cols], axis=1)
            bias_c = [jnp.concatenate([bias_ref[2 * p + half, c, 0:t_len, :] for c in range(ca_tiles)],
                                      axis=-1) for half in range(2)]
            bias_n = [bias_ref[2 * p + half, ca_tiles, 0:t_len, 0:t_len] for half in range(2)]
            out = _pair_softmax_pv_cached(q[:, cols], [kt_p, kt_p], [kn_p, kn_p], [vt_p, vt_p],
                                          [vn_p, vn_p], bias_c, bias_n, lower_half)
            oc_ref[rows, cols] = out.astype(oc_ref.dtype)


def _mixer_sample(ret, swa, ca, sinks, tables, rope, cache_skt, cache_svt, cache_ckt, cache_cvt,
                  state_ret, bias, layer, n_seq, t_len, n_blk):
    m = n_seq * t_len
    cos, sin = rope
    whole = lambda shape: pl.BlockSpec(shape, lambda b: (0,) * len(shape))
    seq_rows = lambda width: pl.BlockSpec((n_blk * t_len, width), lambda b: (b, 0))
    cache = lambda feat, rows: pl.BlockSpec((None, n_blk, feat, rows), lambda b: (layer, b, 0, 0))
    return pl.pallas_call(
        functools.partial(_mixer_sample_kernel, n_blk=n_blk, t_len=t_len),
        grid=(n_seq // n_blk,),
        in_specs=[
            pl.BlockSpec(memory_space=pltpu.SMEM),
            seq_rows(RET_COLS), seq_rows(SWA_COLS), seq_rows(CA_COLS),
            whole((t_len, RET_W)), whole((t_len, RET_W)),
            *[whole(a.shape) for a in tables],
            cache(SWA_KW, SWA_ROWS), cache(SWA_KW, SWA_ROWS),
            cache(CA_W, CA_ROWS), cache(CA_W, CA_ROWS),
            pl.BlockSpec((None, n_blk, H_RET, HEAD_DIM, HEAD_DIM), lambda b: (layer, b, 0, 0, 0)),
            pl.BlockSpec((None, H_CA, CA_EXT_TILES, QBLK, LANES), lambda b: (layer, 0, 0, 0, 0)),
        ],
        out_specs=[
            seq_rows(RET_W), seq_rows(SWA_QW), seq_rows(CA_W),
            pl.BlockSpec((n_blk, H_RET, HEAD_DIM, HEAD_DIM), lambda b: (b, 0, 0, 0)),
        ],
        out_shape=[
            jax.ShapeDtypeStruct((m, RET_W), BF16),
            jax.ShapeDtypeStruct((m, SWA_QW), BF16),
            jax.ShapeDtypeStruct((m, CA_W), BF16),
            jax.ShapeDtypeStruct((n_seq, H_RET, HEAD_DIM, HEAD_DIM), F32),
        ],
        compiler_params=_params(("parallel",)),
        name="mixer_sample",
    )(sinks, ret, swa, ca, cos, sin, *tables, cache_skt, cache_svt, cache_ckt, cache_cvt, state_ret,
      bias)


def _outproj_kernel(x_ref, a_ref, b_ref, c_ref, w_ref, o_ref):
    acc = x_ref[...] + _dot(a_ref[...], w_ref[0:RET_W, :])
    acc = acc + _dot(b_ref[...], w_ref[RET_W:RET_W + SWA_QW, :])
    o_ref[...] = acc + _dot(c_ref[...], w_ref[RET_W + SWA_QW:D_MODEL, :])


def _outproj(x, oa, ob, oc, w_out, layer, tm):
    m = x.shape[0]
    row = lambda i: (i, 0)
    return pl.pallas_call(
        _outproj_kernel,
        grid=(m // tm,),
        in_specs=[
            pl.BlockSpec((tm, D_MODEL), row),
            pl.BlockSpec((tm, RET_W), row),
            pl.BlockSpec((tm, SWA_QW), row),
            pl.BlockSpec((tm, CA_W), row),
            pl.BlockSpec((None, D_MODEL, D_MODEL), lambda i: (layer, 0, 0)),
        ],
        out_specs=pl.BlockSpec((tm, D_MODEL), row),
        out_shape=jax.ShapeDtypeStruct((m, D_MODEL), F32),
        compiler_params=_params(("parallel",)),
        name="outproj",
    )(x, oa, ob, oc, w_out)


def _ffn_kernel(*refs, tm, tf, pad, shift, tiles_per_seq, final_norm):
    if final_norm:
        (x_ref, gain_ref, wg_ref, wu_ref, wd_ref, cw_ref, cb_ref, prev_ref, gf_ref,
         o_ref, st_ref, hn_scr, act_scr, hist_scr) = refs
    else:
        (x_ref, gain_ref, wg_ref, wu_ref, wd_ref, cw_ref, cb_ref, prev_ref,
         o_ref, st_ref, hn_scr, act_scr, hist_scr) = refs
    hn_scr[...] = _rms(x_ref[...], gain_ref[...]).astype(BF16)
    if tiles_per_seq == 1:
        hist = prev_ref
    else:
        hist = hist_scr

        @pl.when((pl.program_id(0) % tiles_per_seq) == 0)
        def _():
            hist_scr[...] = prev_ref[...]

    for c in range(D_FF // tf):
        cols = slice(c * tf, (c + 1) * tf)
        hn = hn_scr[...]
        g = _dot(hn, wg_ref[:, cols])
        u = _dot(hn, wu_ref[:, cols])
        g_ext = jnp.concatenate([hist[:, cols], g], axis=0)
        gc = cb_ref[:, cols] + cw_ref[0:1, cols] * pltpu.roll(g_ext, 2 * shift, axis=0)[pad:]
        gc = gc + cw_ref[1:2, cols] * pltpu.roll(g_ext, shift, axis=0)[pad:]
        gc = gc + cw_ref[2:3, cols] * g
        act_scr[:, cols] = (jax.nn.gelu(gc) * u).astype(BF16)
        tail = g_ext[tm:tm + pad]
        if tiles_per_seq > 1:
            hist_scr[:, cols] = tail
        st_ref[:, cols] = tail

    y = x_ref[...] + _dot(act_scr[...], wd_ref[...])
    if final_norm:
        y = _rms(y, gf_ref[...])
    o_ref[...] = y


def _ffn(x, gain, w_gate, w_up, w_down, conv_w, conv_b, prev, layer, tm, tf, shift, tiles_per_seq,
         final_gain=None):
    m = x.shape[0]
    n_groups, pad, _ = prev.shape
    final_norm = final_gain is not None
    resident = lambda shape: pl.BlockSpec((None,) + shape, lambda i: (layer, 0, 0),
                                          pipeline_mode=pl.Buffered(1))
    state = pl.BlockSpec((None, pad, D_FF), lambda i: (i // tiles_per_seq, 0, 0))
    in_specs = [
        pl.BlockSpec((tm, D_MODEL), lambda i: (i, 0)),
        resident((1, D_MODEL)),
        resident((D_MODEL, D_FF)), resident((D_MODEL, D_FF)), resident((D_FF, D_MODEL)),
        resident((CONV_W, D_FF)), resident((1, D_FF)),
        state,
    ]
    args = [x, gain, w_gate, w_up, w_down, conv_w, conv_b, prev]
    if final_norm:
        in_specs.append(pl.BlockSpec((1, D_MODEL), lambda i: (0, 0)))
        args.append(final_gain)
    return pl.pallas_call(
        functools.partial(_ffn_kernel, tm=tm, tf=tf, pad=pad, shift=shift,
                          tiles_per_seq=tiles_per_seq, final_norm=final_norm),
        grid=(m // tm,),
        in_specs=in_specs,
        out_specs=[pl.BlockSpec((tm, D_MODEL), lambda i: (i, 0)), state],
        out_shape=[
            jax.ShapeDtypeStruct((m, D_MODEL), F32),
            jax.ShapeDtypeStruct((n_groups, pad, D_FF), F32),
        ],
        scratch_shapes=[
            pltpu.VMEM((tm, D_MODEL), BF16),
            pltpu.VMEM((tm, D_FF), BF16),
            pltpu.VMEM((pad, D_FF), F32),
        ],
        compiler_params=_params(("arbitrary",)),
        name="ffn",
    )(*args)


def kernel(x_prompt, x_sample, cache_swa_k, cache_swa_v, cache_ca_k, cache_ca_v, state_ret,
           state_ffn_conv, w_in, w_out, attn_sinks, rel_bias_table, norm_mix, norm_ffn, w_gate,
           w_up, conv_w, conv_b, w_down, norm_final):
    batch, seq, _ = x_prompt.shape
    n_seq, t_len, _ = x_sample.shape
    depth = w_in.shape[0]
    assert seq % (QSTEP * QBLK) == 0 and seq >= CA_WIN and t_len % SUBLANES == 0
    assert cache_swa_k.shape[2] == SWA_ROWS and cache_ca_k.shape[2] == CA_ROWS

    w_in_b = w_in.astype(BF16)
    w_out_b = w_out.astype(BF16)
    w_gate_b = w_gate.astype(BF16)
    w_up_b = w_up.astype(BF16)
    w_down_b = w_down.astype(BF16)
    g_mix = norm_mix.reshape(depth, 1, D_MODEL)
    g_ffn = norm_ffn.reshape(depth, 1, D_MODEL)
    g_fin = norm_final.reshape(1, D_MODEL)
    cb = conv_b.reshape(depth, 1, D_FF)

    rope_p = _rope_tables(jnp.arange(seq))
    rope_s = _rope_tables(PAST_LEN + jnp.arange(t_len))
    tab_p = _retention_tables(QBLK)
    tab_s = _retention_tables(t_len)
    bias = _build_bias(rel_bias_table)

    feature_major = lambda c: jnp.transpose(c, (0, 1, 3, 4, 2)).reshape(depth, n_seq, -1, c.shape[2])
    cskt, csvt, cckt, ccvt = map(feature_major, (cache_swa_k, cache_swa_v, cache_ca_k, cache_ca_v))

    hp = x_prompt.reshape(batch * seq, D_MODEL)
    m_s = n_seq * t_len
    hs = x_sample.reshape(m_s, D_MODEL)

    pad_p = SUBLANES
    pad_s = (CONV_W - 1) * n_seq
    prev_p = jnp.zeros((batch, pad_p, D_FF), F32)
    tm_p = 512
    tiles_per_seq = seq // tm_p

    outs_p = [[] for _ in range(6)]
    outs_s = [[] for _ in range(6)]
    for l in range(depth):
        last = l == depth - 1
        ret, swa, ca = _inproj(hp, g_mix, w_in_b, l, tm=512)
        oa, s_ret = _ret_prompt(ret, batch, seq, tab_p, rope_p, n_blk=4)
        ob = _swa_prompt(swa, attn_sinks[l], batch, seq)
        oc = _ca_prompt(ca, bias, l, batch, seq)
        hp = _outproj(hp, oa, ob, oc, w_out_b, l, tm=512)
        hp, conv_st = _ffn(hp, g_ffn, w_gate_b, w_up_b, w_down_b, conv_w, cb, prev_p, l, tm=tm_p,
                           tf=256, shift=1, tiles_per_seq=tiles_per_seq,
                           final_gain=g_fin if last else None)
        swa3 = swa.reshape(batch, seq, SWA_COLS)
        ca3 = ca.reshape(batch, seq, CA_COLS)
        outs_p[0].append(swa3[:, seq - SWA_ROWS:, SWA_QW:SWA_QW + SWA_KW])
        outs_p[1].append(swa3[:, seq - SWA_ROWS:, SWA_QW + SWA_KW:])
        outs_p[2].append(ca3[:, seq - CA_ROWS:, CA_W:2 * CA_W])
        outs_p[3].append(ca3[:, seq - CA_ROWS:, 2 * CA_W:])
        outs_p[4].append(s_ret)
        outs_p[5].append(conv_st[:, pad_p - (CONV_W - 1):, :])

        ret, swa, ca = _inproj(hs, g_mix, w_in_b, l, tm=m_s // 2)
        oa, ob, oc, s_new = _mixer_sample(ret, swa, ca, attn_sinks[l], tab_s, rope_s, cskt, csvt,
                                          cckt, ccvt, state_ret, bias, l, n_seq, t_len, n_blk=4)
        hs = _outproj(hs, oa, ob, oc, w_out_b, l, tm=m_s // 2)
        hs_tm = jnp.transpose(hs.reshape(n_seq, t_len, D_MODEL), (1, 0, 2)).reshape(m_s, D_MODEL)
        prev_s = jnp.transpose(state_ffn_conv[l], (1, 0, 2)).reshape(1, pad_s, D_FF)
        hs_tm, conv_st = _ffn(hs_tm, g_ffn, w_gate_b, w_up_b, w_down_b, conv_w, cb, prev_s, l,
                              tm=m_s, tf=256, shift=n_seq, tiles_per_seq=1,
                              final_gain=g_fin if last else None)
        hs = jnp.transpose(hs_tm.reshape(t_len, n_seq, D_MODEL), (1, 0, 2)).reshape(m_s, D_MODEL)
        swa3 = swa.reshape(n_seq, t_len, SWA_COLS)
        ca3 = ca.reshape(n_seq, t_len, CA_COLS)
        outs_s[0].append(swa3[:, :, SWA_QW:SWA_QW + SWA_KW])
        outs_s[1].append(swa3[:, :, SWA_QW + SWA_KW:])
        outs_s[2].append(ca3[:, :, CA_W:2 * CA_W])
        outs_s[3].append(ca3[:, :, 2 * CA_W:])
        outs_s[4].append(s_new)
        outs_s[5].append(jnp.transpose(conv_st.reshape(CONV_W - 1, n_seq, D_FF), (1, 0, 2)))

    y_prompt = hp.reshape(batch, seq, D_MODEL)
    y_sample = hs.reshape(n_seq, t_len, D_MODEL)
    stack = lambda v: jnp.stack(v, axis=0)
    swa_k_p = stack(outs_p[0]).reshape(depth, batch, SWA_ROWS, KV_SWA, HEAD_DIM)
    swa_v_p = stack(outs_p[1]).reshape(depth, batch, SWA_ROWS, KV_SWA, HEAD_DIM)
    ca_k_p = stack(outs_p[2]).reshape(depth, batch, CA_ROWS, H_CA, HEAD_DIM)
    ca_v_p = stack(outs_p[3]).reshape(depth, batch, CA_ROWS, H_CA, HEAD_DIM)
    ret_p = stack(outs_p[4])
    conv_p = stack(outs_p[5])

    def roll_cache(cache_t, new_rows, heads):
        new_t = jnp.transpose(stack(new_rows), (0, 1, 3, 2))
        out_t = jnp.concatenate([cache_t[:, :, :, t_len:], new_t], axis=-1)
        return jnp.transpose(out_t.reshape(depth, n_seq, heads, HEAD_DIM, -1), (0, 1, 4, 2, 3))

    swa_k_s = roll_cache(cskt, outs_s[0], KV_SWA)
    swa_v_s = roll_cache(csvt, outs_s[1], KV_SWA)
    ca_k_s = roll_cache(cckt, outs_s[2], H_CA)
    ca_v_s = roll_cache(ccvt, outs_s[3], H_CA)
    ret_s = stack(outs_s[4])
    conv_s = stack(outs_s[5])
    return (y_prompt, y_sample, swa_k_p, swa_v_p, ca_k_p, ca_v_p, ret_p, conv_p,
            swa_k_s, swa_v_s, ca_k_s, ca_v_s, ret_s, conv_s)
```

```python
import functools

import jax
import jax.numpy as jnp
from jax import lax
from jax.experimental import pallas as pl
from jax.experimental.pallas import tpu as pltpu

F32 = jnp.float32
BF16 = jnp.bfloat16

D_MODEL = 1024
CHUNK = 64
HEAD_DIM = 64
H_RET = 6
H_SWA = 6
KV_SWA = 2
G_SWA = H_SWA // KV_SWA
H_CA = 4
SWA_ROWS = 128
CA_ROWS = 512
REL_CLIP = 128
N_REL = 2 * REL_CLIP + 1
D_FF = 2816
CONV_W = 3
EPS = 1e-6
ROPE_BASE = 10000.0
NEG_INF = -1e30
PAST_LEN = 4096

RET_W = H_RET * HEAD_DIM
RET_COLS = 4 * RET_W
SWA_QW = H_SWA * HEAD_DIM
SWA_KW = KV_SWA * HEAD_DIM
SWA_COLS = SWA_QW + 2 * SWA_KW
CA_W = H_CA * HEAD_DIM
CA_COLS = 3 * CA_W
IN_COLS = RET_COLS + SWA_COLS + CA_COLS

LANES = 128
SUBLANES = 8
VMEM_LIMIT_BYTES = 56 * 1024 * 1024

QBLK = 128
QSTEP = 4
SWA_WIN = QBLK + SWA_ROWS
CA_WIN = QBLK + CA_ROWS
CA_WIN_TILES = CA_WIN // LANES
CA_EXT_TILES = (CA_WIN + CA_ROWS) // LANES
BIAS_ROW = CA_EXT_TILES * LANES + LANES


def _params(semantics):
    return pltpu.CompilerParams(dimension_semantics=semantics,
                                vmem_limit_bytes=VMEM_LIMIT_BYTES)


def _rms(x, gain):
    ms = jnp.mean(x * x, axis=-1, keepdims=True)
    return (x * lax.rsqrt(ms + EPS)) * gain


def _dot(a, b):
    return jnp.dot(a, b, preferred_element_type=F32)


def _dot_nt(a, b):
    return lax.dot_general(a, b, (((1,), (1,)), ((), ())), preferred_element_type=F32)


def _dot_tn(a, b):
    return lax.dot_general(a, b, (((0,), (0,)), ((), ())), preferred_element_type=F32)


def _inproj_kernel(x_ref, g_ref, w_ref, ret_ref, swa_ref, ca_ref):
    h = _rms(x_ref[...], g_ref[...]).astype(BF16)
    ret_ref[...] = _dot(h, w_ref[:, 0:RET_COLS])
    swa_ref[...] = _dot(h, w_ref[:, RET_COLS:RET_COLS + SWA_COLS])
    ca_ref[...] = _dot(h, w_ref[:, RET_COLS + SWA_COLS:IN_COLS])


def _inproj(x, gain, w_in, layer, tm):
    m = x.shape[0]
    row = lambda i: (i, 0)
    return pl.pallas_call(
        _inproj_kernel,
        grid=(m // tm,),
        in_specs=[
            pl.BlockSpec((tm, D_MODEL), row),
            pl.BlockSpec((None, 1, D_MODEL), lambda i: (layer, 0, 0)),
            pl.BlockSpec((None, D_MODEL, IN_COLS), lambda i: (layer, 0, 0)),
        ],
        out_specs=[
            pl.BlockSpec((tm, RET_COLS), row),
            pl.BlockSpec((tm, SWA_COLS), row),
            pl.BlockSpec((tm, CA_COLS), row),
        ],
        out_shape=[
            jax.ShapeDtypeStruct((m, RET_COLS), F32),
            jax.ShapeDtypeStruct((m, SWA_COLS), F32),
            jax.ShapeDtypeStruct((m, CA_COLS), F32),
        ],
        compiler_params=_params(("parallel",)),
        name="inproj",
    )(x, gain, w_in)


def _ret_log_decay():
    return jnp.log(1.0 - 2.0 ** (-5.0 - jnp.arange(H_RET, dtype=F32)))


def _rope_tables(pos):
    half = HEAD_DIM // 2
    inv = 1.0 / (ROPE_BASE ** (jnp.arange(half, dtype=F32) / half))
    ang = pos.astype(F32)[:, None] * inv[None, :]
    cos = jnp.cos(ang)
    sin = jnp.sin(ang)
    cos_h = jnp.concatenate([cos, cos], axis=-1)
    sin_h = jnp.concatenate([-sin, sin], axis=-1)
    return jnp.tile(cos_h, (1, H_RET)), jnp.tile(sin_h, (1, H_RET))


def _retention_tables(t_len):
    log_g = _ret_log_decay()
    t = jnp.arange(t_len, dtype=F32)
    diff = t[:, None] - t[None, :]
    decay = jnp.where(diff >= 0, jnp.exp(jnp.maximum(diff, 0.0)[None] * log_g[:, None, None]), 0.0)
    q_decay = jnp.exp((t + 1.0)[:, None] * log_g[None, :])
    k_decay = jnp.exp((t_len - 1.0 - t)[:, None] * log_g[None, :])
    head = jnp.arange(RET_W) // HEAD_DIM
    same_head = (head[:, None] == head[None, :]).astype(F32)
    state_decay = jnp.broadcast_to(jnp.exp(t_len * log_g)[head][:, None], (RET_W, RET_W))
    return (decay.reshape(H_RET * t_len, t_len), jnp.repeat(q_decay, HEAD_DIM, axis=1),
            jnp.repeat(k_decay, HEAD_DIM, axis=1), (same_head / HEAD_DIM).astype(BF16), same_head,
            state_decay)


def _rope(x, cos, sin):
    n = x.shape[-1]
    lane = lax.broadcasted_iota(jnp.int32, x.shape, 1)
    first_half = (lane & (HEAD_DIM - 1)) < (HEAD_DIM // 2)
    partner = jnp.where(first_half,
                        pltpu.roll(x, n - HEAD_DIM // 2, axis=1),
                        pltpu.roll(x, HEAD_DIM // 2, axis=1))
    return x * cos + partner * sin


def _group_mean(x, head_avg):
    t_len = x.shape[0]
    hi = x.astype(BF16)
    lo = (x - hi.astype(F32)).astype(BF16)
    r = _dot(jnp.concatenate([hi, lo], axis=0), head_avg)
    return r[:t_len] + r[t_len:]


def _retention_block(q, k, v, gate, cos, sin, dec, q_decay, k_decay, head_avg, same_head,
                     state_decay, state):
    t_len = q.shape[0]
    q = _rope(q, cos, sin)
    k = _rope(k, cos, sin) * (HEAD_DIM ** -0.5)
    qb = q.astype(BF16)
    kb = k.astype(BF16)
    kdb = (k * k_decay).astype(BF16)
    vb = v.astype(BF16)
    lane_head = lax.broadcasted_iota(jnp.int32, (t_len, RET_W), 1) // HEAD_DIM
    q_stack = jnp.concatenate([jnp.where(lane_head == h, q, 0.0).astype(BF16) for h in range(H_RET)],
                              axis=0)
    scores = _dot_nt(q_stack, kb) * dec
    inner_all = _dot(scores.astype(BF16), vb)
    inner = inner_all[0:t_len]
    for h in range(1, H_RET):
        inner = jnp.where(lane_head == h, inner_all[h * t_len:(h + 1) * t_len], inner)
    o = inner + _dot(qb, state.astype(BF16)) * q_decay
    d = o - _group_mean(o, head_avg)
    var = _group_mean(d * d, head_avg)
    on = d * lax.rsqrt(var + EPS)
    new_state = state_decay * state + same_head * _dot_tn(kdb, vb)
    return on * (gate * jax.nn.sigmoid(gate)), new_state


def _ret_prompt_kernel(q_ref, k_ref, v_ref, g_ref, cos_ref, sin_ref, dec_ref, qd_ref, kd_ref,
                       avg_ref, same_ref, sdec_ref, o_ref, s_ref, state_scr):
    t = pl.program_id(1)

    @pl.when(t == 0)
    def _():
        state_scr[...] = jnp.zeros_like(state_scr)

    n_blk = q_ref.shape[0]
    for n in range(n_blk):
        out, new_state = _retention_block(q_ref[n], k_ref[n], v_ref[n], g_ref[n], cos_ref[...],
                                          sin_ref[...], dec_ref[...], qd_ref[...], kd_ref[...],
                                          avg_ref[...], same_ref[...], sdec_ref[...], state_scr[n])
        o_ref[n] = out.astype(o_ref.dtype)
        state_scr[n] = new_state

    @pl.when(t == pl.num_programs(1) - 1)
    def _():
        for n in range(n_blk):
            for h in range(H_RET):
                sl = slice(h * HEAD_DIM, (h + 1) * HEAD_DIM)
                s_ref[n, h] = state_scr[n, sl, sl]


def _ret_prompt(ret, batch, seq, tables, rope, n_blk):
    tb = QBLK
    cos, sin = rope
    ret3 = ret.reshape(batch, seq, RET_COLS)
    col = lambda c: pl.BlockSpec((n_blk, tb, RET_W), lambda b, t: (b, t, c))
    pos = pl.BlockSpec((tb, RET_W), lambda b, t: (t, 0))
    whole = lambda a: pl.BlockSpec(a.shape, lambda b, t: (0, 0))
    out, s_fin = pl.pallas_call(
        _ret_prompt_kernel,
        grid=(batch // n_blk, seq // tb),
        in_specs=[col(0), col(1), col(2), col(3), pos, pos] + [whole(a) for a in tables],
        out_specs=[
            pl.BlockSpec((n_blk, tb, RET_W), lambda b, t: (b, t, 0)),
            pl.BlockSpec((n_blk, H_RET, HEAD_DIM, HEAD_DIM), lambda b, t: (b, 0, 0, 0)),
        ],
        out_shape=[
            jax.ShapeDtypeStruct((batch, seq, RET_W), BF16),
            jax.ShapeDtypeStruct((batch, H_RET, HEAD_DIM, HEAD_DIM), F32),
        ],
        scratch_shapes=[pltpu.VMEM((n_blk, RET_W, RET_W), F32)],
        compiler_params=_params(("parallel", "arbitrary")),
        name="ret_prompt",
    )(ret3, ret3, ret3, ret3, cos, sin, *tables)
    return out.reshape(batch * seq, RET_W), s_fin


def _pair_softmax_pv(q_pair, k_halves, v_halves, bias_halves, lower_half, sinks=None):
    outs = []
    for half in range(2):
        q_h = jnp.where(lower_half, q_pair, 0) if half == 0 else jnp.where(lower_half, 0, q_pair)
        s = _dot_nt(q_h, k_halves[half]) + bias_halves[half]
        m = jnp.max(s, axis=-1, keepdims=True)
        if sinks is not None:
            m = jnp.maximum(m, sinks[half])
        r = _dot(jnp.exp(s - m).astype(BF16), v_halves[half])
        denom = r[:, LANES:]
        if sinks is not None:
            denom = denom + jnp.exp(sinks[half] - m)
        outs.append(r[:, :LANES] / denom)
    return jnp.where(lower_half, outs[0], outs[1])


def _swa_prompt_kernel(sink_ref, q_ref, k_ref, v_ref, o_ref, k_scr, v_scr):
    i = pl.program_id(1)

    @pl.when(i == 0)
    def _():
        k = k_ref[...]
        v = v_ref[...]
        ones = jnp.ones((k.shape[0], LANES), BF16)
        for x, (kx, vx) in enumerate(((k, v), (pltpu.roll(k, HEAD_DIM, axis=1),
                                               pltpu.roll(v, HEAD_DIM, axis=1)))):
            k_scr[x] = kx.astype(BF16)
            v_scr[x, :, 0:LANES] = vx.astype(BF16)
            v_scr[x, :, LANES:] = ones

    row = lax.broadcasted_iota(jnp.int32, (QBLK, SWA_WIN), 0)
    col = lax.broadcasted_iota(jnp.int32, (QBLK, SWA_WIN), 1)
    n_prev = SWA_ROWS // CHUNK
    lower_half = lax.broadcasted_iota(jnp.int32, (QBLK, LANES), 1) < HEAD_DIM
    for u in range(QSTEP):
        blk = i * QSTEP + u
        rows = slice(u * QBLK, (u + 1) * QBLK)
        start = pl.multiple_of(jnp.maximum(blk * QBLK - SWA_ROWS, 0), QBLK)
        win = pl.ds(start, SWA_WIN)
        q = (q_ref[rows, :] * (HEAD_DIM ** -0.5)).astype(BF16)
        behind = (blk * (QBLK // CHUNK) + row // CHUNK) - (start // CHUNK + col // CHUNK)
        mask = jnp.where(jnp.abs(2 * behind - n_prev) <= n_prev, 0.0, NEG_INF)
        for c in range(SWA_QW // LANES):
            cols = slice(c * LANES, (c + 1) * LANES)
            heads = (2 * c, 2 * c + 1)
            swap = [(h // G_SWA) ^ half for half, h in enumerate(heads)]
            out = _pair_softmax_pv(q[:, cols], [k_scr[x, win, :] for x in swap],
                                   [v_scr[x, win, :] for x in swap], [mask, mask], lower_half,
                                   sinks=[sink_ref[h] for h in heads])
            o_ref[rows, cols] = out.astype(o_ref.dtype)


def _swa_prompt(swa, sinks, batch, seq):
    tq = QSTEP * QBLK
    nq = seq // tq
    k_col = SWA_QW // SWA_KW
    return pl.pallas_call(
        _swa_prompt_kernel,
        grid=(batch, nq),
        in_specs=[
            pl.BlockSpec(memory_space=pltpu.SMEM),
            pl.BlockSpec((tq, SWA_QW), lambda b, i: (b * nq + i, 0)),
            pl.BlockSpec((seq, SWA_KW), lambda b, i: (b, k_col)),
            pl.BlockSpec((seq, SWA_KW), lambda b, i: (b, k_col + 1)),
        ],
        out_specs=pl.BlockSpec((tq, SWA_QW), lambda b, i: (b * nq + i, 0)),
        out_shape=jax.ShapeDtypeStruct((batch * seq, SWA_QW), BF16),
        scratch_shapes=[
            pltpu.VMEM((2, seq, SWA_KW), BF16),
            pltpu.VMEM((2, seq, 2 * LANES), BF16),
        ],
        compiler_params=_params(("parallel", "arbitrary")),
        name="swa_prompt",
    )(sinks, swa, swa, swa)


def _bias_kernel(t_ref, o_ref):
    n_pad = t_ref.shape[-1]
    r = lax.broadcasted_iota(jnp.int32, (n_pad, BIAS_ROW), 0)
    w = lax.broadcasted_iota(jnp.int32, (n_pad, BIAS_ROW), 1)
    idx = jnp.clip(BIAS_ROW // 2 - 1 - w, -REL_CLIP, REL_CLIP) + REL_CLIP
    onehot = jnp.where(r == idx, 1.0, 0.0).astype(BF16)
    t = t_ref[...]
    t_hi = t.astype(BF16)
    rem = t - t_hi.astype(F32)
    t_mid = rem.astype(BF16)
    t_lo = (rem - t_mid.astype(F32)).astype(BF16)
    f = (_dot(t_hi, onehot) + _dot(t_mid, onehot)) + _dot(t_lo, onehot)
    ext = CA_EXT_TILES * LANES
    row = lax.broadcasted_iota(jnp.int32, (QBLK, ext), 0)
    colj = lax.broadcasted_iota(jnp.int32, (QBLK, ext), 1)
    ahead = colj // CHUNK - row // CHUNK
    n_prev = CA_ROWS // CHUNK
    valid = jnp.abs(2 * ahead - n_prev) <= n_prev
    for h in range(H_CA):
        x = jnp.broadcast_to(f[h:h + 1, :], (QBLK, BIAS_ROW))
        y = pltpu.roll(x, BIAS_ROW - (QBLK - 1), axis=1, stride=1, stride_axis=0)
        y = jnp.where(valid, y[:, 0:ext], NEG_INF)
        for c in range(CA_EXT_TILES):
            o_ref[h, c] = y[:, c * LANES:(c + 1) * LANES]


def _build_bias(rel_bias_table):
    depth = rel_bias_table.shape[0]
    n_pad = 3 * LANES
    t = jnp.transpose(rel_bias_table, (0, 2, 1))
    t = jnp.pad(t, ((0, 0), (0, SUBLANES - H_CA), (0, n_pad - N_REL)))
    return pl.pallas_call(
        _bias_kernel,
        grid=(depth,),
        in_specs=[pl.BlockSpec((None, SUBLANES, n_pad), lambda l: (l, 0, 0))],
        out_specs=pl.BlockSpec((None, H_CA, CA_EXT_TILES, QBLK, LANES), lambda l: (l, 0, 0, 0, 0)),
        out_shape=jax.ShapeDtypeStruct((depth, H_CA, CA_EXT_TILES, QBLK, LANES), F32),
        compiler_params=_params(("parallel",)),
        name="ca_bias",
    )(t)


def _ca_prompt_kernel(q_ref, k_ref, v_ref, bias_ref, o_ref, k_scr, v_scr):
    i = pl.program_id(1)
    n_pairs = CA_W // LANES

    @pl.when(i == 0)
    def _():
        k_scr[...] = k_ref[...].astype(BF16)
        for p in range(n_pairs):
            v_scr[p, :, 0:LANES] = v_ref[:, p * LANES:(p + 1) * LANES].astype(BF16)
            v_scr[p, :, LANES:] = jnp.ones((v_scr.shape[1], LANES), BF16)

    lower_half = lax.broadcasted_iota(jnp.int32, (QBLK, LANES), 1) < HEAD_DIM
    for u in range(QSTEP):
        rows = slice(u * QBLK, (u + 1) * QBLK)
        ideal = (i * QSTEP + u) * QBLK - CA_ROWS
        start = pl.multiple_of(jnp.maximum(ideal, 0), QBLK)
        off = (start - ideal) // LANES
        q = (q_ref[rows, :] * (HEAD_DIM ** -0.5)).astype(BF16)
        for p in range(n_pairs):
            cols = slice(p * LANES, (p + 1) * LANES)
            bias = [jnp.concatenate([bias_ref[2 * p + half, off + c] for c in range(CA_WIN_TILES)],
                                    axis=-1) for half in range(2)]
            k_win = k_scr[pl.ds(start, CA_WIN), cols]
            v_win = v_scr[p, pl.ds(start, CA_WIN), :]
            out = _pair_softmax_pv(q[:, cols], [k_win, k_win], [v_win, v_win], bias, lower_half)
            o_ref[rows, cols] = out.astype(o_ref.dtype)


def _ca_prompt(ca, bias, layer, batch, seq):
    tq = QSTEP * QBLK
    nq = seq // tq
    return pl.pallas_call(
        _ca_prompt_kernel,
        grid=(batch, nq),
        in_specs=[
            pl.BlockSpec((tq, CA_W), lambda b, i: (b * nq + i, 0)),
            pl.BlockSpec((seq, CA_W), lambda b, i: (b, 1)),
            pl.BlockSpec((seq, CA_W), lambda b, i: (b, 2)),
            pl.BlockSpec((None, H_CA, CA_EXT_TILES, QBLK, LANES), lambda b, i: (layer, 0, 0, 0, 0)),
        ],
        out_specs=pl.BlockSpec((tq, CA_W), lambda b, i: (b * nq + i, 0)),
        out_shape=jax.ShapeDtypeStruct((batch * seq, CA_W), BF16),
        scratch_shapes=[
            pltpu.VMEM((seq, CA_W), BF16),
            pltpu.VMEM((CA_W // LANES, seq, 2 * LANES), BF16),
        ],
        compiler_params=_params(("parallel", "arbitrary")),
        name="ca_prompt",
    )(ca, ca, ca, bias)


def _block_diag(blocks):
    z = jnp.zeros_like(blocks[0])
    return jnp.concatenate([jnp.concatenate([b if j == i else z for j in range(len(blocks))], axis=1)
                            for i, b in enumerate(blocks)], axis=0)


def _swap_halves_rows(x):
    half = x.shape[0] // 2
    return jnp.concatenate([x[half:], x[:half]], axis=0)


def _pair_softmax_pv_cached(q_pair, kt_halves, kn_halves, vt_halves, vn_halves, bias_c, bias_n,
                            lower_half, sinks=None):
    outs = []
    for half in range(2):
        q_h = jnp.where(lower_half, q_pair, 0) if half == 0 else jnp.where(lower_half, 0, q_pair)
        s_c = _dot(q_h, kt_halves[half]) + bias_c[half]
        s_n = _dot_nt(q_h, kn_halves[half]) + bias_n[half]
        m = jnp.maximum(jnp.max(s_c, axis=-1, keepdims=True), jnp.max(s_n, axis=-1, keepdims=True))
        if sinks is not None:
            m = jnp.maximum(m, sinks[half])
        r = (_dot_nt(jnp.exp(s_c - m).astype(BF16), vt_halves[half])
             + _dot(jnp.exp(s_n - m).astype(BF16), vn_halves[half]))
        denom = r[:, LANES:]
        if sinks is not None:
            denom = denom + jnp.exp(sinks[half] - m)
        outs.append(r[:, :LANES] / denom)
    return jnp.where(lower_half, outs[0], outs[1])


def _mixer_sample_kernel(sink_ref, ret_ref, swa_ref, ca_ref, cos_ref, sin_ref, dec_ref, qd_ref,
                         kd_ref, avg_ref, same_ref, sdec_ref, skt_ref, svt_ref, ckt_ref, cvt_ref,
                         st_ref, bias_ref, oa_ref, ob_ref, oc_ref, snew_ref, *, n_blk, t_len):
    lower_half = lax.broadcasted_iota(jnp.int32, (t_len, LANES), 1) < HEAD_DIM
    ones_rows = lambda n: jnp.ones((LANES, n), BF16)
    ones_cols = jnp.ones((t_len, LANES), BF16)
    ca_tiles = CA_ROWS // LANES
    for n in range(n_blk):
        rows = slice(n * t_len, (n + 1) * t_len)

        col = lambda c: ret_ref[rows, c * RET_W:(c + 1) * RET_W]
        pairs = [_block_diag([st_ref[n, 2 * p], st_ref[n, 2 * p + 1]]) for p in range(RET_W // LANES)]
        out, new_state = _retention_block(col(0), col(1), col(2), col(3), cos_ref[...], sin_ref[...],
                                          dec_ref[...], qd_ref[...], kd_ref[...], avg_ref[...],
                                          same_ref[...], sdec_ref[...], _block_diag(pairs))
        oa_ref[rows, :] = out.astype(oa_ref.dtype)
        for h in range(H_RET):
            sl = slice(h * HEAD_DIM, (h + 1) * HEAD_DIM)
            snew_ref[n, h] = new_state[sl, sl]

        q = (swa_ref[rows, 0:SWA_QW] * (HEAD_DIM ** -0.5)).astype(BF16)
        k_new = swa_ref[rows, SWA_QW:SWA_QW + SWA_KW]
        v_new = swa_ref[rows, SWA_QW + SWA_KW:SWA_COLS]
        k_cache = skt_ref[n]
        v_cache = svt_ref[n]
        kt = [k_cache.astype(BF16), _swap_halves_rows(k_cache).astype(BF16)]
        vt = [jnp.concatenate([x.astype(BF16), ones_rows(SWA_ROWS)], axis=0)
              for x in (v_cache, _swap_halves_rows(v_cache))]
        kn = [k_new.astype(BF16), pltpu.roll(k_new, HEAD_DIM, axis=1).astype(BF16)]
        vn = [jnp.concatenate([x.astype(BF16), ones_cols], axis=1)
              for x in (v_new, pltpu.roll(v_new, HEAD_DIM, axis=1))]
        for c in range(SWA_QW // LANES):
            cols = slice(c * LANES, (c + 1) * LANES)
            heads = (2 * c, 2 * c + 1)
            swap = [(h // G_SWA) ^ half for half, h in enumerate(heads)]
            out = _pair_softmax_pv_cached(q[:, cols], [kt[x] for x in swap], [kn[x] for x in swap],
                                          [vt[x] for x in swap], [vn[x] for x in swap],
                                          [0.0, 0.0], [0.0, 0.0], lower_half,
                                          sinks=[sink_ref[h] for h in heads])
            ob_ref[rows, cols] = out.astype(ob_ref.dtype)

        q = (ca_ref[rows, 0:CA_W] * (HEAD_DIM ** -0.5)).astype(BF16)
        k_new = ca_ref[rows, CA_W:2 * CA_W]
        v_new = ca_ref[rows, 2 * CA_W:CA_COLS]
        k_cache = ckt_ref[n]
        v_cache = cvt_ref[n]
        for p in range(CA_W // LANES):
            cols = slice(p * LANES, (p + 1) * LANES)
            kt_p = k_cache[cols, :].astype(BF16)
            vt_p = jnp.concatenate([v_cache[cols, :].astype(BF16), ones_rows(CA_ROWS)], axis=0)
            kn_p = k_new[:, cols].astype(BF16)
            vn_p = jnp.concatenate([v_new[:, cols].astype(BF16), ones_cols], axis=1)
            bias_c = [jnp.concatenate([bias_ref[2 * p + half, c, 0:t_len, :] for c in range(ca_tiles)],
                                      axis=-1) for half in range(2)]
            bias_n = [bias_ref[2 * p + half, ca_tiles, 0:t_len, 0:t_len] for half in range(2)]
            out = _pair_softmax_pv_cached(q[:, cols], [kt_p, kt_p], [kn_p, kn_p], [vt_p, vt_p],
                                          [vn_p, vn_p], bias_c, bias_n, lower_half)
            oc_ref[rows, cols] = out.astype(oc_ref.dtype)


def _mixer_sample(ret, swa, ca, sinks, tables, rope, cache_skt, cache_svt, cache_ckt, cache_cvt,
                  state_ret, bias, layer, n_seq, t_len, n_blk):
    m = n_seq * t_len
    cos, sin = rope
    whole = lambda shape: pl.BlockSpec(shape, lambda b: (0,) * len(shape))
    seq_rows = lambda width: pl.BlockSpec((n_blk * t_len, width), lambda b: (b, 0))
    cache = lambda feat, rows: pl.BlockSpec((None, n_blk, feat, rows), lambda b: (layer, b, 0, 0))
    return pl.pallas_call(
        functools.partial(_mixer_sample_kernel, n_blk=n_blk, t_len=t_len),
        grid=(n_seq // n_blk,),
        in_specs=[
            pl.BlockSpec(memory_space=pltpu.SMEM),
            seq_rows(RET_COLS), seq_rows(SWA_COLS), seq_rows(CA_COLS),
            whole((t_len, RET_W)), whole((t_len, RET_W)),
            *[whole(a.shape) for a in tables],
            cache(SWA_KW, SWA_ROWS), cache(SWA_KW, SWA_ROWS),
            cache(CA_W, CA_ROWS), cache(CA_W, CA_ROWS),
            pl.BlockSpec((None, n_blk, H_RET, HEAD_DIM, HEAD_DIM), lambda b: (layer, b, 0, 0, 0)),
            pl.BlockSpec((None, H_CA, CA_EXT_TILES, QBLK, LANES), lambda b: (layer, 0, 0, 0, 0)),
        ],
        out_specs=[
            seq_rows(RET_W), seq_rows(SWA_QW), seq_rows(CA_W),
            pl.BlockSpec((n_blk, H_RET, HEAD_DIM, HEAD_DIM), lambda b: (b, 0, 0, 0)),
        ],
        out_shape=[
            jax.ShapeDtypeStruct((m, RET_W), BF16),
            jax.ShapeDtypeStruct((m, SWA_QW), BF16),
            jax.ShapeDtypeStruct((m, CA_W), BF16),
            jax.ShapeDtypeStruct((n_seq, H_RET, HEAD_DIM, HEAD_DIM), F32),
        ],
        compiler_params=_params(("parallel",)),
        name="mixer_sample",
    )(sinks, ret, swa, ca, cos, sin, *tables, cache_skt, cache_svt, cache_ckt, cache_cvt, state_ret,
      bias)


def _append_rows_feature_major(cache_t, new_rows):
    n_feat, n_rows = cache_t.shape
    t_len = new_rows.shape[0]
    rolled = pltpu.roll(cache_t, n_rows - t_len, axis=1)
    t_idx = lax.broadcasted_iota(jnp.int32, (t_len, LANES), 0)
    x_idx = lax.broadcasted_iota(jnp.int32, (t_len, LANES), 1)
    place = jnp.where(x_idx == LANES - t_len + t_idx, 1.0, 0.0).astype(BF16)
    hi = new_rows.astype(BF16)
    rem = new_rows - hi.astype(F32)
    mid = rem.astype(BF16)
    lo = (rem - mid.astype(F32)).astype(BF16)
    placed = (_dot_tn(hi, place) + _dot_tn(mid, place)) + _dot_tn(lo, place)
    lane = lax.broadcasted_iota(jnp.int32, (n_feat, LANES), 1)
    last = jnp.where(lane < LANES - t_len, rolled[:, n_rows - LANES:], placed)
    if n_rows == LANES:
        return last
    return jnp.concatenate([rolled[:, :n_rows - LANES], last], axis=1)


def _cache_roll_kernel(k_ref, v_ref, nk_ref, nv_ref, ok_ref, ov_ref):
    for n in range(k_ref.shape[0]):
        ok_ref[n] = _append_rows_feature_major(k_ref[n], nk_ref[n])
        ov_ref[n] = _append_rows_feature_major(v_ref[n], nv_ref[n])


def _cache_roll(cache_kt, cache_vt, new_k, new_v, n_blk):
    depth, n_seq, n_feat, n_rows = cache_kt.shape
    t_len = new_k.shape[2]
    cache = pl.BlockSpec((None, n_blk, n_feat, n_rows), lambda l, b: (l, b, 0, 0))
    new = pl.BlockSpec((None, n_blk, t_len, n_feat), lambda l, b: (l, b, 0, 0))
    shape = jax.ShapeDtypeStruct(cache_kt.shape, F32)
    return pl.pallas_call(
        _cache_roll_kernel,
        grid=(depth, n_seq // n_blk),
        in_specs=[cache, cache, new, new],
        out_specs=[cache, cache],
        out_shape=[shape, shape],
        compiler_params=_params(("parallel", "parallel")),
        name="cache_roll",
    )(cache_kt, cache_vt, new_k, new_v)


def _ffn_kernel(*refs, tm, tf, pad, shift, tiles_per_seq, final_norm):
    if final_norm:
        (x_ref, a_ref, b_ref, c_ref, wo_ref, gain_ref, wg_ref, wu_ref, wd_ref, cw_ref, cb_ref,
         prev_ref, gf_ref, o_ref, st_ref, hn_scr, act_scr, hist_scr) = refs
    else:
        (x_ref, a_ref, b_ref, c_ref, wo_ref, gain_ref, wg_ref, wu_ref, wd_ref, cw_ref, cb_ref,
         prev_ref, o_ref, st_ref, hn_scr, act_scr, hist_scr) = refs
    x = x_ref[...] + _dot(a_ref[...], wo_ref[0:RET_W, :])
    x = x + _dot(b_ref[...], wo_ref[RET_W:RET_W + SWA_QW, :])
    x = x + _dot(c_ref[...], wo_ref[RET_W + SWA_QW:D_MODEL, :])
    o_ref[...] = x
    hn_scr[...] = _rms(x, gain_ref[...]).astype(BF16)
    if tiles_per_seq == 1:
        hist = prev_ref
    else:
        hist = hist_scr

        @pl.when((pl.program_id(0) % tiles_per_seq) == 0)
        def _():
            hist_scr[...] = prev_ref[...]

    for c in range(D_FF // tf):
        cols = slice(c * tf, (c + 1) * tf)
        hn = hn_scr[...]
        g = _dot(hn, wg_ref[:, cols])
        u = _dot(hn, wu_ref[:, cols])
        g_ext = jnp.concatenate([hist[:, cols], g], axis=0)
        gc = cb_ref[:, cols] + cw_ref[0:1, cols] * pltpu.roll(g_ext, 2 * shift, axis=0)[pad:]
        gc = gc + cw_ref[1:2, cols] * pltpu.roll(g_ext, shift, axis=0)[pad:]
        gc = gc + cw_ref[2:3, cols] * g
        act_scr[:, cols] = (jax.nn.gelu(gc) * u).astype(BF16)
        tail = g_ext[tm:tm + pad]
        if tiles_per_seq > 1:
            hist_scr[:, cols] = tail
        st_ref[:, cols] = tail

    y = o_ref[...] + _dot(act_scr[...], wd_ref[...])
    if final_norm:
        y = _rms(y, gf_ref[...])
    o_ref[...] = y


def _outproj_ffn(x, oa, ob, oc, w_out, gain, w_gate, w_up, w_down, conv_w, conv_b, prev, layer, tm,
                 tf, shift, tiles_per_seq, final_gain=None):
    m = x.shape[0]
    n_groups, pad, _ = prev.shape
    final_norm = final_gain is not None
    resident = lambda shape: pl.BlockSpec((None,) + shape, lambda i: (layer, 0, 0),
                                          pipeline_mode=pl.Buffered(1))
    state = pl.BlockSpec((None, pad, D_FF), lambda i: (i // tiles_per_seq, 0, 0))
    rows = lambda width: pl.BlockSpec((tm, width), lambda i: (i, 0))
    in_specs = [
        rows(D_MODEL), rows(RET_W), rows(SWA_QW), rows(CA_W),
        resident((D_MODEL, D_MODEL)),
        resident((1, D_MODEL)),
        resident((D_MODEL, D_FF)), resident((D_MODEL, D_FF)), resident((D_FF, D_MODEL)),
        resident((CONV_W, D_FF)), resident((1, D_FF)),
        state,
    ]
    args = [x, oa, ob, oc, w_out, gain, w_gate, w_up, w_down, conv_w, conv_b, prev]
    if final_norm:
        in_specs.append(pl.BlockSpec((1, D_MODEL), lambda i: (0, 0)))
        args.append(final_gain)
    return pl.pallas_call(
        functools.partial(_ffn_kernel, tm=tm, tf=tf, pad=pad, shift=shift,
                          tiles_per_seq=tiles_per_seq, final_norm=final_norm),
        grid=(m // tm,),
        in_specs=in_specs,
        out_specs=[rows(D_MODEL), state],
        out_shape=[
            jax.ShapeDtypeStruct((m, D_MODEL), F32),
            jax.ShapeDtypeStruct((n_groups, pad, D_FF), F32),
        ],
        scratch_shapes=[
            pltpu.VMEM((tm, D_MODEL), BF16),
            pltpu.VMEM((tm, D_FF), BF16),
            pltpu.VMEM((pad, D_FF), F32),
        ],
        compiler_params=_params(("arbitrary",)),
        name="outproj_ffn",
    )(*args)


def kernel(x_prompt, x_sample, cache_swa_k, cache_swa_v, cache_ca_k, cache_ca_v, state_ret,
           state_ffn_conv, w_in, w_out, attn_sinks, rel_bias_table, norm_mix, norm_ffn, w_gate,
           w_up, conv_w, conv_b, w_down, norm_final):
    batch, seq, _ = x_prompt.shape
    n_seq, t_len, _ = x_sample.shape
    depth = w_in.shape[0]
    assert seq % (QSTEP * QBLK) == 0 and seq >= CA_WIN and t_len % SUBLANES == 0
    assert cache_swa_k.shape[2] == SWA_ROWS and cache_ca_k.shape[2] == CA_ROWS

    w_in_b = w_in.astype(BF16)
    w_out_b = w_out.astype(BF16)
    w_gate_b = w_gate.astype(BF16)
    w_up_b = w_up.astype(BF16)
    w_down_b = w_down.astype(BF16)
    g_mix = norm_mix.reshape(depth, 1, D_MODEL)
    g_ffn = norm_ffn.reshape(depth, 1, D_MODEL)
    g_fin = norm_final.reshape(1, D_MODEL)
    cb = conv_b.reshape(depth, 1, D_FF)

    rope_p = _rope_tables(jnp.arange(seq))
    rope_s = _rope_tables(PAST_LEN + jnp.arange(t_len))
    tab_p = _retention_tables(QBLK)
    tab_s = _retention_tables(t_len)
    bias = _build_bias(rel_bias_table)

    feature_major = lambda c: jnp.transpose(c, (0, 1, 3, 4, 2)).reshape(depth, n_seq, -1, c.shape[2])
    cskt, csvt, cckt, ccvt = map(feature_major, (cache_swa_k, cache_swa_v, cache_ca_k, cache_ca_v))

    hp = x_prompt.reshape(batch * seq, D_MODEL)
    m_s = n_seq * t_len
    hs = x_sample.reshape(m_s, D_MODEL)

    pad_p = SUBLANES
    pad_s = (CONV_W - 1) * n_seq
    prev_p = jnp.zeros((batch, pad_p, D_FF), F32)
    tm_p = 512
    tiles_per_seq = seq // tm_p

    outs_p = [[] for _ in range(6)]
    outs_s = [[] for _ in range(6)]
    for l in range(depth):
        last = l == depth - 1
        ret, swa, ca = _inproj(hp, g_mix, w_in_b, l, tm=512)
        oa, s_ret = _ret_prompt(ret, batch, seq, tab_p, rope_p, n_blk=4)
        ob = _swa_prompt(swa, attn_sinks[l], batch, seq)
        oc = _ca_prompt(ca, bias, l, batch, seq)
        hp, conv_st = _outproj_ffn(hp, oa, ob, oc, w_out_b, g_ffn, w_gate_b, w_up_b, w_down_b, conv_w,
                                   cb, prev_p, l, tm=tm_p, tf=256, shift=1,
                                   tiles_per_seq=tiles_per_seq, final_gain=g_fin if last else None)
        swa3 = swa.reshape(batch, seq, SWA_COLS)
        ca3 = ca.reshape(batch, seq, CA_COLS)
        outs_p[0].append(swa3[:, seq - SWA_ROWS:, SWA_QW:SWA_QW + SWA_KW])
        outs_p[1].append(swa3[:, seq - SWA_ROWS:, SWA_QW + SWA_KW:])
        outs_p[2].append(ca3[:, seq - CA_ROWS:, CA_W:2 * CA_W])
        outs_p[3].append(ca3[:, seq - CA_ROWS:, 2 * CA_W:])
        outs_p[4].append(s_ret)
        outs_p[5].append(conv_st[:, pad_p - (CONV_W - 1):, :])

        ret, swa, ca = _inproj(hs, g_mix, w_in_b, l, tm=m_s // 2)
        oa, ob, oc, s_new = _mixer_sample(ret, swa, ca, attn_sinks[l], tab_s, rope_s, cskt, csvt,
                                          cckt, ccvt, state_ret, bias, l, n_seq, t_len, n_blk=8)
        time_major = lambda a: jnp.transpose(a.reshape(n_seq, t_len, -1), (1, 0, 2)).reshape(m_s, -1)
        prev_s = jnp.transpose(state_ffn_conv[l], (1, 0, 2)).reshape(1, pad_s, D_FF)
        hs_tm, conv_st = _outproj_ffn(time_major(hs), time_major(oa), time_major(ob), time_major(oc),
                                      w_out_b, g_ffn, w_gate_b, w_up_b, w_down_b, conv_w, cb, prev_s,
                                      l, tm=m_s, tf=256, shift=n_seq, tiles_per_seq=1,
                                      final_gain=g_fin if last else None)
        hs = jnp.transpose(hs_tm.reshape(t_len, n_seq, D_MODEL), (1, 0, 2)).reshape(m_s, D_MODEL)
        swa3 = swa.reshape(n_seq, t_len, SWA_COLS)
        ca3 = ca.reshape(n_seq, t_len, CA_COLS)
        outs_s[0].append(swa3[:, :, SWA_QW:SWA_QW + SWA_KW])
        outs_s[1].append(swa3[:, :, SWA_QW + SWA_KW:])
        outs_s[2].append(ca3[:, :, CA_W:2 * CA_W])
        outs_s[3].append(ca3[:, :, 2 * CA_W:])
        outs_s[4].append(s_new)
        outs_s[5].append(jnp.transpose(conv_st.reshape(CONV_W - 1, n_seq, D_FF), (1, 0, 2)))

    y_prompt = hp.reshape(batch, seq, D_MODEL)
    y_sample = hs.reshape(n_seq, t_len, D_MODEL)
    stack = lambda v: jnp.stack(v, axis=0)
    swa_k_p = stack(outs_p[0]).reshape(depth, batch, SWA_ROWS, KV_SWA, HEAD_DIM)
    swa_v_p = stack(outs_p[1]).reshape(depth, batch, SWA_ROWS, KV_SWA, HEAD_DIM)
    ca_k_p = stack(outs_p[2]).reshape(depth, batch, CA_ROWS, H_CA, HEAD_DIM)
    ca_v_p = stack(outs_p[3]).reshape(depth, batch, CA_ROWS, H_CA, HEAD_DIM)
    ret_p = stack(outs_p[4])
    conv_p = stack(outs_p[5])

    row_major = lambda c, heads: jnp.transpose(c.reshape(depth, n_seq, heads, HEAD_DIM, -1), (0, 1, 4, 2, 3))
    swa_kt, swa_vt = _cache_roll(cskt, csvt, stack(outs_s[0]), stack(outs_s[1]), n_blk=8)
    ca_kt, ca_vt = _cache_roll(cckt, ccvt, stack(outs_s[2]), stack(outs_s[3]), n_blk=4)
    swa_k_s = row_major(swa_kt, KV_SWA)
    swa_v_s = row_major(swa_vt, KV_SWA)
    ca_k_s = row_major(ca_kt, H_CA)
    ca_v_s = row_major(ca_vt, H_CA)
    ret_s = stack(outs_s[4])
    conv_s = stack(outs_s[5])
    return (y_prompt, y_sample, swa_k_p, swa_v_p, ca_k_p, ca_v_p, ret_p, conv_p,
            swa_k_s, swa_v_s, ca_k_s, ca_v_s, ret_s, conv_s)
```

```python
import functools

import jax
import jax.numpy as jnp
from jax import lax
from jax.experimental import pallas as pl
from jax.experimental.pallas import tpu as pltpu

F32 = jnp.float32
BF16 = jnp.bfloat16

D_MODEL = 1024
CHUNK = 64
HEAD_DIM = 64
H_RET = 6
H_SWA = 6
KV_SWA = 2
G_SWA = H_SWA // KV_SWA
H_CA = 4
SWA_ROWS = 128
CA_ROWS = 512
REL_CLIP = 128
N_REL = 2 * REL_CLIP + 1
D_FF = 2816
CONV_W = 3
EPS = 1e-6
ROPE_BASE = 10000.0
NEG_INF = -1e30
PAST_LEN = 4096

RET_W = H_RET * HEAD_DIM
RET_COLS = 4 * RET_W
SWA_QW = H_SWA * HEAD_DIM
SWA_KW = KV_SWA * HEAD_DIM
SWA_COLS = SWA_QW + 2 * SWA_KW
CA_W = H_CA * HEAD_DIM
CA_COLS = 3 * CA_W
IN_COLS = RET_COLS + SWA_COLS + CA_COLS

LANES = 128
SUBLANES = 8
VMEM_LIMIT_BYTES = 56 * 1024 * 1024

QBLK = 128
QSTEP = 4
SWA_WIN = QBLK + SWA_ROWS
CA_WIN = QBLK + CA_ROWS
CA_WIN_TILES = CA_WIN // LANES
CA_EXT_TILES = (CA_WIN + CA_ROWS) // LANES
BIAS_ROW = CA_EXT_TILES * LANES + LANES


def _params(semantics):
    return pltpu.CompilerParams(dimension_semantics=semantics,
                                vmem_limit_bytes=VMEM_LIMIT_BYTES)


def _rms(x, gain):
    ms = jnp.mean(x * x, axis=-1, keepdims=True)
    return (x * lax.rsqrt(ms + EPS)) * gain


def _dot(a, b):
    return jnp.dot(a, b, preferred_element_type=F32)


def _dot_nt(a, b):
    return lax.dot_general(a, b, (((1,), (1,)), ((), ())), preferred_element_type=F32)


def _dot_tn(a, b):
    return lax.dot_general(a, b, (((0,), (0,)), ((), ())), preferred_element_type=F32)


def _inproj_kernel(x_ref, g_ref, w_ref, ret_ref, swa_ref, ca_ref):
    h = _rms(x_ref[...], g_ref[...]).astype(BF16)
    ret_ref[...] = _dot(h, w_ref[:, 0:RET_COLS])
    swa_ref[...] = _dot(h, w_ref[:, RET_COLS:RET_COLS + SWA_COLS])
    ca_ref[...] = _dot(h, w_ref[:, RET_COLS + SWA_COLS:IN_COLS])


def _inproj(x, gain, w_in, layer, tm):
    m = x.shape[0]
    row = lambda i: (i, 0)
    return pl.pallas_call(
        _inproj_kernel,
        grid=(m // tm,),
        in_specs=[
            pl.BlockSpec((tm, D_MODEL), row),
            pl.BlockSpec((None, 1, D_MODEL), lambda i: (layer, 0, 0)),
            pl.BlockSpec((None, D_MODEL, IN_COLS), lambda i: (layer, 0, 0)),
        ],
        out_specs=[
            pl.BlockSpec((tm, RET_COLS), row),
            pl.BlockSpec((tm, SWA_COLS), row),
            pl.BlockSpec((tm, CA_COLS), row),
        ],
        out_shape=[
            jax.ShapeDtypeStruct((m, RET_COLS), F32),
            jax.ShapeDtypeStruct((m, SWA_COLS), F32),
            jax.ShapeDtypeStruct((m, CA_COLS), F32),
        ],
        compiler_params=_params(("parallel",)),
        name="inproj",
    )(x, gain, w_in)


def _ret_log_decay():
    return jnp.log(1.0 - 2.0 ** (-5.0 - jnp.arange(H_RET, dtype=F32)))


def _rope_tables(pos):
    half = HEAD_DIM // 2
    inv = 1.0 / (ROPE_BASE ** (jnp.arange(half, dtype=F32) / half))
    ang = pos.astype(F32)[:, None] * inv[None, :]
    cos = jnp.cos(ang)
    sin = jnp.sin(ang)
    cos_h = jnp.concatenate([cos, cos], axis=-1)
    sin_h = jnp.concatenate([-sin, sin], axis=-1)
    return jnp.tile(cos_h, (1, H_RET)), jnp.tile(sin_h, (1, H_RET))


def _retention_tables(t_len):
    log_g = _ret_log_decay()
    t = jnp.arange(t_len, dtype=F32)
    diff = t[:, None] - t[None, :]
    decay = jnp.where(diff >= 0, jnp.exp(jnp.maximum(diff, 0.0)[None] * log_g[:, None, None]), 0.0)
    q_decay = jnp.exp((t + 1.0)[:, None] * log_g[None, :])
    k_decay = jnp.exp((t_len - 1.0 - t)[:, None] * log_g[None, :])
    head = jnp.arange(LANES) // HEAD_DIM
    same_head = (head[:, None] == head[None, :]).astype(F32)
    blk = jnp.exp(t_len * log_g).reshape(RET_W // LANES, LANES // HEAD_DIM)
    state_decay = jnp.broadcast_to(blk[:, head][:, :, None], (RET_W // LANES, LANES, LANES))
    return (decay.reshape(H_RET * t_len, t_len), jnp.repeat(q_decay, HEAD_DIM, axis=1),
            jnp.repeat(k_decay, HEAD_DIM, axis=1), (same_head / HEAD_DIM).astype(BF16), same_head,
            state_decay)


def _rope(x, cos, sin):
    nb, t_len, n = x.shape
    x2 = x.reshape(nb * t_len, n)
    lane = lax.broadcasted_iota(jnp.int32, x2.shape, 1)
    first_half = (lane & (HEAD_DIM - 1)) < (HEAD_DIM // 2)
    partner = jnp.where(first_half,
                        pltpu.roll(x2, n - HEAD_DIM // 2, axis=1),
                        pltpu.roll(x2, HEAD_DIM // 2, axis=1))
    return x * cos + partner.reshape(x.shape) * sin


def _group_mean(x, head_avg):
    nb, t_len, n = x.shape
    hi = x.astype(BF16)
    lo = (x - hi.astype(F32)).astype(BF16)
    both = jnp.concatenate([hi.reshape(nb * t_len, n), lo.reshape(nb * t_len, n)], axis=0)
    r = _dot(both, head_avg)
    return (r[:nb * t_len] + r[nb * t_len:]).reshape(x.shape)


def _bdot(spec, a, b):
    return jnp.einsum(spec, a, b, preferred_element_type=F32)


def _retention_block(q, k, v, gate, cos, sin, dec, q_decay, k_decay, head_avg, same_head,
                     state_decay, state):
    t_len = q.shape[1]
    q = _rope(q, cos, sin)
    k = _rope(k, cos, sin) * (HEAD_DIM ** -0.5)
    kd = k * k_decay
    lower_half = lax.broadcasted_iota(jnp.int32, (t_len, LANES), 1) < HEAD_DIM
    outs = []
    new_state = []
    for p in range(RET_W // LANES):
        cols = slice(p * LANES, (p + 1) * LANES)
        qp = q[:, :, cols]
        kp = k[:, :, cols].astype(BF16)
        vp = v[:, :, cols].astype(BF16)
        q_stack = jnp.concatenate([jnp.where(lower_half, qp, 0.0).astype(BF16),
                                   jnp.where(lower_half, 0.0, qp).astype(BF16)], axis=1)
        scores = _bdot('bqd,bkd->bqk', q_stack, kp) * dec[2 * p * t_len:(2 * p + 2) * t_len]
        inner_all = _bdot('bqk,bkd->bqd', scores.astype(BF16), vp)
        inner = jnp.where(lower_half, inner_all[:, :t_len], inner_all[:, t_len:])
        cross = _bdot('bqd,bde->bqe', qp.astype(BF16), state[p].astype(BF16)) * q_decay[:, cols]
        o = inner + cross
        d = o - _group_mean(o, head_avg)
        var = _group_mean(d * d, head_avg)
        outs.append(d * lax.rsqrt(var + EPS))
        new_state.append(state_decay[p] * state[p]
                         + same_head * _bdot('bkd,bke->bde', kd[:, :, cols].astype(BF16), vp))
    on = jnp.concatenate(outs, axis=-1)
    return on * (gate * jax.nn.sigmoid(gate)), new_state


def _ret_prompt_kernel(q_ref, k_ref, v_ref, g_ref, cos_ref, sin_ref, dec_ref, qd_ref, kd_ref,
                       avg_ref, same_ref, sdec_ref, o_ref, s_ref, state_scr):
    t = pl.program_id(1)
    n_cols = RET_W // LANES

    @pl.when(t == 0)
    def _():
        state_scr[...] = jnp.zeros_like(state_scr)

    out, new_state = _retention_block(q_ref[...], k_ref[...], v_ref[...], g_ref[...], cos_ref[...],
                                      sin_ref[...], dec_ref[...], qd_ref[...], kd_ref[...],
                                      avg_ref[...], same_ref[...], sdec_ref[...],
                                      [state_scr[p] for p in range(n_cols)])
    o_ref[...] = out.astype(o_ref.dtype)
    for p in range(n_cols):
        state_scr[p] = new_state[p]

    @pl.when(t == pl.num_programs(1) - 1)
    def _():
        for p in range(n_cols):
            s_ref[:, 2 * p] = state_scr[p, :, 0:HEAD_DIM, 0:HEAD_DIM]
            s_ref[:, 2 * p + 1] = state_scr[p, :, HEAD_DIM:, HEAD_DIM:]


def _ret_prompt(ret, batch, seq, tables, rope, n_blk):
    tb = QBLK
    cos, sin = rope
    ret3 = ret.reshape(batch, seq, RET_COLS)
    col = lambda c: pl.BlockSpec((n_blk, tb, RET_W), lambda b, t: (b, t, c))
    pos = pl.BlockSpec((tb, RET_W), lambda b, t: (t, 0))
    whole = lambda a: pl.BlockSpec(a.shape, lambda b, t: (0,) * a.ndim)
    out, s_fin = pl.pallas_call(
        _ret_prompt_kernel,
        grid=(batch // n_blk, seq // tb),
        in_specs=[col(0), col(1), col(2), col(3), pos, pos] + [whole(a) for a in tables],
        out_specs=[
            pl.BlockSpec((n_blk, tb, RET_W), lambda b, t: (b, t, 0)),
            pl.BlockSpec((n_blk, H_RET, HEAD_DIM, HEAD_DIM), lambda b, t: (b, 0, 0, 0)),
        ],
        out_shape=[
            jax.ShapeDtypeStruct((batch, seq, RET_W), BF16),
            jax.ShapeDtypeStruct((batch, H_RET, HEAD_DIM, HEAD_DIM), F32),
        ],
        scratch_shapes=[pltpu.VMEM((RET_W // LANES, n_blk, LANES, LANES), F32)],
        compiler_params=_params(("parallel", "arbitrary")),
        name="ret_prompt",
    )(ret3, ret3, ret3, ret3, cos, sin, *tables)
    return out.reshape(batch * seq, RET_W), s_fin


def _emit_cache_tail(i, k_ref, v_ref, kt_ref, vt_ref):
    @pl.when(i == pl.num_programs(1) - 1)
    def _():
        n_rows = kt_ref.shape[-1]
        seq = k_ref.shape[0]
        kt_ref[...] = k_ref[seq - n_rows:seq, :].T
        vt_ref[...] = v_ref[seq - n_rows:seq, :].T


def _pair_softmax_pv(q_pair, k_halves, v_halves, bias_halves, lower_half, sinks=None):
    outs = []
    for half in range(2):
        q_h = jnp.where(lower_half, q_pair, 0) if half == 0 else jnp.where(lower_half, 0, q_pair)
        s = _dot_nt(q_h, k_halves[half]) + bias_halves[half]
        m = jnp.max(s, axis=-1, keepdims=True)
        if sinks is not None:
            m = jnp.maximum(m, sinks[half])
        r = _dot(jnp.exp(s - m).astype(BF16), v_halves[half])
        denom = r[:, LANES:]
        if sinks is not None:
            denom = denom + jnp.exp(sinks[half] - m)
        outs.append(r[:, :LANES] / denom)
    return jnp.where(lower_half, outs[0], outs[1])


def _swa_prompt_kernel(sink_ref, q_ref, k_ref, v_ref, o_ref, kt_ref, vt_ref, k_scr, v_scr):
    i = pl.program_id(1)

    @pl.when(i == 0)
    def _():
        k = k_ref[...]
        v = v_ref[...]
        ones = jnp.ones((k.shape[0], LANES), BF16)
        for x, (kx, vx) in enumerate(((k, v), (pltpu.roll(k, HEAD_DIM, axis=1),
                                               pltpu.roll(v, HEAD_DIM, axis=1)))):
            k_scr[x] = kx.astype(BF16)
            v_scr[x, :, 0:LANES] = vx.astype(BF16)
            v_scr[x, :, LANES:] = ones

    row = lax.broadcasted_iota(jnp.int32, (QBLK, SWA_WIN), 0)
    col = lax.broadcasted_iota(jnp.int32, (QBLK, SWA_WIN), 1)
    n_prev = SWA_ROWS // CHUNK
    lower_half = lax.broadcasted_iota(jnp.int32, (QBLK, LANES), 1) < HEAD_DIM
    for u in range(QSTEP):
        blk = i * QSTEP + u
        rows = slice(u * QBLK, (u + 1) * QBLK)
        start = pl.multiple_of(jnp.maximum(blk * QBLK - SWA_ROWS, 0), QBLK)
        win = pl.ds(start, SWA_WIN)
        q = (q_ref[rows, :] * (HEAD_DIM ** -0.5)).astype(BF16)
        behind = (blk * (QBLK // CHUNK) + row // CHUNK) - (start // CHUNK + col // CHUNK)
        mask = jnp.where(jnp.abs(2 * behind - n_prev) <= n_prev, 0.0, NEG_INF)
        for c in range(SWA_QW // LANES):
            cols = slice(c * LANES, (c + 1) * LANES)
            heads = (2 * c, 2 * c + 1)
            swap = [(h // G_SWA) ^ half for half, h in enumerate(heads)]
            out = _pair_softmax_pv(q[:, cols], [k_scr[x, win, :] for x in swap],
                                   [v_scr[x, win, :] for x in swap], [mask, mask], lower_half,
                                   sinks=[sink_ref[h] for h in heads])
            o_ref[rows, cols] = out.astype(o_ref.dtype)

    _emit_cache_tail(i, k_ref, v_ref, kt_ref, vt_ref)


def _swa_prompt(swa, sinks, batch, seq):
    tq = QSTEP * QBLK
    nq = seq // tq
    k_col = SWA_QW // SWA_KW
    tail = pl.BlockSpec((None, SWA_KW, SWA_ROWS), lambda b, i: (b, 0, 0))
    tail_shape = jax.ShapeDtypeStruct((batch, SWA_KW, SWA_ROWS), F32)
    return pl.pallas_call(
        _swa_prompt_kernel,
        grid=(batch, nq),
        in_specs=[
            pl.BlockSpec(memory_space=pltpu.SMEM),
            pl.BlockSpec((tq, SWA_QW), lambda b, i: (b * nq + i, 0)),
            pl.BlockSpec((seq, SWA_KW), lambda b, i: (b, k_col)),
            pl.BlockSpec((seq, SWA_KW), lambda b, i: (b, k_col + 1)),
        ],
        out_specs=[pl.BlockSpec((tq, SWA_QW), lambda b, i: (b * nq + i, 0)), tail, tail],
        out_shape=[jax.ShapeDtypeStruct((batch * seq, SWA_QW), BF16), tail_shape, tail_shape],
        scratch_shapes=[
            pltpu.VMEM((2, seq, SWA_KW), BF16),
            pltpu.VMEM((2, seq, 2 * LANES), BF16),
        ],
        compiler_params=_params(("parallel", "arbitrary")),
        name="swa_prompt",
    )(sinks, swa, swa, swa)


def _bias_kernel(t_ref, o_ref):
    n_pad = t_ref.shape[-1]
    r = lax.broadcasted_iota(jnp.int32, (n_pad, BIAS_ROW), 0)
    w = lax.broadcasted_iota(jnp.int32, (n_pad, BIAS_ROW), 1)
    idx = jnp.clip(BIAS_ROW // 2 - 1 - w, -REL_CLIP, REL_CLIP) + REL_CLIP
    onehot = jnp.where(r == idx, 1.0, 0.0).astype(BF16)
    t = t_ref[...]
    t_hi = t.astype(BF16)
    rem = t - t_hi.astype(F32)
    t_mid = rem.astype(BF16)
    t_lo = (rem - t_mid.astype(F32)).astype(BF16)
    f = (_dot(t_hi, onehot) + _dot(t_mid, onehot)) + _dot(t_lo, onehot)
    ext = CA_EXT_TILES * LANES
    row = lax.broadcasted_iota(jnp.int32, (QBLK, ext), 0)
    colj = lax.broadcasted_iota(jnp.int32, (QBLK, ext), 1)
    ahead = colj // CHUNK - row // CHUNK
    n_prev = CA_ROWS // CHUNK
    valid = jnp.abs(2 * ahead - n_prev) <= n_prev
    for h in range(H_CA):
        x = jnp.broadcast_to(f[h:h + 1, :], (QBLK, BIAS_ROW))
        y = pltpu.roll(x, BIAS_ROW - (QBLK - 1), axis=1, stride=1, stride_axis=0)
        y = jnp.where(valid, y[:, 0:ext], NEG_INF)
        for c in range(CA_EXT_TILES):
            o_ref[h, c] = y[:, c * LANES:(c + 1) * LANES]


def _build_bias(rel_bias_table):
    depth = rel_bias_table.shape[0]
    n_pad = 3 * LANES
    t = jnp.transpose(rel_bias_table, (0, 2, 1))
    t = jnp.pad(t, ((0, 0), (0, SUBLANES - H_CA), (0, n_pad - N_REL)))
    return pl.pallas_call(
        _bias_kernel,
        grid=(depth,),
        in_specs=[pl.BlockSpec((None, SUBLANES, n_pad), lambda l: (l, 0, 0))],
        out_specs=pl.BlockSpec((None, H_CA, CA_EXT_TILES, QBLK, LANES), lambda l: (l, 0, 0, 0, 0)),
        out_shape=jax.ShapeDtypeStruct((depth, H_CA, CA_EXT_TILES, QBLK, LANES), F32),
        compiler_params=_params(("parallel",)),
        name="ca_bias",
    )(t)


def _ca_prompt_kernel(q_ref, k_ref, v_ref, bias_ref, o_ref, kt_ref, vt_ref, k_scr, v_scr):
    i = pl.program_id(1)
    n_pairs = CA_W // LANES

    @pl.when(i == 0)
    def _():
        k_scr[...] = k_ref[...].astype(BF16)
        for p in range(n_pairs):
            v_scr[p, :, 0:LANES] = v_ref[:, p * LANES:(p + 1) * LANES].astype(BF16)
            v_scr[p, :, LANES:] = jnp.ones((v_scr.shape[1], LANES), BF16)

    lower_half = lax.broadcasted_iota(jnp.int32, (QBLK, LANES), 1) < HEAD_DIM
    for u in range(QSTEP):
        rows = slice(u * QBLK, (u + 1) * QBLK)
        ideal = (i * QSTEP + u) * QBLK - CA_ROWS
        start = pl.multiple_of(jnp.maximum(ideal, 0), QBLK)
        off = (start - ideal) // LANES
        q = (q_ref[rows, :] * (HEAD_DIM ** -0.5)).astype(BF16)
        for p in range(n_pairs):
            cols = slice(p * LANES, (p + 1) * LANES)
            bias = [jnp.concatenate([bias_ref[2 * p + half, off + c] for c in range(CA_WIN_TILES)],
                                    axis=-1) for half in range(2)]
            k_win = k_scr[pl.ds(start, CA_WIN), cols]
            v_win = v_scr[p, pl.ds(start, CA_WIN), :]
            out = _pair_softmax_pv(q[:, cols], [k_win, k_win], [v_win, v_win], bias, lower_half)
            o_ref[rows, cols] = out.astype(o_ref.dtype)

    _emit_cache_tail(i, k_ref, v_ref, kt_ref, vt_ref)


def _ca_prompt(ca, bias, layer, batch, seq):
    tq = QSTEP * QBLK
    nq = seq // tq
    tail = pl.BlockSpec((None, CA_W, CA_ROWS), lambda b, i: (b, 0, 0))
    tail_shape = jax.ShapeDtypeStruct((batch, CA_W, CA_ROWS), F32)
    return pl.pallas_call(
        _ca_prompt_kernel,
        grid=(batch, nq),
        in_specs=[
            pl.BlockSpec((tq, CA_W), lambda b, i: (b * nq + i, 0)),
            pl.BlockSpec((seq, CA_W), lambda b, i: (b, 1)),
            pl.BlockSpec((seq, CA_W), lambda b, i: (b, 2)),
            pl.BlockSpec((None, H_CA, CA_EXT_TILES, QBLK, LANES), lambda b, i: (layer, 0, 0, 0, 0)),
        ],
        out_specs=[pl.BlockSpec((tq, CA_W), lambda b, i: (b * nq + i, 0)), tail, tail],
        out_shape=[jax.ShapeDtypeStruct((batch * seq, CA_W), BF16), tail_shape, tail_shape],
        scratch_shapes=[
            pltpu.VMEM((seq, CA_W), BF16),
            pltpu.VMEM((CA_W // LANES, seq, 2 * LANES), BF16),
        ],
        compiler_params=_params(("parallel", "arbitrary")),
        name="ca_prompt",
    )(ca, ca, ca, bias)


def _block_diag(blocks):
    z = jnp.zeros_like(blocks[0])
    return jnp.concatenate([jnp.concatenate([b if j == i else z for j in range(len(blocks))], axis=2)
                            for i, b in enumerate(blocks)], axis=1)


def _swap_halves_rows(x):
    half = x.shape[1] // 2
    return jnp.concatenate([x[:, half:], x[:, :half]], axis=1)


def _pair_softmax_pv_cached(q_pair, kt_halves, kn_halves, vt_halves, vn_halves, bias_c, bias_n,
                            lower_half, sinks=None):
    outs = []
    for half in range(2):
        q_h = jnp.where(lower_half, q_pair, 0) if half == 0 else jnp.where(lower_half, 0, q_pair)
        s_c = _bdot('bqd,bdk->bqk', q_h, kt_halves[half]) + bias_c[half]
        s_n = _bdot('bqd,bkd->bqk', q_h, kn_halves[half]) + bias_n[half]
        m = jnp.maximum(jnp.max(s_c, axis=-1, keepdims=True), jnp.max(s_n, axis=-1, keepdims=True))
        if sinks is not None:
            m = jnp.maximum(m, sinks[half])
        r = (_bdot('bqk,bfk->bqf', jnp.exp(s_c - m).astype(BF16), vt_halves[half])
             + _bdot('bqk,bkf->bqf', jnp.exp(s_n - m).astype(BF16), vn_halves[half]))
        denom = r[:, :, LANES:]
        if sinks is not None:
            denom = denom + jnp.exp(sinks[half] - m)
        outs.append(r[:, :, :LANES] / denom)
    return jnp.where(lower_half, outs[0], outs[1])


def _mixer_sample_kernel(sink_ref, ret_ref, swa_ref, ca_ref, cos_ref, sin_ref, dec_ref, qd_ref,
                         kd_ref, avg_ref, same_ref, sdec_ref, skt_ref, svt_ref, ckt_ref, cvt_ref,
                         st_ref, bias_ref, oa_ref, ob_ref, oc_ref, snew_ref):
    n_blk, t_len, _ = ret_ref.shape
    lower_half = lax.broadcasted_iota(jnp.int32, (t_len, LANES), 1) < HEAD_DIM
    ones_rows = lambda n: jnp.ones((n_blk, LANES, n), BF16)
    ones_cols = jnp.ones((n_blk, t_len, LANES), BF16)
    ca_tiles = CA_ROWS // LANES

    col = lambda c: ret_ref[:, :, c * RET_W:(c + 1) * RET_W]
    pairs = [_block_diag([st_ref[:, 2 * p], st_ref[:, 2 * p + 1]]) for p in range(RET_W // LANES)]
    out, new_state = _retention_block(col(0), col(1), col(2), col(3), cos_ref[...], sin_ref[...],
                                      dec_ref[...], qd_ref[...], kd_ref[...], avg_ref[...],
                                      same_ref[...], sdec_ref[...], pairs)
    oa_ref[...] = out.astype(oa_ref.dtype)
    for p in range(RET_W // LANES):
        snew_ref[:, 2 * p] = new_state[p][:, 0:HEAD_DIM, 0:HEAD_DIM]
        snew_ref[:, 2 * p + 1] = new_state[p][:, HEAD_DIM:, HEAD_DIM:]

    q = (swa_ref[:, :, 0:SWA_QW] * (HEAD_DIM ** -0.5)).astype(BF16)
    k_new = swa_ref[:, :, SWA_QW:SWA_QW + SWA_KW]
    v_new = swa_ref[:, :, SWA_QW + SWA_KW:SWA_COLS]
    k_cache = skt_ref[...]
    v_cache = svt_ref[...]
    swap_lanes = lambda x: pltpu.roll(x.reshape(n_blk * t_len, SWA_KW), HEAD_DIM, axis=1).reshape(x.shape)
    kt = [k_cache.astype(BF16), _swap_halves_rows(k_cache).astype(BF16)]
    vt = [jnp.concatenate([x.astype(BF16), ones_rows(SWA_ROWS)], axis=1)
          for x in (v_cache, _swap_halves_rows(v_cache))]
    kn = [k_new.astype(BF16), swap_lanes(k_new).astype(BF16)]
    vn = [jnp.concatenate([x.astype(BF16), ones_cols], axis=2) for x in (v_new, swap_lanes(v_new))]
    for c in range(SWA_QW // LANES):
        cols = slice(c * LANES, (c + 1) * LANES)
        heads = (2 * c, 2 * c + 1)
        swap = [(h // G_SWA) ^ half for half, h in enumerate(heads)]
        out = _pair_softmax_pv_cached(q[:, :, cols], [kt[x] for x in swap], [kn[x] for x in swap],
                                      [vt[x] for x in swap], [vn[x] for x in swap],
                                      [0.0, 0.0], [0.0, 0.0], lower_half,
                                      sinks=[sink_ref[h] for h in heads])
        ob_ref[:, :, cols] = out.astype(ob_ref.dtype)

    q = (ca_ref[:, :, 0:CA_W] * (HEAD_DIM ** -0.5)).astype(BF16)
    k_new = ca_ref[:, :, CA_W:2 * CA_W]
    v_new = ca_ref[:, :, 2 * CA_W:CA_COLS]
    for p in range(CA_W // LANES):
        cols = slice(p * LANES, (p + 1) * LANES)
        kt_p = ckt_ref[:, cols, :].astype(BF16)
        vt_p = jnp.concatenate([cvt_ref[:, cols, :].astype(BF16), ones_rows(CA_ROWS)], axis=1)
        kn_p = k_new[:, :, cols].astype(BF16)
        vn_p = jnp.concatenate([v_new[:, :, cols].astype(BF16), ones_cols], axis=2)
        bias_c = [jnp.concatenate([bias_ref[2 * p + half, c, 0:t_len, :] for c in range(ca_tiles)],
                                  axis=-1) for half in range(2)]
        bias_n = [bias_ref[2 * p + half, ca_tiles, 0:t_len, 0:t_len] for half in range(2)]
        out = _pair_softmax_pv_cached(q[:, :, cols], [kt_p, kt_p], [kn_p, kn_p], [vt_p, vt_p],
                                      [vn_p, vn_p], bias_c, bias_n, lower_half)
        oc_ref[:, :, cols] = out.astype(oc_ref.dtype)


def _mixer_sample(ret, swa, ca, sinks, tables, rope, cache_skt, cache_svt, cache_ckt, cache_cvt,
                  state_ret, bias, layer, n_seq, t_len, n_blk):
    cos, sin = rope
    whole = lambda shape: pl.BlockSpec(shape, lambda b: (0,) * len(shape))
    seq_rows = lambda width: pl.BlockSpec((n_blk, t_len, width), lambda b: (b, 0, 0))
    cache = lambda feat, rows: pl.BlockSpec((None, n_blk, feat, rows), lambda b: (layer, b, 0, 0))
    per_stream = lambda a: a.reshape(n_seq, t_len, a.shape[-1])
    oa, ob, oc, s_new = pl.pallas_call(
        _mixer_sample_kernel,
        grid=(n_seq // n_blk,),
        in_specs=[
            pl.BlockSpec(memory_space=pltpu.SMEM),
            seq_rows(RET_COLS), seq_rows(SWA_COLS), seq_rows(CA_COLS),
            whole((t_len, RET_W)), whole((t_len, RET_W)),
            *[whole(a.shape) for a in tables],
            cache(SWA_KW, SWA_ROWS), cache(SWA_KW, SWA_ROWS),
            cache(CA_W, CA_ROWS), cache(CA_W, CA_ROWS),
            pl.BlockSpec((None, n_blk, H_RET, HEAD_DIM, HEAD_DIM), lambda b: (layer, b, 0, 0, 0)),
            pl.BlockSpec((None, H_CA, CA_EXT_TILES, QBLK, LANES), lambda b: (layer, 0, 0, 0, 0)),
        ],
        out_specs=[
            seq_rows(RET_W), seq_rows(SWA_QW), seq_rows(CA_W),
            pl.BlockSpec((n_blk, H_RET, HEAD_DIM, HEAD_DIM), lambda b: (b, 0, 0, 0)),
        ],
        out_shape=[
            jax.ShapeDtypeStruct((n_seq, t_len, RET_W), BF16),
            jax.ShapeDtypeStruct((n_seq, t_len, SWA_QW), BF16),
            jax.ShapeDtypeStruct((n_seq, t_len, CA_W), BF16),
            jax.ShapeDtypeStruct((n_seq, H_RET, HEAD_DIM, HEAD_DIM), F32),
        ],
        compiler_params=_params(("parallel",)),
        name="mixer_sample",
    )(sinks, per_stream(ret), per_stream(swa), per_stream(ca), cos, sin, *tables, cache_skt,
      cache_svt, cache_ckt, cache_cvt, state_ret, bias)
    flat = lambda a: a.reshape(n_seq * t_len, a.shape[-1])
    return flat(oa), flat(ob), flat(oc), s_new


def _append_rows_feature_major(cache_t, new_rows):
    n_feat, n_rows = cache_t.shape
    t_len = new_rows.shape[0]
    rolled = pltpu.roll(cache_t, n_rows - t_len, axis=1)
    t_idx = lax.broadcasted_iota(jnp.int32, (t_len, LANES), 0)
    x_idx = lax.broadcasted_iota(jnp.int32, (t_len, LANES), 1)
    place = jnp.where(x_idx == LANES - t_len + t_idx, 1.0, 0.0).astype(BF16)
    hi = new_rows.astype(BF16)
    rem = new_rows - hi.astype(F32)
    mid = rem.astype(BF16)
    lo = (rem - mid.astype(F32)).astype(BF16)
    placed = (_dot_tn(hi, place) + _dot_tn(mid, place)) + _dot_tn(lo, place)
    lane = lax.broadcasted_iota(jnp.int32, (n_feat, LANES), 1)
    last = jnp.where(lane < LANES - t_len, rolled[:, n_rows - LANES:], placed)
    if n_rows == LANES:
        return last
    return jnp.concatenate([rolled[:, :n_rows - LANES], last], axis=1)


def _cache_roll_kernel(k_ref, v_ref, nk_ref, nv_ref, ok_ref, ov_ref):
    for n in range(k_ref.shape[0]):
        ok_ref[n] = _append_rows_feature_major(k_ref[n], nk_ref[n])
        ov_ref[n] = _append_rows_feature_major(v_ref[n], nv_ref[n])


def _cache_roll(cache_kt, cache_vt, new_k, new_v, n_blk):
    depth, n_seq, n_feat, n_rows = cache_kt.shape
    t_len = new_k.shape[2]
    cache = pl.BlockSpec((None, n_blk, n_feat, n_rows), lambda l, b: (l, b, 0, 0))
    new = pl.BlockSpec((None, n_blk, t_len, n_feat), lambda l, b: (l, b, 0, 0))
    shape = jax.ShapeDtypeStruct(cache_kt.shape, F32)
    return pl.pallas_call(
        _cache_roll_kernel,
        grid=(depth, n_seq // n_blk),
        in_specs=[cache, cache, new, new],
        out_specs=[cache, cache],
        out_shape=[shape, shape],
        compiler_params=_params(("parallel", "parallel")),
        name="cache_roll",
    )(cache_kt, cache_vt, new_k, new_v)


def _ffn_kernel(*refs, tm, tf, pad, shift, tiles_per_seq, final_norm):
    if final_norm:
        (x_ref, a_ref, b_ref, c_ref, wo_ref, gain_ref, wg_ref, wu_ref, wd_ref, cw_ref, cb_ref,
         prev_ref, gf_ref, o_ref, st_ref, hn_scr, act_scr, hist_scr) = refs
    else:
        (x_ref, a_ref, b_ref, c_ref, wo_ref, gain_ref, wg_ref, wu_ref, wd_ref, cw_ref, cb_ref,
         prev_ref, o_ref, st_ref, hn_scr, act_scr, hist_scr) = refs
    x = x_ref[...] + _dot(a_ref[...], wo_ref[0:RET_W, :])
    x = x + _dot(b_ref[...], wo_ref[RET_W:RET_W + SWA_QW, :])
    x = x + _dot(c_ref[...], wo_ref[RET_W + SWA_QW:D_MODEL, :])
    o_ref[...] = x
    hn_scr[...] = _rms(x, gain_ref[...]).astype(BF16)
    if tiles_per_seq == 1:
        hist = prev_ref
    else:
        hist = hist_scr

        @pl.when((pl.program_id(0) % tiles_per_seq) == 0)
        def _():
            hist_scr[...] = prev_ref[...]

    for c in range(D_FF // tf):
        cols = slice(c * tf, (c + 1) * tf)
        hn = hn_scr[...]
        g = _dot(hn, wg_ref[:, cols])
        u = _dot(hn, wu_ref[:, cols])
        g_ext = jnp.concatenate([hist[:, cols], g], axis=0)
        gc = cb_ref[:, cols] + cw_ref[0:1, cols] * pltpu.roll(g_ext, 2 * shift, axis=0)[pad:]
        gc = gc + cw_ref[1:2, cols] * pltpu.roll(g_ext, shift, axis=0)[pad:]
        gc = gc + cw_ref[2:3, cols] * g
        act_scr[:, cols] = (jax.nn.gelu(gc) * u).astype(BF16)
        tail = g_ext[tm:tm + pad]
        if tiles_per_seq > 1:
            hist_scr[:, cols] = tail
        st_ref[:, cols] = tail

    y = o_ref[...] + _dot(act_scr[...], wd_ref[...])
    if final_norm:
        y = _rms(y, gf_ref[...])
    o_ref[...] = y


def _outproj_ffn(x, oa, ob, oc, w_out, gain, w_gate, w_up, w_down, conv_w, conv_b, prev, layer, tm,
                 tf, shift, tiles_per_seq, final_gain=None):
    m = x.shape[0]
    n_groups, pad, _ = prev.shape
    final_norm = final_gain is not None
    resident = lambda shape: pl.BlockSpec((None,) + shape, lambda i: (layer, 0, 0),
                                          pipeline_mode=pl.Buffered(1))
    state = pl.BlockSpec((None, pad, D_FF), lambda i: (i // tiles_per_seq, 0, 0))
    rows = lambda width: pl.BlockSpec((tm, width), lambda i: (i, 0))
    in_specs = [
        rows(D_MODEL), rows(RET_W), rows(SWA_QW), rows(CA_W),
        resident((D_MODEL, D_MODEL)),
        resident((1, D_MODEL)),
        resident((D_MODEL, D_FF)), resident((D_MODEL, D_FF)), resident((D_FF, D_MODEL)),
        resident((CONV_W, D_FF)), resident((1, D_FF)),
        state,
    ]
    args = [x, oa, ob, oc, w_out, gain, w_gate, w_up, w_down, conv_w, conv_b, prev]
    if final_norm:
        in_specs.append(pl.BlockSpec((1, D_MODEL), lambda i: (0, 0)))
        args.append(final_gain)
    return pl.pallas_call(
        functools.partial(_ffn_kernel, tm=tm, tf=tf, pad=pad, shift=shift,
                          tiles_per_seq=tiles_per_seq, final_norm=final_norm),
        grid=(m // tm,),
        in_specs=in_specs,
        out_specs=[rows(D_MODEL), state],
        out_shape=[
            jax.ShapeDtypeStruct((m, D_MODEL), F32),
            jax.ShapeDtypeStruct((n_groups, pad, D_FF), F32),
        ],
        scratch_shapes=[
            pltpu.VMEM((tm, D_MODEL), BF16),
            pltpu.VMEM((tm, D_FF), BF16),
            pltpu.VMEM((pad, D_FF), F32),
        ],
        compiler_params=_params(("arbitrary",)),
        name="outproj_ffn",
    )(*args)


def kernel(x_prompt, x_sample, cache_swa_k, cache_swa_v, cache_ca_k, cache_ca_v, state_ret,
           state_ffn_conv, w_in, w_out, attn_sinks, rel_bias_table, norm_mix, norm_ffn, w_gate,
           w_up, conv_w, conv_b, w_down, norm_final):
    batch, seq, _ = x_prompt.shape
    n_seq, t_len, _ = x_sample.shape
    depth = w_in.shape[0]
    assert seq % (QSTEP * QBLK) == 0 and seq >= CA_WIN and t_len % SUBLANES == 0
    assert cache_swa_k.shape[2] == SWA_ROWS and cache_ca_k.shape[2] == CA_ROWS

    w_in_b = w_in.astype(BF16)
    w_out_b = w_out.astype(BF16)
    w_gate_b = w_gate.astype(BF16)
    w_up_b = w_up.astype(BF16)
    w_down_b = w_down.astype(BF16)
    g_mix = norm_mix.reshape(depth, 1, D_MODEL)
    g_ffn = norm_ffn.reshape(depth, 1, D_MODEL)
    g_fin = norm_final.reshape(1, D_MODEL)
    cb = conv_b.reshape(depth, 1, D_FF)

    rope_p = _rope_tables(jnp.arange(seq))
    rope_s = _rope_tables(PAST_LEN + jnp.arange(t_len))
    tab_p = _retention_tables(QBLK)
    tab_s = _retention_tables(t_len)
    bias = _build_bias(rel_bias_table)

    feature_major = lambda c: jnp.transpose(c, (0, 1, 3, 4, 2)).reshape(depth, n_seq, -1, c.shape[2])
    cskt, csvt, cckt, ccvt = map(feature_major, (cache_swa_k, cache_swa_v, cache_ca_k, cache_ca_v))

    hp = x_prompt.reshape(batch * seq, D_MODEL)
    m_s = n_seq * t_len
    hs = x_sample.reshape(m_s, D_MODEL)

    pad_p = SUBLANES
    pad_s = (CONV_W - 1) * n_seq
    prev_p = jnp.zeros((batch, pad_p, D_FF), F32)
    tm_p = 512
    tiles_per_seq = seq // tm_p

    outs_p = [[] for _ in range(6)]
    outs_s = [[] for _ in range(6)]
    for l in range(depth):
        last = l == depth - 1
        ret, swa, ca = _inproj(hp, g_mix, w_in_b, l, tm=512)
        oa, s_ret = _ret_prompt(ret, batch, seq, tab_p, rope_p, n_blk=4)
        ob, *swa_tail = _swa_prompt(swa, attn_sinks[l], batch, seq)
        oc, *ca_tail = _ca_prompt(ca, bias, l, batch, seq)
        hp, conv_st = _outproj_ffn(hp, oa, ob, oc, w_out_b, g_ffn, w_gate_b, w_up_b, w_down_b, conv_w,
                                   cb, prev_p, l, tm=tm_p, tf=256, shift=1,
                                   tiles_per_seq=tiles_per_seq, final_gain=g_fin if last else None)
        for lst, val in zip(outs_p[:4], swa_tail + ca_tail):
            lst.append(val)
        outs_p[4].append(s_ret)
        outs_p[5].append(conv_st[:, pad_p - (CONV_W - 1):, :])

        ret, swa, ca = _inproj(hs, g_mix, w_in_b, l, tm=m_s // 2)
        oa, ob, oc, s_new = _mixer_sample(ret, swa, ca, attn_sinks[l], tab_s, rope_s, cskt, csvt,
                                          cckt, ccvt, state_ret, bias, l, n_seq, t_len, n_blk=8)
        time_major = lambda a: jnp.transpose(a.reshape(n_seq, t_len, -1), (1, 0, 2)).reshape(m_s, -1)
        prev_s = jnp.transpose(state_ffn_conv[l], (1, 0, 2)).reshape(1, pad_s, D_FF)
        hs_tm, conv_st = _outproj_ffn(time_major(hs), time_major(oa), time_major(ob), time_major(oc),
                                      w_out_b, g_ffn, w_gate_b, w_up_b, w_down_b, conv_w, cb, prev_s,
                                      l, tm=m_s, tf=256, shift=n_seq, tiles_per_seq=1,
                                      final_gain=g_fin if last else None)
        hs = jnp.transpose(hs_tm.reshape(t_len, n_seq, D_MODEL), (1, 0, 2)).reshape(m_s, D_MODEL)
        swa3 = swa.reshape(n_seq, t_len, SWA_COLS)
        ca3 = ca.reshape(n_seq, t_len, CA_COLS)
        outs_s[0].append(swa3[:, :, SWA_QW:SWA_QW + SWA_KW])
        outs_s[1].append(swa3[:, :, SWA_QW + SWA_KW:])
        outs_s[2].append(ca3[:, :, CA_W:2 * CA_W])
        outs_s[3].append(ca3[:, :, 2 * CA_W:])
        outs_s[4].append(s_new)
        outs_s[5].append(jnp.transpose(conv_st.reshape(CONV_W - 1, n_seq, D_FF), (1, 0, 2)))

    y_prompt = hp.reshape(batch, seq, D_MODEL)
    y_sample = hs.reshape(n_seq, t_len, D_MODEL)
    stack = lambda v: jnp.stack(v, axis=0)
    row_major = lambda c, heads: jnp.transpose(
        c.reshape(depth, c.shape[1], heads, HEAD_DIM, -1), (0, 1, 4, 2, 3))
    swa_k_p = row_major(stack(outs_p[0]), KV_SWA)
    swa_v_p = row_major(stack(outs_p[1]), KV_SWA)
    ca_k_p = row_major(stack(outs_p[2]), H_CA)
    ca_v_p = row_major(stack(outs_p[3]), H_CA)
    ret_p = stack(outs_p[4])
    conv_p = stack(outs_p[5])

    swa_kt, swa_vt = _cache_roll(cskt, csvt, stack(outs_s[0]), stack(outs_s[1]), n_blk=8)
    ca_kt, ca_vt = _cache_roll(cckt, ccvt, stack(outs_s[2]), stack(outs_s[3]), n_blk=4)
    swa_k_s = row_major(swa_kt, KV_SWA)
    swa_v_s = row_major(swa_vt, KV_SWA)
    ca_k_s = row_major(ca_kt, H_CA)
    ca_v_s = row_major(ca_vt, H_CA)
    ret_s = stack(outs_s[4])
    conv_s = stack(outs_s[5])
    return (y_prompt, y_sample, swa_k_p, swa_v_p, ca_k_p, ca_v_p, ret_p, conv_p,
            swa_k_s, swa_v_s, ca_k_s, ca_v_s, ret_s, conv_s)
```

```python
import functools

import jax
import jax.numpy as jnp
from jax import lax
from jax.experimental import pallas as pl
from jax.experimental.pallas import tpu as pltpu

F32 = jnp.float32
BF16 = jnp.bfloat16

D_MODEL = 1024
CHUNK = 64
HEAD_DIM = 64
H_RET = 6
H_SWA = 6
KV_SWA = 2
G_SWA = H_SWA // KV_SWA
H_CA = 4
SWA_ROWS = 128
CA_ROWS = 512
REL_CLIP = 128
N_REL = 2 * REL_CLIP + 1
D_FF = 2816
CONV_W = 3
EPS = 1e-6
ROPE_BASE = 10000.0
NEG_INF = -1e30
PAST_LEN = 4096

RET_W = H_RET * HEAD_DIM
RET_COLS = 4 * RET_W
SWA_QW = H_SWA * HEAD_DIM
SWA_KW = KV_SWA * HEAD_DIM
SWA_COLS = SWA_QW + 2 * SWA_KW
CA_W = H_CA * HEAD_DIM
CA_COLS = 3 * CA_W
IN_COLS = RET_COLS + SWA_COLS + CA_COLS

LANES = 128
SUBLANES = 8
VMEM_LIMIT_BYTES = 56 * 1024 * 1024

QBLK = 128
QSTEP = 8
SWA_WIN = QBLK + SWA_ROWS
CA_WIN = QBLK + CA_ROWS
CA_WIN_TILES = CA_WIN // LANES
CA_EXT_TILES = (CA_WIN + CA_ROWS) // LANES
BIAS_ROW = CA_EXT_TILES * LANES + LANES


def _params(semantics):
    return pltpu.CompilerParams(dimension_semantics=semantics,
                                vmem_limit_bytes=VMEM_LIMIT_BYTES)


def _rms(x, gain):
    ms = jnp.mean(x * x, axis=-1, keepdims=True)
    return (x * lax.rsqrt(ms + EPS)) * gain


def _dot(a, b):
    return jnp.dot(a, b, preferred_element_type=F32)


def _dot_nt(a, b):
    return lax.dot_general(a, b, (((1,), (1,)), ((), ())), preferred_element_type=F32)


def _dot_tn(a, b):
    return lax.dot_general(a, b, (((0,), (0,)), ((), ())), preferred_element_type=F32)


def _inproj_kernel(x_ref, g_ref, w_ref, ret_ref, swa_ref, ca_ref):
    h = _rms(x_ref[...], g_ref[...]).astype(BF16)
    ret_ref[...] = _dot(h, w_ref[:, 0:RET_COLS])
    swa_ref[...] = _dot(h, w_ref[:, RET_COLS:RET_COLS + SWA_COLS])
    ca_ref[...] = _dot(h, w_ref[:, RET_COLS + SWA_COLS:IN_COLS])


def _inproj(x, gain, w_in, layer, tm):
    m = x.shape[0]
    row = lambda i: (i, 0)
    return pl.pallas_call(
        _inproj_kernel,
        grid=(m // tm,),
        in_specs=[
            pl.BlockSpec((tm, D_MODEL), row),
            pl.BlockSpec((None, 1, D_MODEL), lambda i: (layer, 0, 0)),
            pl.BlockSpec((None, D_MODEL, IN_COLS), lambda i: (layer, 0, 0)),
        ],
        out_specs=[
            pl.BlockSpec((tm, RET_COLS), row),
            pl.BlockSpec((tm, SWA_COLS), row),
            pl.BlockSpec((tm, CA_COLS), row),
        ],
        out_shape=[
            jax.ShapeDtypeStruct((m, RET_COLS), F32),
            jax.ShapeDtypeStruct((m, SWA_COLS), F32),
            jax.ShapeDtypeStruct((m, CA_COLS), F32),
        ],
        compiler_params=_params(("parallel",)),
        name="inproj",
    )(x, gain, w_in)


def _ret_log_decay():
    return jnp.log(1.0 - 2.0 ** (-5.0 - jnp.arange(H_RET, dtype=F32)))


def _rope_tables(pos):
    half = HEAD_DIM // 2
    inv = 1.0 / (ROPE_BASE ** (jnp.arange(half, dtype=F32) / half))
    ang = pos.astype(F32)[:, None] * inv[None, :]
    cos = jnp.cos(ang)
    sin = jnp.sin(ang)
    cos_h = jnp.concatenate([cos, cos], axis=-1)
    sin_h = jnp.concatenate([-sin, sin], axis=-1)
    return jnp.tile(cos_h, (1, H_RET)), jnp.tile(sin_h, (1, H_RET))


def _retention_tables(t_len):
    log_g = _ret_log_decay()
    t = jnp.arange(t_len, dtype=F32)
    diff = t[:, None] - t[None, :]
    decay = jnp.where(diff >= 0, jnp.exp(jnp.maximum(diff, 0.0)[None] * log_g[:, None, None]), 0.0)
    q_decay = jnp.exp((t + 1.0)[:, None] * log_g[None, :])
    k_decay = jnp.exp((t_len - 1.0 - t)[:, None] * log_g[None, :])
    head = jnp.arange(LANES) // HEAD_DIM
    same_head = (head[:, None] == head[None, :]).astype(F32)
    blk = jnp.exp(t_len * log_g).reshape(RET_W // LANES, LANES // HEAD_DIM)
    state_decay = jnp.broadcast_to(blk[:, head][:, :, None], (RET_W // LANES, LANES, LANES))
    return (decay.reshape(H_RET * t_len, t_len), jnp.repeat(q_decay, HEAD_DIM, axis=1),
            jnp.repeat(k_decay, HEAD_DIM, axis=1), (same_head / HEAD_DIM).astype(BF16), same_head,
            state_decay)


def _rope(x, cos, sin):
    nb, t_len, n = x.shape
    x2 = x.reshape(nb * t_len, n)
    lane = lax.broadcasted_iota(jnp.int32, x2.shape, 1)
    first_half = (lane & (HEAD_DIM - 1)) < (HEAD_DIM // 2)
    partner = jnp.where(first_half,
                        pltpu.roll(x2, n - HEAD_DIM // 2, axis=1),
                        pltpu.roll(x2, HEAD_DIM // 2, axis=1))
    return x * cos + partner.reshape(x.shape) * sin


def _group_mean(x, head_avg):
    nb, t_len, n = x.shape
    hi = x.astype(BF16)
    lo = (x - hi.astype(F32)).astype(BF16)
    both = jnp.concatenate([hi.reshape(nb * t_len, n), lo.reshape(nb * t_len, n)], axis=0)
    r = _dot(both, head_avg)
    return (r[:nb * t_len] + r[nb * t_len:]).reshape(x.shape)


def _bdot(spec, a, b):
    return jnp.einsum(spec, a, b, preferred_element_type=F32)


def _retention_block(q, k, v, gate, cos, sin, dec, q_decay, k_decay, head_avg, same_head,
                     state_decay, state):
    t_len = q.shape[1]
    q = _rope(q, cos, sin)
    k = _rope(k, cos, sin) * (HEAD_DIM ** -0.5)
    kd = k * k_decay
    lower_half = lax.broadcasted_iota(jnp.int32, (t_len, LANES), 1) < HEAD_DIM
    outs = []
    new_state = []
    for p in range(RET_W // LANES):
        cols = slice(p * LANES, (p + 1) * LANES)
        qp = q[:, :, cols]
        kp = k[:, :, cols].astype(BF16)
        vp = v[:, :, cols].astype(BF16)
        q_stack = jnp.concatenate([jnp.where(lower_half, qp, 0.0).astype(BF16),
                                   jnp.where(lower_half, 0.0, qp).astype(BF16)], axis=1)
        scores = _bdot('bqd,bkd->bqk', q_stack, kp) * dec[2 * p * t_len:(2 * p + 2) * t_len]
        inner_all = _bdot('bqk,bkd->bqd', scores.astype(BF16), vp)
        inner = jnp.where(lower_half, inner_all[:, :t_len], inner_all[:, t_len:])
        cross = _bdot('bqd,bde->bqe', qp.astype(BF16), state[p].astype(BF16)) * q_decay[:, cols]
        o = inner + cross
        d = o - _group_mean(o, head_avg)
        var = _group_mean(d * d, head_avg)
        outs.append(d * lax.rsqrt(var + EPS))
        new_state.append(state_decay[p] * state[p]
                         + same_head * _bdot('bkd,bke->bde', kd[:, :, cols].astype(BF16), vp))
    on = jnp.concatenate(outs, axis=-1)
    return on * (gate * jax.nn.sigmoid(gate)), new_state


def _ret_prompt_kernel(q_ref, k_ref, v_ref, g_ref, cos_ref, sin_ref, dec_ref, qd_ref, kd_ref,
                       avg_ref, same_ref, sdec_ref, o_ref, s_ref, state_scr):
    t = pl.program_id(1)
    n_cols = RET_W // LANES

    @pl.when(t == 0)
    def _():
        state_scr[...] = jnp.zeros_like(state_scr)

    out, new_state = _retention_block(q_ref[...], k_ref[...], v_ref[...], g_ref[...], cos_ref[...],
                                      sin_ref[...], dec_ref[...], qd_ref[...], kd_ref[...],
                                      avg_ref[...], same_ref[...], sdec_ref[...],
                                      [state_scr[p] for p in range(n_cols)])
    o_ref[...] = out.astype(o_ref.dtype)
    for p in range(n_cols):
        state_scr[p] = new_state[p]

    @pl.when(t == pl.num_programs(1) - 1)
    def _():
        for p in range(n_cols):
            s_ref[:, 2 * p] = state_scr[p, :, 0:HEAD_DIM, 0:HEAD_DIM]
            s_ref[:, 2 * p + 1] = state_scr[p, :, HEAD_DIM:, HEAD_DIM:]


def _ret_prompt(ret, batch, seq, tables, rope, n_blk):
    tb = QBLK
    cos, sin = rope
    ret3 = ret.reshape(batch, seq, RET_COLS)
    col = lambda c: pl.BlockSpec((n_blk, tb, RET_W), lambda b, t: (b, t, c))
    pos = pl.BlockSpec((tb, RET_W), lambda b, t: (t, 0))
    whole = lambda a: pl.BlockSpec(a.shape, lambda b, t: (0,) * a.ndim)
    out, s_fin = pl.pallas_call(
        _ret_prompt_kernel,
        grid=(batch // n_blk, seq // tb),
        in_specs=[col(0), col(1), col(2), col(3), pos, pos] + [whole(a) for a in tables],
        out_specs=[
            pl.BlockSpec((n_blk, tb, RET_W), lambda b, t: (b, t, 0)),
            pl.BlockSpec((n_blk, H_RET, HEAD_DIM, HEAD_DIM), lambda b, t: (b, 0, 0, 0)),
        ],
        out_shape=[
            jax.ShapeDtypeStruct((batch, seq, RET_W), BF16),
            jax.ShapeDtypeStruct((batch, H_RET, HEAD_DIM, HEAD_DIM), F32),
        ],
        scratch_shapes=[pltpu.VMEM((RET_W // LANES, n_blk, LANES, LANES), F32)],
        compiler_params=_params(("parallel", "arbitrary")),
        name="ret_prompt",
    )(ret3, ret3, ret3, ret3, cos, sin, *tables)
    return out.reshape(batch * seq, RET_W), s_fin


def _emit_cache_tail(i, k_ref, v_ref, kt_ref, vt_ref):
    @pl.when(i == pl.num_programs(1) - 1)
    def _():
        n_rows = kt_ref.shape[-1]
        seq = k_ref.shape[0]
        kt_ref[...] = k_ref[seq - n_rows:seq, :].T
        vt_ref[...] = v_ref[seq - n_rows:seq, :].T


def _pair_softmax_pv(q_pair, k_halves, v_halves, bias_halves, lower_half, sinks=None):
    outs = []
    for half in range(2):
        q_h = jnp.where(lower_half, q_pair, 0) if half == 0 else jnp.where(lower_half, 0, q_pair)
        s = _dot_nt(q_h, k_halves[half]) + bias_halves[half]
        m = jnp.max(s, axis=-1, keepdims=True)
        if sinks is not None:
            m = jnp.maximum(m, sinks[half])
        r = _dot(jnp.exp(s - m).astype(BF16), v_halves[half])
        denom = r[:, LANES:]
        if sinks is not None:
            denom = denom + jnp.exp(sinks[half] - m)
        outs.append(r[:, :LANES] / denom)
    return jnp.where(lower_half, outs[0], outs[1])


def _swa_prompt_kernel(sink_ref, q_ref, k_ref, v_ref, o_ref, kt_ref, vt_ref, k_scr, v_scr):
    i = pl.program_id(1)

    @pl.when(i == 0)
    def _():
        k = k_ref[...]
        v = v_ref[...]
        ones = jnp.ones((k.shape[0], LANES), BF16)
        for x, (kx, vx) in enumerate(((k, v), (pltpu.roll(k, HEAD_DIM, axis=1),
                                               pltpu.roll(v, HEAD_DIM, axis=1)))):
            k_scr[x] = kx.astype(BF16)
            v_scr[x, :, 0:LANES] = vx.astype(BF16)
            v_scr[x, :, LANES:] = ones

    row = lax.broadcasted_iota(jnp.int32, (QBLK, SWA_WIN), 0)
    col = lax.broadcasted_iota(jnp.int32, (QBLK, SWA_WIN), 1)
    n_prev = SWA_ROWS // CHUNK
    lower_half = lax.broadcasted_iota(jnp.int32, (QBLK, LANES), 1) < HEAD_DIM
    for u in range(QSTEP):
        blk = i * QSTEP + u
        rows = slice(u * QBLK, (u + 1) * QBLK)
        start = pl.multiple_of(jnp.maximum(blk * QBLK - SWA_ROWS, 0), QBLK)
        win = pl.ds(start, SWA_WIN)
        q = (q_ref[rows, :] * (HEAD_DIM ** -0.5)).astype(BF16)
        behind = (blk * (QBLK // CHUNK) + row // CHUNK) - (start // CHUNK + col // CHUNK)
        mask = jnp.where(jnp.abs(2 * behind - n_prev) <= n_prev, 0.0, NEG_INF)
        for c in range(SWA_QW // LANES):
            cols = slice(c * LANES, (c + 1) * LANES)
            heads = (2 * c, 2 * c + 1)
            swap = [(h // G_SWA) ^ half for half, h in enumerate(heads)]
            out = _pair_softmax_pv(q[:, cols], [k_scr[x, win, :] for x in swap],
                                   [v_scr[x, win, :] for x in swap], [mask, mask], lower_half,
                                   sinks=[sink_ref[h] for h in heads])
            o_ref[rows, cols] = out.astype(o_ref.dtype)

    _emit_cache_tail(i, k_ref, v_ref, kt_ref, vt_ref)


def _swa_prompt(swa, sinks, batch, seq):
    tq = QSTEP * QBLK
    nq = seq // tq
    k_col = SWA_QW // SWA_KW
    tail = pl.BlockSpec((None, SWA_KW, SWA_ROWS), lambda b, i: (b, 0, 0))
    tail_shape = jax.ShapeDtypeStruct((batch, SWA_KW, SWA_ROWS), F32)
    return pl.pallas_call(
        _swa_prompt_kernel,
        grid=(batch, nq),
        in_specs=[
            pl.BlockSpec(memory_space=pltpu.SMEM),
            pl.BlockSpec((tq, SWA_QW), lambda b, i: (b * nq + i, 0)),
            pl.BlockSpec((seq, SWA_KW), lambda b, i: (b, k_col)),
            pl.BlockSpec((seq, SWA_KW), lambda b, i: (b, k_col + 1)),
        ],
        out_specs=[pl.BlockSpec((tq, SWA_QW), lambda b, i: (b * nq + i, 0)), tail, tail],
        out_shape=[jax.ShapeDtypeStruct((batch * seq, SWA_QW), BF16), tail_shape, tail_shape],
        scratch_shapes=[
            pltpu.VMEM((2, seq, SWA_KW), BF16),
            pltpu.VMEM((2, seq, 2 * LANES), BF16),
        ],
        compiler_params=_params(("parallel", "arbitrary")),
        name="swa_prompt",
    )(sinks, swa, swa, swa)


def _bias_kernel(t_ref, o_ref):
    n_pad = t_ref.shape[-1]
    r = lax.broadcasted_iota(jnp.int32, (n_pad, BIAS_ROW), 0)
    w = lax.broadcasted_iota(jnp.int32, (n_pad, BIAS_ROW), 1)
    idx = jnp.clip(BIAS_ROW // 2 - 1 - w, -REL_CLIP, REL_CLIP) + REL_CLIP
    onehot = jnp.where(r == idx, 1.0, 0.0).astype(BF16)
    t = t_ref[...]
    t_hi = t.astype(BF16)
    rem = t - t_hi.astype(F32)
    t_mid = rem.astype(BF16)
    t_lo = (rem - t_mid.astype(F32)).astype(BF16)
    f = (_dot(t_hi, onehot) + _dot(t_mid, onehot)) + _dot(t_lo, onehot)
    ext = CA_EXT_TILES * LANES
    row = lax.broadcasted_iota(jnp.int32, (QBLK, ext), 0)
    colj = lax.broadcasted_iota(jnp.int32, (QBLK, ext), 1)
    ahead = colj // CHUNK - row // CHUNK
    n_prev = CA_ROWS // CHUNK
    valid = jnp.abs(2 * ahead - n_prev) <= n_prev
    for h in range(H_CA):
        x = jnp.broadcast_to(f[h:h + 1, :], (QBLK, BIAS_ROW))
        y = pltpu.roll(x, BIAS_ROW - (QBLK - 1), axis=1, stride=1, stride_axis=0)
        y = jnp.where(valid, y[:, 0:ext], NEG_INF)
        for c in range(CA_EXT_TILES):
            o_ref[h, c] = y[:, c * LANES:(c + 1) * LANES]


def _build_bias(rel_bias_table):
    depth = rel_bias_table.shape[0]
    n_pad = 3 * LANES
    t = jnp.transpose(rel_bias_table, (0, 2, 1))
    t = jnp.pad(t, ((0, 0), (0, SUBLANES - H_CA), (0, n_pad - N_REL)))
    return pl.pallas_call(
        _bias_kernel,
        grid=(depth,),
        in_specs=[pl.BlockSpec((None, SUBLANES, n_pad), lambda l: (l, 0, 0))],
        out_specs=pl.BlockSpec((None, H_CA, CA_EXT_TILES, QBLK, LANES), lambda l: (l, 0, 0, 0, 0)),
        out_shape=jax.ShapeDtypeStruct((depth, H_CA, CA_EXT_TILES, QBLK, LANES), F32),
        compiler_params=_params(("parallel",)),
        name="ca_bias",
    )(t)


def _ca_prompt_kernel(q_ref, k_ref, v_ref, bias_ref, o_ref, kt_ref, vt_ref, k_scr, v_scr):
    i = pl.program_id(1)
    n_pairs = CA_W // LANES

    @pl.when(i == 0)
    def _():
        k_scr[...] = k_ref[...].astype(BF16)
        for p in range(n_pairs):
            v_scr[p, :, 0:LANES] = v_ref[:, p * LANES:(p + 1) * LANES].astype(BF16)
            v_scr[p, :, LANES:] = jnp.ones((v_scr.shape[1], LANES), BF16)

    lower_half = lax.broadcasted_iota(jnp.int32, (QBLK, LANES), 1) < HEAD_DIM
    for u in range(QSTEP):
        rows = slice(u * QBLK, (u + 1) * QBLK)
        ideal = (i * QSTEP + u) * QBLK - CA_ROWS
        start = pl.multiple_of(jnp.maximum(ideal, 0), QBLK)
        off = (start - ideal) // LANES
        q = (q_ref[rows, :] * (HEAD_DIM ** -0.5)).astype(BF16)
        for p in range(n_pairs):
            cols = slice(p * LANES, (p + 1) * LANES)
            bias = [jnp.concatenate([bias_ref[2 * p + half, off + c] for c in range(CA_WIN_TILES)],
                                    axis=-1) for half in range(2)]
            k_win = k_scr[pl.ds(start, CA_WIN), cols]
            v_win = v_scr[p, pl.ds(start, CA_WIN), :]
            out = _pair_softmax_pv(q[:, cols], [k_win, k_win], [v_win, v_win], bias, lower_half)
            o_ref[rows, cols] = out.astype(o_ref.dtype)

    _emit_cache_tail(i, k_ref, v_ref, kt_ref, vt_ref)


def _ca_prompt(ca, bias, layer, batch, seq):
    tq = QSTEP * QBLK
    nq = seq // tq
    tail = pl.BlockSpec((None, CA_W, CA_ROWS), lambda b, i: (b, 0, 0))
    tail_shape = jax.ShapeDtypeStruct((batch, CA_W, CA_ROWS), F32)
    return pl.pallas_call(
        _ca_prompt_kernel,
        grid=(batch, nq),
        in_specs=[
            pl.BlockSpec((tq, CA_W), lambda b, i: (b * nq + i, 0)),
            pl.BlockSpec((seq, CA_W), lambda b, i: (b, 1)),
            pl.BlockSpec((seq, CA_W), lambda b, i: (b, 2)),
            pl.BlockSpec((None, H_CA, CA_EXT_TILES, QBLK, LANES), lambda b, i: (layer, 0, 0, 0, 0)),
        ],
        out_specs=[pl.BlockSpec((tq, CA_W), lambda b, i: (b * nq + i, 0)), tail, tail],
        out_shape=[jax.ShapeDtypeStruct((batch * seq, CA_W), BF16), tail_shape, tail_shape],
        scratch_shapes=[
            pltpu.VMEM((seq, CA_W), BF16),
            pltpu.VMEM((CA_W // LANES, seq, 2 * LANES), BF16),
        ],
        compiler_params=_params(("parallel", "arbitrary")),
        name="ca_prompt",
    )(ca, ca, ca, bias)


def _block_diag(blocks):
    z = jnp.zeros_like(blocks[0])
    return jnp.concatenate([jnp.concatenate([b if j == i else z for j in range(len(blocks))], axis=2)
                            for i, b in enumerate(blocks)], axis=1)


def _swap_halves_rows(x):
    half = x.shape[1] // 2
    return jnp.concatenate([x[:, half:], x[:, :half]], axis=1)


def _pair_softmax_pv_cached(q_pair, kt_halves, kn_halves, vt_halves, vn_halves, bias_c, bias_n,
                            lower_half, sinks=None):
    outs = []
    for half in range(2):
        q_h = jnp.where(lower_half, q_pair, 0) if half == 0 else jnp.where(lower_half, 0, q_pair)
        s_c = _bdot('bqd,bdk->bqk', q_h, kt_halves[half]) + bias_c[half]
        s_n = _bdot('bqd,bkd->bqk', q_h, kn_halves[half]) + bias_n[half]
        m = jnp.maximum(jnp.max(s_c, axis=-1, keepdims=True), jnp.max(s_n, axis=-1, keepdims=True))
        if sinks is not None:
            m = jnp.maximum(m, sinks[half])
        r = (_bdot('bqk,bfk->bqf', jnp.exp(s_c - m).astype(BF16), vt_halves[half])
             + _bdot('bqk,bkf->bqf', jnp.exp(s_n - m).astype(BF16), vn_halves[half]))
        denom = r[:, :, LANES:]
        if sinks is not None:
            denom = denom + jnp.exp(sinks[half] - m)
        outs.append(r[:, :, :LANES] / denom)
    return jnp.where(lower_half, outs[0], outs[1])


def _mixer_sample_kernel(sink_ref, ret_ref, swa_ref, ca_ref, cos_ref, sin_ref, dec_ref, qd_ref,
                         kd_ref, avg_ref, same_ref, sdec_ref, skt_ref, svt_ref, ckt_ref, cvt_ref,
                         st_ref, bias_ref, oa_ref, ob_ref, oc_ref, snew_ref):
    n_blk, t_len, _ = ret_ref.shape
    lower_half = lax.broadcasted_iota(jnp.int32, (t_len, LANES), 1) < HEAD_DIM
    ones_rows = lambda n: jnp.ones((n_blk, LANES, n), BF16)
    ones_cols = jnp.ones((n_blk, t_len, LANES), BF16)
    ca_tiles = CA_ROWS // LANES

    col = lambda c: ret_ref[:, :, c * RET_W:(c + 1) * RET_W]
    pairs = [_block_diag([st_ref[:, 2 * p], st_ref[:, 2 * p + 1]]) for p in range(RET_W // LANES)]
    out, new_state = _retention_block(col(0), col(1), col(2), col(3), cos_ref[...], sin_ref[...],
                                      dec_ref[...], qd_ref[...], kd_ref[...], avg_ref[...],
                                      same_ref[...], sdec_ref[...], pairs)
    oa_ref[...] = out.astype(oa_ref.dtype)
    for p in range(RET_W // LANES):
        snew_ref[:, 2 * p] = new_state[p][:, 0:HEAD_DIM, 0:HEAD_DIM]
        snew_ref[:, 2 * p + 1] = new_state[p][:, HEAD_DIM:, HEAD_DIM:]

    q = (swa_ref[:, :, 0:SWA_QW] * (HEAD_DIM ** -0.5)).astype(BF16)
    k_new = swa_ref[:, :, SWA_QW:SWA_QW + SWA_KW]
    v_new = swa_ref[:, :, SWA_QW + SWA_KW:SWA_COLS]
    k_cache = skt_ref[...]
    v_cache = svt_ref[...]
    swap_lanes = lambda x: pltpu.roll(x.reshape(n_blk * t_len, SWA_KW), HEAD_DIM, axis=1).reshape(x.shape)
    kt = [k_cache.astype(BF16), _swap_halves_rows(k_cache).astype(BF16)]
    vt = [jnp.concatenate([x.astype(BF16), ones_rows(SWA_ROWS)], axis=1)
          for x in (v_cache, _swap_halves_rows(v_cache))]
    kn = [k_new.astype(BF16), swap_lanes(k_new).astype(BF16)]
    vn = [jnp.concatenate([x.astype(BF16), ones_cols], axis=2) for x in (v_new, swap_lanes(v_new))]
    for c in range(SWA_QW // LANES):
        cols = slice(c * LANES, (c + 1) * LANES)
        heads = (2 * c, 2 * c + 1)
        swap = [(h // G_SWA) ^ half for half, h in enumerate(heads)]
        out = _pair_softmax_pv_cached(q[:, :, cols], [kt[x] for x in swap], [kn[x] for x in swap],
                                      [vt[x] for x in swap], [vn[x] for x in swap],
                                      [0.0, 0.0], [0.0, 0.0], lower_half,
                                      sinks=[sink_ref[h] for h in heads])
        ob_ref[:, :, cols] = out.astype(ob_ref.dtype)

    q = (ca_ref[:, :, 0:CA_W] * (HEAD_DIM ** -0.5)).astype(BF16)
    k_new = ca_ref[:, :, CA_W:2 * CA_W]
    v_new = ca_ref[:, :, 2 * CA_W:CA_COLS]
    for p in range(CA_W // LANES):
        cols = slice(p * LANES, (p + 1) * LANES)
        kt_p = ckt_ref[:, cols, :].astype(BF16)
        vt_p = jnp.concatenate([cvt_ref[:, cols, :].astype(BF16), ones_rows(CA_ROWS)], axis=1)
        kn_p = k_new[:, :, cols].astype(BF16)
        vn_p = jnp.concatenate([v_new[:, :, cols].astype(BF16), ones_cols], axis=2)
        bias_c = [jnp.concatenate([bias_ref[2 * p + half, c, 0:t_len, :] for c in range(ca_tiles)],
                                  axis=-1) for half in range(2)]
        bias_n = [bias_ref[2 * p + half, ca_tiles, 0:t_len, 0:t_len] for half in range(2)]
        out = _pair_softmax_pv_cached(q[:, :, cols], [kt_p, kt_p], [kn_p, kn_p], [vt_p, vt_p],
                                      [vn_p, vn_p], bias_c, bias_n, lower_half)
        oc_ref[:, :, cols] = out.astype(oc_ref.dtype)


def _mixer_sample(ret, swa, ca, sinks, tables, rope, cache_skt, cache_svt, cache_ckt, cache_cvt,
                  state_ret, bias, layer, n_seq, t_len, n_blk):
    cos, sin = rope
    whole = lambda shape: pl.BlockSpec(shape, lambda b: (0,) * len(shape))
    seq_rows = lambda width: pl.BlockSpec((n_blk, t_len, width), lambda b: (b, 0, 0))
    cache = lambda feat, rows: pl.BlockSpec((None, n_blk, feat, rows), lambda b: (layer, b, 0, 0))
    per_stream = lambda a: a.reshape(n_seq, t_len, a.shape[-1])
    oa, ob, oc, s_new = pl.pallas_call(
        _mixer_sample_kernel,
        grid=(n_seq // n_blk,),
        in_specs=[
            pl.BlockSpec(memory_space=pltpu.SMEM),
            seq_rows(RET_COLS), seq_rows(SWA_COLS), seq_rows(CA_COLS),
            whole((t_len, RET_W)), whole((t_len, RET_W)),
            *[whole(a.shape) for a in tables],
            cache(SWA_KW, SWA_ROWS), cache(SWA_KW, SWA_ROWS),
            cache(CA_W, CA_ROWS), cache(CA_W, CA_ROWS),
            pl.BlockSpec((None, n_blk, H_RET, HEAD_DIM, HEAD_DIM), lambda b: (layer, b, 0, 0, 0)),
            pl.BlockSpec((None, H_CA, CA_EXT_TILES, QBLK, LANES), lambda b: (layer, 0, 0, 0, 0)),
        ],
        out_specs=[
            seq_rows(RET_W), seq_rows(SWA_QW), seq_rows(CA_W),
            pl.BlockSpec((n_blk, H_RET, HEAD_DIM, HEAD_DIM), lambda b: (b, 0, 0, 0)),
        ],
        out_shape=[
            jax.ShapeDtypeStruct((n_seq, t_len, RET_W), BF16),
            jax.ShapeDtypeStruct((n_seq, t_len, SWA_QW), BF16),
            jax.ShapeDtypeStruct((n_seq, t_len, CA_W), BF16),
            jax.ShapeDtypeStruct((n_seq, H_RET, HEAD_DIM, HEAD_DIM), F32),
        ],
        compiler_params=_params(("parallel",)),
        name="mixer_sample",
    )(sinks, per_stream(ret), per_stream(swa), per_stream(ca), cos, sin, *tables, cache_skt,
      cache_svt, cache_ckt, cache_cvt, state_ret, bias)
    flat = lambda a: a.reshape(n_seq * t_len, a.shape[-1])
    return flat(oa), flat(ob), flat(oc), s_new


def _append_rows_feature_major(cache_t, new_rows):
    n_feat, n_rows = cache_t.shape
    t_len = new_rows.shape[0]
    rolled = pltpu.roll(cache_t, n_rows - t_len, axis=1)
    t_idx = lax.broadcasted_iota(jnp.int32, (t_len, LANES), 0)
    x_idx = lax.broadcasted_iota(jnp.int32, (t_len, LANES), 1)
    place = jnp.where(x_idx == LANES - t_len + t_idx, 1.0, 0.0).astype(BF16)
    hi = new_rows.astype(BF16)
    rem = new_rows - hi.astype(F32)
    mid = rem.astype(BF16)
    lo = (rem - mid.astype(F32)).astype(BF16)
    placed = (_dot_tn(hi, place) + _dot_tn(mid, place)) + _dot_tn(lo, place)
    lane = lax.broadcasted_iota(jnp.int32, (n_feat, LANES), 1)
    last = jnp.where(lane < LANES - t_len, rolled[:, n_rows - LANES:], placed)
    if n_rows == LANES:
        return last
    return jnp.concatenate([rolled[:, :n_rows - LANES], last], axis=1)


def _cache_roll_kernel(k_ref, v_ref, nk_ref, nv_ref, ok_ref, ov_ref):
    for n in range(k_ref.shape[0]):
        ok_ref[n] = _append_rows_feature_major(k_ref[n], nk_ref[n])
        ov_ref[n] = _append_rows_feature_major(v_ref[n], nv_ref[n])


def _cache_roll(cache_kt, cache_vt, new_k, new_v, n_blk):
    depth, n_seq, n_feat, n_rows = cache_kt.shape
    t_len = new_k.shape[2]
    cache = pl.BlockSpec((None, n_blk, n_feat, n_rows), lambda l, b: (l, b, 0, 0))
    new = pl.BlockSpec((None, n_blk, t_len, n_feat), lambda l, b: (l, b, 0, 0))
    shape = jax.ShapeDtypeStruct(cache_kt.shape, F32)
    return pl.pallas_call(
        _cache_roll_kernel,
        grid=(depth, n_seq // n_blk),
        in_specs=[cache, cache, new, new],
        out_specs=[cache, cache],
        out_shape=[shape, shape],
        compiler_params=_params(("parallel", "parallel")),
        name="cache_roll",
    )(cache_kt, cache_vt, new_k, new_v)


def _ffn_kernel(*refs, tm, tf, pad, shift, tiles_per_seq, final_norm):
    if final_norm:
        (x_ref, a_ref, b_ref, c_ref, wo_ref, gain_ref, wg_ref, wu_ref, wd_ref, cw_ref, cb_ref,
         prev_ref, gf_ref, o_ref, st_ref, hn_scr, act_scr, hist_scr) = refs
    else:
        (x_ref, a_ref, b_ref, c_ref, wo_ref, gain_ref, wg_ref, wu_ref, wd_ref, cw_ref, cb_ref,
         prev_ref, o_ref, st_ref, hn_scr, act_scr, hist_scr) = refs
    x = x_ref[...] + _dot(a_ref[...], wo_ref[0:RET_W, :])
    x = x + _dot(b_ref[...], wo_ref[RET_W:RET_W + SWA_QW, :])
    x = x + _dot(c_ref[...], wo_ref[RET_W + SWA_QW:D_MODEL, :])
    o_ref[...] = x
    hn_scr[...] = _rms(x, gain_ref[...]).astype(BF16)
    if tiles_per_seq == 1:
        hist = prev_ref
    else:
        hist = hist_scr

        @pl.when((pl.program_id(0) % tiles_per_seq) == 0)
        def _():
            hist_scr[...] = prev_ref[...]

    for c in range(D_FF // tf):
        cols = slice(c * tf, (c + 1) * tf)
        hn = hn_scr[...]
        g = _dot(hn, wg_ref[:, cols])
        u = _dot(hn, wu_ref[:, cols])
        g_ext = jnp.concatenate([hist[:, cols], g], axis=0)
        gc = cb_ref[:, cols] + cw_ref[0:1, cols] * pltpu.roll(g_ext, 2 * shift, axis=0)[pad:]
        gc = gc + cw_ref[1:2, cols] * pltpu.roll(g_ext, shift, axis=0)[pad:]
        gc = gc + cw_ref[2:3, cols] * g
        act_scr[:, cols] = (jax.nn.gelu(gc) * u).astype(BF16)
        tail = g_ext[tm:tm + pad]
        if tiles_per_seq > 1:
            hist_scr[:, cols] = tail
        st_ref[:, cols] = tail

    y = o_ref[...] + _dot(act_scr[...], wd_ref[...])
    if final_norm:
        y = _rms(y, gf_ref[...])
    o_ref[...] = y


def _outproj_ffn(x, oa, ob, oc, w_out, gain, w_gate, w_up, w_down, conv_w, conv_b, prev, layer, tm,
                 tf, shift, tiles_per_seq, final_gain=None):
    m = x.shape[0]
    n_groups, pad, _ = prev.shape
    final_norm = final_gain is not None
    resident = lambda shape: pl.BlockSpec((None,) + shape, lambda i: (layer, 0, 0),
                                          pipeline_mode=pl.Buffered(1))
    state = pl.BlockSpec((None, pad, D_FF), lambda i: (i // tiles_per_seq, 0, 0))
    rows = lambda width: pl.BlockSpec((tm, width), lambda i: (i, 0))
    in_specs = [
        rows(D_MODEL), rows(RET_W), rows(SWA_QW), rows(CA_W),
        resident((D_MODEL, D_MODEL)),
        resident((1, D_MODEL)),
        resident((D_MODEL, D_FF)), resident((D_MODEL, D_FF)), resident((D_FF, D_MODEL)),
        resident((CONV_W, D_FF)), resident((1, D_FF)),
        state,
    ]
    args = [x, oa, ob, oc, w_out, gain, w_gate, w_up, w_down, conv_w, conv_b, prev]
    if final_norm:
        in_specs.append(pl.BlockSpec((1, D_MODEL), lambda i: (0, 0)))
        args.append(final_gain)
    return pl.pallas_call(
        functools.partial(_ffn_kernel, tm=tm, tf=tf, pad=pad, shift=shift,
                          tiles_per_seq=tiles_per_seq, final_norm=final_norm),
        grid=(m // tm,),
        in_specs=in_specs,
        out_specs=[rows(D_MODEL), state],
        out_shape=[
            jax.ShapeDtypeStruct((m, D_MODEL), F32),
            jax.ShapeDtypeStruct((n_groups, pad, D_FF), F32),
        ],
        scratch_shapes=[
            pltpu.VMEM((tm, D_MODEL), BF16),
            pltpu.VMEM((tm, D_FF), BF16),
            pltpu.VMEM((pad, D_FF), F32),
        ],
        compiler_params=_params(("arbitrary",)),
        name="outproj_ffn",
    )(*args)


def kernel(x_prompt, x_sample, cache_swa_k, cache_swa_v, cache_ca_k, cache_ca_v, state_ret,
           state_ffn_conv, w_in, w_out, attn_sinks, rel_bias_table, norm_mix, norm_ffn, w_gate,
           w_up, conv_w, conv_b, w_down, norm_final):
    batch, seq, _ = x_prompt.shape
    n_seq, t_len, _ = x_sample.shape
    depth = w_in.shape[0]
    assert seq % (QSTEP * QBLK) == 0 and seq >= CA_WIN and t_len % SUBLANES == 0
    assert cache_swa_k.shape[2] == SWA_ROWS and cache_ca_k.shape[2] == CA_ROWS

    w_in_b = w_in.astype(BF16)
    w_out_b = w_out.astype(BF16)
    w_gate_b = w_gate.astype(BF16)
    w_up_b = w_up.astype(BF16)
    w_down_b = w_down.astype(BF16)
    g_mix = norm_mix.reshape(depth, 1, D_MODEL)
    g_ffn = norm_ffn.reshape(depth, 1, D_MODEL)
    g_fin = norm_final.reshape(1, D_MODEL)
    cb = conv_b.reshape(depth, 1, D_FF)

    rope_p = _rope_tables(jnp.arange(seq))
    rope_s = _rope_tables(PAST_LEN + jnp.arange(t_len))
    tab_p = _retention_tables(QBLK)
    tab_s = _retention_tables(t_len)
    bias = _build_bias(rel_bias_table)

    feature_major = lambda c: jnp.transpose(c, (0, 1, 3, 4, 2)).reshape(depth, n_seq, -1, c.shape[2])
    cskt, csvt, cckt, ccvt = map(feature_major, (cache_swa_k, cache_swa_v, cache_ca_k, cache_ca_v))

    hp = x_prompt.reshape(batch * seq, D_MODEL)
    m_s = n_seq * t_len
    hs = x_sample.reshape(m_s, D_MODEL)

    pad_p = SUBLANES
    pad_s = (CONV_W - 1) * n_seq
    prev_p = jnp.zeros((batch, pad_p, D_FF), F32)
    tm_p = 512
    tiles_per_seq = seq // tm_p

    outs_p = [[] for _ in range(6)]
    outs_s = [[] for _ in range(6)]
    for l in range(depth):
        last = l == depth - 1
        ret, swa, ca = _inproj(hp, g_mix, w_in_b, l, tm=512)
        oa, s_ret = _ret_prompt(ret, batch, seq, tab_p, rope_p, n_blk=8)
        ob, *swa_tail = _swa_prompt(swa, attn_sinks[l], batch, seq)
        oc, *ca_tail = _ca_prompt(ca, bias, l, batch, seq)
        hp, conv_st = _outproj_ffn(hp, oa, ob, oc, w_out_b, g_ffn, w_gate_b, w_up_b, w_down_b, conv_w,
                                   cb, prev_p, l, tm=tm_p, tf=256, shift=1,
                                   tiles_per_seq=tiles_per_seq, final_gain=g_fin if last else None)
        for lst, val in zip(outs_p[:4], swa_tail + ca_tail):
            lst.append(val)
        outs_p[4].append(s_ret)
        outs_p[5].append(conv_st[:, pad_p - (CONV_W - 1):, :])

        ret, swa, ca = _inproj(hs, g_mix, w_in_b, l, tm=m_s // 2)
        oa, ob, oc, s_new = _mixer_sample(ret, swa, ca, attn_sinks[l], tab_s, rope_s, cskt, csvt,
                                          cckt, ccvt, state_ret, bias, l, n_seq, t_len, n_blk=8)
        time_major = lambda a: jnp.transpose(a.reshape(n_seq, t_len, -1), (1, 0, 2)).reshape(m_s, -1)
        prev_s = jnp.transpose(state_ffn_conv[l], (1, 0, 2)).reshape(1, pad_s, D_FF)
        hs_tm, conv_st = _outproj_ffn(time_major(hs), time_major(oa), time_major(ob), time_major(oc),
                                      w_out_b, g_ffn, w_gate_b, w_up_b, w_down_b, conv_w, cb, prev_s,
                                      l, tm=m_s, tf=256, shift=n_seq, tiles_per_seq=1,
                                      final_gain=g_fin if last else None)
        hs = jnp.transpose(hs_tm.reshape(t_len, n_seq, D_MODEL), (1, 0, 2)).reshape(m_s, D_MODEL)
        swa3 = swa.reshape(n_seq, t_len, SWA_COLS)
        ca3 = ca.reshape(n_seq, t_len, CA_COLS)
        outs_s[0].append(swa3[:, :, SWA_QW:SWA_QW + SWA_KW])
        outs_s[1].append(swa3[:, :, SWA_QW + SWA_KW:])
        outs_s[2].append(ca3[:, :, CA_W:2 * CA_W])
        outs_s[3].append(ca3[:, :, 2 * CA_W:])
        outs_s[4].append(s_new)
        outs_s[5].append(jnp.transpose(conv_st.reshape(CONV_W - 1, n_seq, D_FF), (1, 0, 2)))

    y_prompt = hp.reshape(batch, seq, D_MODEL)
    y_sample = hs.reshape(n_seq, t_len, D_MODEL)
    stack = lambda v: jnp.stack(v, axis=0)
    row_major = lambda c, heads: jnp.transpose(
        c.reshape(depth, c.shape[1], heads, HEAD_DIM, -1), (0, 1, 4, 2, 3))
    swa_k_p = row_major(stack(outs_p[0]), KV_SWA)
    swa_v_p = row_major(stack(outs_p[1]), KV_SWA)
    ca_k_p = row_major(stack(outs_p[2]), H_CA)
    ca_v_p = row_major(stack(outs_p[3]), H_CA)
    ret_p = stack(outs_p[4])
    conv_p = stack(outs_p[5])

    swa_kt, swa_vt = _cache_roll(cskt, csvt, stack(outs_s[0]), stack(outs_s[1]), n_blk=8)
    ca_kt, ca_vt = _cache_roll(cckt, ccvt, stack(outs_s[2]), stack(outs_s[3]), n_blk=4)
    swa_k_s = row_major(swa_kt, KV_SWA)
    swa_v_s = row_major(swa_vt, KV_SWA)
    ca_k_s = row_major(ca_kt, H_CA)
    ca_v_s = row_major(ca_vt, H_CA)
    ret_s = stack(outs_s[4])
    conv_s = stack(outs_s[5])
    return (y_prompt, y_sample, swa_k_p, swa_v_p, ca_k_p, ca_v_p, ret_p, conv_p,
            swa_k_s, swa_v_s, ca_k_s, ca_v_s, ret_s, conv_s)
```

```python
import functools

import jax
import jax.numpy as jnp
from jax import lax
from jax.experimental import pallas as pl
from jax.experimental.pallas import tpu as pltpu

F32 = jnp.float32
BF16 = jnp.bfloat16

D_MODEL = 1024
CHUNK = 64
HEAD_DIM = 64
H_RET = 6
H_SWA = 6
KV_SWA = 2
G_SWA = H_SWA // KV_SWA
H_CA = 4
SWA_ROWS = 128
CA_ROWS = 512
REL_CLIP = 128
N_REL = 2 * REL_CLIP + 1
D_FF = 2816
CONV_W = 3
EPS = 1e-6
ROPE_BASE = 10000.0
NEG_INF = -1e30
PAST_LEN = 4096

RET_W = H_RET * HEAD_DIM
RET_COLS = 4 * RET_W
SWA_QW = H_SWA * HEAD_DIM
SWA_KW = KV_SWA * HEAD_DIM
SWA_COLS = SWA_QW + 2 * SWA_KW
CA_W = H_CA * HEAD_DIM
CA_COLS = 3 * CA_W
IN_COLS = RET_COLS + SWA_COLS + CA_COLS

LANES = 128
SUBLANES = 8
VMEM_LIMIT_BYTES = 56 * 1024 * 1024

QBLK = 128
SWA_QSTEP = 8
CA_QSTEP = 16
SWA_WIN = QBLK + SWA_ROWS
CA_WIN = QBLK + CA_ROWS
CA_WIN_TILES = CA_WIN // LANES
CA_EXT_TILES = (CA_WIN + CA_ROWS) // LANES
BIAS_ROW = CA_EXT_TILES * LANES + LANES


def _params(semantics):
    return pltpu.CompilerParams(dimension_semantics=semantics,
                                vmem_limit_bytes=VMEM_LIMIT_BYTES)


def _rms(x, gain):
    ms = jnp.mean(x * x, axis=-1, keepdims=True)
    return (x * lax.rsqrt(ms + EPS)) * gain


def _dot(a, b):
    return jnp.dot(a, b, preferred_element_type=F32)


def _dot_nt(a, b):
    return lax.dot_general(a, b, (((1,), (1,)), ((), ())), preferred_element_type=F32)


def _dot_tn(a, b):
    return lax.dot_general(a, b, (((0,), (0,)), ((), ())), preferred_element_type=F32)


def _inproj_kernel(x_ref, g_ref, w_ref, ret_ref, swa_ref, ca_ref):
    h = _rms(x_ref[...], g_ref[...]).astype(BF16)
    ret_ref[...] = _dot(h, w_ref[:, 0:RET_COLS])
    swa_ref[...] = _dot(h, w_ref[:, RET_COLS:RET_COLS + SWA_COLS])
    ca_ref[...] = _dot(h, w_ref[:, RET_COLS + SWA_COLS:IN_COLS])


def _inproj(x, gain, w_in, layer, tm):
    m = x.shape[0]
    row = lambda i: (i, 0)
    return pl.pallas_call(
        _inproj_kernel,
        grid=(m // tm,),
        in_specs=[
            pl.BlockSpec((tm, D_MODEL), row),
            pl.BlockSpec((None, 1, D_MODEL), lambda i: (layer, 0, 0)),
            pl.BlockSpec((None, D_MODEL, IN_COLS), lambda i: (layer, 0, 0)),
        ],
        out_specs=[
            pl.BlockSpec((tm, RET_COLS), row),
            pl.BlockSpec((tm, SWA_COLS), row),
            pl.BlockSpec((tm, CA_COLS), row),
        ],
        out_shape=[
            jax.ShapeDtypeStruct((m, RET_COLS), F32),
            jax.ShapeDtypeStruct((m, SWA_COLS), F32),
            jax.ShapeDtypeStruct((m, CA_COLS), F32),
        ],
        compiler_params=_params(("parallel",)),
        name="inproj",
    )(x, gain, w_in)


def _ret_log_decay():
    return jnp.log(1.0 - 2.0 ** (-5.0 - jnp.arange(H_RET, dtype=F32)))


def _rope_tables(pos):
    half = HEAD_DIM // 2
    inv = 1.0 / (ROPE_BASE ** (jnp.arange(half, dtype=F32) / half))
    ang = pos.astype(F32)[:, None] * inv[None, :]
    cos = jnp.cos(ang)
    sin = jnp.sin(ang)
    cos_h = jnp.concatenate([cos, cos], axis=-1)
    sin_h = jnp.concatenate([-sin, sin], axis=-1)
    return jnp.tile(cos_h, (1, H_RET)), jnp.tile(sin_h, (1, H_RET))


def _retention_tables(t_len):
    log_g = _ret_log_decay()
    t = jnp.arange(t_len, dtype=F32)
    diff = t[:, None] - t[None, :]
    decay = jnp.where(diff >= 0, jnp.exp(jnp.maximum(diff, 0.0)[None] * log_g[:, None, None]), 0.0)
    q_decay = jnp.exp((t + 1.0)[:, None] * log_g[None, :])
    k_decay = jnp.exp((t_len - 1.0 - t)[:, None] * log_g[None, :])
    head = jnp.arange(LANES) // HEAD_DIM
    same_head = (head[:, None] == head[None, :]).astype(F32)
    blk = jnp.exp(t_len * log_g).reshape(RET_W // LANES, LANES // HEAD_DIM)
    state_decay = jnp.broadcast_to(blk[:, head][:, :, None], (RET_W // LANES, LANES, LANES))
    return (decay.reshape(H_RET * t_len, t_len), jnp.repeat(q_decay, HEAD_DIM, axis=1),
            jnp.repeat(k_decay, HEAD_DIM, axis=1), (same_head / HEAD_DIM).astype(BF16), same_head,
            state_decay)


def _rope(x, cos, sin):
    nb, t_len, n = x.shape
    x2 = x.reshape(nb * t_len, n)
    lane = lax.broadcasted_iota(jnp.int32, x2.shape, 1)
    first_half = (lane & (HEAD_DIM - 1)) < (HEAD_DIM // 2)
    partner = jnp.where(first_half,
                        pltpu.roll(x2, n - HEAD_DIM // 2, axis=1),
                        pltpu.roll(x2, HEAD_DIM // 2, axis=1))
    return x * cos + partner.reshape(x.shape) * sin


def _group_mean(x, head_avg, split=True):
    nb, t_len, n = x.shape
    hi = x.astype(BF16).reshape(nb * t_len, n)
    if not split:
        return _dot(hi, head_avg).reshape(x.shape)
    lo = (x - x.astype(BF16).astype(F32)).astype(BF16).reshape(nb * t_len, n)
    r = _dot(jnp.concatenate([hi, lo], axis=0), head_avg)
    return (r[:nb * t_len] + r[nb * t_len:]).reshape(x.shape)


def _bdot(spec, a, b):
    return jnp.einsum(spec, a, b, preferred_element_type=F32)


def _retention_block(q, k, v, gate, cos, sin, dec, q_decay, k_decay, head_avg, same_head,
                     state_decay, state):
    nb, t_len, _ = q.shape
    n_cols = RET_W // LANES
    q = _rope(q, cos, sin)
    k = _rope(k, cos, sin) * (HEAD_DIM ** -0.5)
    lower_half = lax.broadcasted_iota(jnp.int32, (t_len, LANES), 1) < HEAD_DIM
    to_cols = lambda x: jnp.concatenate([x[:, :, p * LANES:(p + 1) * LANES] for p in range(n_cols)],
                                        axis=0)
    per_col = lambda x, tabs: jnp.concatenate([x[p * nb:(p + 1) * nb] * tabs[p] for p in range(n_cols)],
                                              axis=0)
    col_tabs = lambda tab: [tab[:, p * LANES:(p + 1) * LANES] for p in range(n_cols)]
    qc = to_cols(q)
    kc = to_cols(k).astype(BF16)
    kdc = to_cols(k * k_decay).astype(BF16)
    vc = to_cols(v).astype(BF16)
    st = jnp.concatenate(state, axis=0)
    q_stack = jnp.concatenate([jnp.where(lower_half, qc, 0.0).astype(BF16),
                               jnp.where(lower_half, 0.0, qc).astype(BF16)], axis=1)
    scores = per_col(_bdot('bqd,bkd->bqk', q_stack, kc),
                     [dec[2 * p * t_len:(2 * p + 2) * t_len] for p in range(n_cols)])
    inner_all = _bdot('bqk,bkd->bqd', scores.astype(BF16), vc)
    inner = jnp.where(lower_half, inner_all[:, :t_len], inner_all[:, t_len:])
    cross = per_col(_bdot('bqd,bde->bqe', qc.astype(BF16), st.astype(BF16)), col_tabs(q_decay))
    o = inner + cross
    d = o - _group_mean(o, head_avg)
    var = _group_mean(d * d, head_avg, split=False)
    on_cols = d * lax.rsqrt(var + EPS)
    new_st = (per_col(st, [state_decay[p] for p in range(n_cols)])
              + same_head * _bdot('bkd,bke->bde', kdc, vc))
    on = jnp.concatenate([on_cols[p * nb:(p + 1) * nb] for p in range(n_cols)], axis=-1)
    new_state = [new_st[p * nb:(p + 1) * nb] for p in range(n_cols)]
    return on * (gate * jax.nn.sigmoid(gate)), new_state


def _ret_prompt_kernel(q_ref, k_ref, v_ref, g_ref, cos_ref, sin_ref, dec_ref, qd_ref, kd_ref,
                       avg_ref, same_ref, sdec_ref, o_ref, s_ref, state_scr):
    t = pl.program_id(1)
    n_cols = RET_W // LANES

    @pl.when(t == 0)
    def _():
        state_scr[...] = jnp.zeros_like(state_scr)

    out, new_state = _retention_block(q_ref[...], k_ref[...], v_ref[...], g_ref[...], cos_ref[...],
                                      sin_ref[...], dec_ref[...], qd_ref[...], kd_ref[...],
                                      avg_ref[...], same_ref[...], sdec_ref[...],
                                      [state_scr[p] for p in range(n_cols)])
    o_ref[...] = out.astype(o_ref.dtype)
    for p in range(n_cols):
        state_scr[p] = new_state[p]

    @pl.when(t == pl.num_programs(1) - 1)
    def _():
        for p in range(n_cols):
            s_ref[:, 2 * p] = state_scr[p, :, 0:HEAD_DIM, 0:HEAD_DIM]
            s_ref[:, 2 * p + 1] = state_scr[p, :, HEAD_DIM:, HEAD_DIM:]


def _ret_prompt(ret, batch, seq, tables, rope, n_blk):
    tb = QBLK
    cos, sin = rope
    ret3 = ret.reshape(batch, seq, RET_COLS)
    col = lambda c: pl.BlockSpec((n_blk, tb, RET_W), lambda b, t: (b, t, c))
    pos = pl.BlockSpec((tb, RET_W), lambda b, t: (t, 0))
    whole = lambda a: pl.BlockSpec(a.shape, lambda b, t: (0,) * a.ndim)
    out, s_fin = pl.pallas_call(
        _ret_prompt_kernel,
        grid=(batch // n_blk, seq // tb),
        in_specs=[col(0), col(1), col(2), col(3), pos, pos] + [whole(a) for a in tables],
        out_specs=[
            pl.BlockSpec((n_blk, tb, RET_W), lambda b, t: (b, t, 0)),
            pl.BlockSpec((n_blk, H_RET, HEAD_DIM, HEAD_DIM), lambda b, t: (b, 0, 0, 0)),
        ],
        out_shape=[
            jax.ShapeDtypeStruct((batch, seq, RET_W), BF16),
            jax.ShapeDtypeStruct((batch, H_RET, HEAD_DIM, HEAD_DIM), F32),
        ],
        scratch_shapes=[pltpu.VMEM((RET_W // LANES, n_blk, LANES, LANES), F32)],
        compiler_params=_params(("parallel", "arbitrary")),
        name="ret_prompt",
    )(ret3, ret3, ret3, ret3, cos, sin, *tables)
    return out.reshape(batch * seq, RET_W), s_fin


def _emit_cache_tail(i, k_ref, v_ref, kt_ref, vt_ref):
    @pl.when(i == pl.num_programs(1) - 1)
    def _():
        n_rows = kt_ref.shape[-1]
        seq = k_ref.shape[0]
        kt_ref[...] = k_ref[seq - n_rows:seq, :].T
        vt_ref[...] = v_ref[seq - n_rows:seq, :].T


def _pair_softmax_pv(q_pair, k_halves, v_halves, bias_halves, lower_half, sinks=None):
    outs = []
    for half in range(2):
        q_h = jnp.where(lower_half, q_pair, 0) if half == 0 else jnp.where(lower_half, 0, q_pair)
        s = _dot_nt(q_h, k_halves[half]) + bias_halves[half]
        m = jnp.max(s, axis=-1, keepdims=True)
        if sinks is not None:
            m = jnp.maximum(m, sinks[half])
        r = _dot(jnp.exp(s - m).astype(BF16), v_halves[half])
        denom = r[:, LANES:]
        if sinks is not None:
            denom = denom + jnp.exp(sinks[half] - m)
        outs.append(r[:, :LANES] / denom)
    return jnp.where(lower_half, outs[0], outs[1])


def _swa_prompt_kernel(sink_ref, q_ref, k_ref, v_ref, o_ref, kt_ref, vt_ref, k_scr, v_scr):
    i = pl.program_id(1)

    @pl.when(i == 0)
    def _():
        k = k_ref[...]
        v = v_ref[...]
        ones = jnp.ones((k.shape[0], LANES), BF16)
        for x, (kx, vx) in enumerate(((k, v), (pltpu.roll(k, HEAD_DIM, axis=1),
                                               pltpu.roll(v, HEAD_DIM, axis=1)))):
            k_scr[x] = kx.astype(BF16)
            v_scr[x, :, 0:LANES] = vx.astype(BF16)
            v_scr[x, :, LANES:] = ones

    row = lax.broadcasted_iota(jnp.int32, (QBLK, SWA_WIN), 0)
    col = lax.broadcasted_iota(jnp.int32, (QBLK, SWA_WIN), 1)
    n_prev = SWA_ROWS // CHUNK
    lower_half = lax.broadcasted_iota(jnp.int32, (QBLK, LANES), 1) < HEAD_DIM
    for u in range(SWA_QSTEP):
        blk = i * SWA_QSTEP + u
        rows = slice(u * QBLK, (u + 1) * QBLK)
        start = pl.multiple_of(jnp.maximum(blk * QBLK - SWA_ROWS, 0), QBLK)
        win = pl.ds(start, SWA_WIN)
        q = (q_ref[rows, :] * (HEAD_DIM ** -0.5)).astype(BF16)
        behind = (blk * (QBLK // CHUNK) + row // CHUNK) - (start // CHUNK + col // CHUNK)
        mask = jnp.where(jnp.abs(2 * behind - n_prev) <= n_prev, 0.0, NEG_INF)
        for c in range(SWA_QW // LANES):
            cols = slice(c * LANES, (c + 1) * LANES)
            heads = (2 * c, 2 * c + 1)
            swap = [(h // G_SWA) ^ half for half, h in enumerate(heads)]
            out = _pair_softmax_pv(q[:, cols], [k_scr[x, win, :] for x in swap],
                                   [v_scr[x, win, :] for x in swap], [mask, mask], lower_half,
                                   sinks=[sink_ref[h] for h in heads])
            o_ref[rows, cols] = out.astype(o_ref.dtype)

    _emit_cache_tail(i, k_ref, v_ref, kt_ref, vt_ref)


def _swa_prompt(swa, sinks, batch, seq):
    tq = SWA_QSTEP * QBLK
    nq = seq // tq
    k_col = SWA_QW // SWA_KW
    tail = pl.BlockSpec((None, SWA_KW, SWA_ROWS), lambda b, i: (b, 0, 0))
    tail_shape = jax.ShapeDtypeStruct((batch, SWA_KW, SWA_ROWS), F32)
    return pl.pallas_call(
        _swa_prompt_kernel,
        grid=(batch, nq),
        in_specs=[
            pl.BlockSpec(memory_space=pltpu.SMEM),
            pl.BlockSpec((tq, SWA_QW), lambda b, i: (b * nq + i, 0)),
            pl.BlockSpec((seq, SWA_KW), lambda b, i: (b, k_col)),
            pl.BlockSpec((seq, SWA_KW), lambda b, i: (b, k_col + 1)),
        ],
        out_specs=[pl.BlockSpec((tq, SWA_QW), lambda b, i: (b * nq + i, 0)), tail, tail],
        out_shape=[jax.ShapeDtypeStruct((batch * seq, SWA_QW), BF16), tail_shape, tail_shape],
        scratch_shapes=[
            pltpu.VMEM((2, seq, SWA_KW), BF16),
            pltpu.VMEM((2, seq, 2 * LANES), BF16),
        ],
        compiler_params=_params(("parallel", "arbitrary")),
        name="swa_prompt",
    )(sinks, swa, swa, swa)


def _bias_kernel(t_ref, o_ref):
    n_pad = t_ref.shape[-1]
    r = lax.broadcasted_iota(jnp.int32, (n_pad, BIAS_ROW), 0)
    w = lax.broadcasted_iota(jnp.int32, (n_pad, BIAS_ROW), 1)
    idx = jnp.clip(BIAS_ROW // 2 - 1 - w, -REL_CLIP, REL_CLIP) + REL_CLIP
    onehot = jnp.where(r == idx, 1.0, 0.0).astype(BF16)
    t = t_ref[...]
    t_hi = t.astype(BF16)
    rem = t - t_hi.astype(F32)
    t_mid = rem.astype(BF16)
    t_lo = (rem - t_mid.astype(F32)).astype(BF16)
    f = (_dot(t_hi, onehot) + _dot(t_mid, onehot)) + _dot(t_lo, onehot)
    ext = CA_EXT_TILES * LANES
    row = lax.broadcasted_iota(jnp.int32, (QBLK, ext), 0)
    colj = lax.broadcasted_iota(jnp.int32, (QBLK, ext), 1)
    ahead = colj // CHUNK - row // CHUNK
    n_prev = CA_ROWS // CHUNK
    valid = jnp.abs(2 * ahead - n_prev) <= n_prev
    for h in range(H_CA):
        x = jnp.broadcast_to(f[h:h + 1, :], (QBLK, BIAS_ROW))
        y = pltpu.roll(x, BIAS_ROW - (QBLK - 1), axis=1, stride=1, stride_axis=0)
        y = jnp.where(valid, y[:, 0:ext], NEG_INF)
        for c in range(CA_EXT_TILES):
            o_ref[h, c] = y[:, c * LANES:(c + 1) * LANES]


def _build_bias(rel_bias_table):
    depth = rel_bias_table.shape[0]
    n_pad = 3 * LANES
    t = jnp.transpose(rel_bias_table, (0, 2, 1))
    t = jnp.pad(t, ((0, 0), (0, SUBLANES - H_CA), (0, n_pad - N_REL)))
    return pl.pallas_call(
        _bias_kernel,
        grid=(depth,),
        in_specs=[pl.BlockSpec((None, SUBLANES, n_pad), lambda l: (l, 0, 0))],
        out_specs=pl.BlockSpec((None, H_CA, CA_EXT_TILES, QBLK, LANES), lambda l: (l, 0, 0, 0, 0)),
        out_shape=jax.ShapeDtypeStruct((depth, H_CA, CA_EXT_TILES, QBLK, LANES), F32),
        compiler_params=_params(("parallel",)),
        name="ca_bias",
    )(t)


def _ca_prompt_kernel(q_ref, k_ref, v_ref, bias_ref, o_ref, kt_ref, vt_ref, k_scr, v_scr):
    i = pl.program_id(1)
    n_pairs = CA_W // LANES

    @pl.when(i == 0)
    def _():
        k_scr[...] = k_ref[...].astype(BF16)
        for p in range(n_pairs):
            v_scr[p, :, 0:LANES] = v_ref[:, p * LANES:(p + 1) * LANES].astype(BF16)
            v_scr[p, :, LANES:] = jnp.ones((v_scr.shape[1], LANES), BF16)

    lower_half = lax.broadcasted_iota(jnp.int32, (QBLK, LANES), 1) < HEAD_DIM
    for u in range(CA_QSTEP):
        rows = slice(u * QBLK, (u + 1) * QBLK)
        ideal = (i * CA_QSTEP + u) * QBLK - CA_ROWS
        start = pl.multiple_of(jnp.maximum(ideal, 0), QBLK)
        off = (start - ideal) // LANES
        q = (q_ref[rows, :] * (HEAD_DIM ** -0.5)).astype(BF16)
        for p in range(n_pairs):
            cols = slice(p * LANES, (p + 1) * LANES)
            bias = [jnp.concatenate([bias_ref[2 * p + half, off + c] for c in range(CA_WIN_TILES)],
                                    axis=-1) for half in range(2)]
            k_win = k_scr[pl.ds(start, CA_WIN), cols]
            v_win = v_scr[p, pl.ds(start, CA_WIN), :]
            out = _pair_softmax_pv(q[:, cols], [k_win, k_win], [v_win, v_win], bias, lower_half)
            o_ref[rows, cols] = out.astype(o_ref.dtype)

    _emit_cache_tail(i, k_ref, v_ref, kt_ref, vt_ref)


def _ca_prompt(ca, bias, layer, batch, seq):
    tq = CA_QSTEP * QBLK
    nq = seq // tq
    tail = pl.BlockSpec((None, CA_W, CA_ROWS), lambda b, i: (b, 0, 0))
    tail_shape = jax.ShapeDtypeStruct((batch, CA_W, CA_ROWS), F32)
    return pl.pallas_call(
        _ca_prompt_kernel,
        grid=(batch, nq),
        in_specs=[
            pl.BlockSpec((tq, CA_W), lambda b, i: (b * nq + i, 0)),
            pl.BlockSpec((seq, CA_W), lambda b, i: (b, 1)),
            pl.BlockSpec((seq, CA_W), lambda b, i: (b, 2)),
            pl.BlockSpec((None, H_CA, CA_EXT_TILES, QBLK, LANES), lambda b, i: (layer, 0, 0, 0, 0)),
        ],
        out_specs=[pl.BlockSpec((tq, CA_W), lambda b, i: (b * nq + i, 0)), tail, tail],
        out_shape=[jax.ShapeDtypeStruct((batch * seq, CA_W), BF16), tail_shape, tail_shape],
        scratch_shapes=[
            pltpu.VMEM((seq, CA_W), BF16),
            pltpu.VMEM((CA_W // LANES, seq, 2 * LANES), BF16),
        ],
        compiler_params=_params(("parallel", "arbitrary")),
        name="ca_prompt",
    )(ca, ca, ca, bias)


def _block_diag(blocks):
    z = jnp.zeros_like(blocks[0])
    return jnp.concatenate([jnp.concatenate([b if j == i else z for j in range(len(blocks))], axis=2)
                            for i, b in enumerate(blocks)], axis=1)


def _swap_halves_rows(x):
    half = x.shape[1] // 2
    return jnp.concatenate([x[:, half:], x[:, :half]], axis=1)


def _heads_softmax_pv_cached(q_heads, kt, kn, vt, vn, bias_c, bias_n, sink=None):
    s_c = _bdot('bqd,bdk->bqk', q_heads, kt) + bias_c
    s_n = _bdot('bqd,bkd->bqk', q_heads, kn) + bias_n
    m = jnp.maximum(jnp.max(s_c, axis=-1, keepdims=True), jnp.max(s_n, axis=-1, keepdims=True))
    if sink is not None:
        m = jnp.maximum(m, sink)
    r = (_bdot('bqk,bfk->bqf', jnp.exp(s_c - m).astype(BF16), vt)
         + _bdot('bqk,bkf->bqf', jnp.exp(s_n - m).astype(BF16), vn))
    denom = r[:, :, LANES:]
    if sink is not None:
        denom = denom + jnp.exp(sink - m)
    return r[:, :, :LANES] / denom


def _mixer_sample_kernel(sink_ref, ret_ref, swa_ref, ca_ref, cos_ref, sin_ref, dec_ref, qd_ref,
                         kd_ref, avg_ref, same_ref, sdec_ref, skt_ref, svt_ref, ckt_ref, cvt_ref,
                         st_ref, bias_ref, oa_ref, ob_ref, oc_ref, snew_ref):
    n_blk, t_len, _ = ret_ref.shape
    lower_half = lax.broadcasted_iota(jnp.int32, (t_len, LANES), 1) < HEAD_DIM
    ones_rows = lambda n: jnp.ones((n_blk, LANES, n), BF16)
    ones_cols = jnp.ones((n_blk, t_len, LANES), BF16)
    ca_tiles = CA_ROWS // LANES

    col = lambda c: ret_ref[:, :, c * RET_W:(c + 1) * RET_W]
    pairs = [_block_diag([st_ref[:, 2 * p], st_ref[:, 2 * p + 1]]) for p in range(RET_W // LANES)]
    out, new_state = _retention_block(col(0), col(1), col(2), col(3), cos_ref[...], sin_ref[...],
                                      dec_ref[...], qd_ref[...], kd_ref[...], avg_ref[...],
                                      same_ref[...], sdec_ref[...], pairs)
    oa_ref[...] = out.astype(oa_ref.dtype)
    for p in range(RET_W // LANES):
        snew_ref[:, 2 * p] = new_state[p][:, 0:HEAD_DIM, 0:HEAD_DIM]
        snew_ref[:, 2 * p + 1] = new_state[p][:, HEAD_DIM:, HEAD_DIM:]

    q = (swa_ref[:, :, 0:SWA_QW] * (HEAD_DIM ** -0.5)).astype(BF16)
    k_new = swa_ref[:, :, SWA_QW:SWA_QW + SWA_KW]
    v_new = swa_ref[:, :, SWA_QW + SWA_KW:SWA_COLS]
    k_cache = skt_ref[...]
    v_cache = svt_ref[...]
    swap_lanes = lambda x: pltpu.roll(x.reshape(n_blk * t_len, SWA_KW), HEAD_DIM, axis=1).reshape(x.shape)
    kt = [k_cache.astype(BF16), _swap_halves_rows(k_cache).astype(BF16)]
    vt = [jnp.concatenate([x.astype(BF16), ones_rows(SWA_ROWS)], axis=1)
          for x in (v_cache, _swap_halves_rows(v_cache))]
    kn = [k_new.astype(BF16), swap_lanes(k_new).astype(BF16)]
    vn = [jnp.concatenate([x.astype(BF16), ones_cols], axis=2) for x in (v_new, swap_lanes(v_new))]
    own_lanes = lambda x, half: jnp.where(lower_half, x, 0) if half == 0 else jnp.where(lower_half, 0, x)
    stack = lambda parts: jnp.concatenate(parts, axis=0)
    swap = [(h // G_SWA) ^ (h % 2) for h in range(H_SWA)]
    out = _heads_softmax_pv_cached(
        stack([own_lanes(q[:, :, (h // 2) * LANES:(h // 2 + 1) * LANES], h % 2) for h in range(H_SWA)]),
        stack([kt[x] for x in swap]), stack([kn[x] for x in swap]),
        stack([vt[x] for x in swap]), stack([vn[x] for x in swap]), 0.0, 0.0,
        stack([jnp.full((n_blk, 1, 1), sink_ref[h], F32) for h in range(H_SWA)]))
    for c in range(SWA_QW // LANES):
        ob_ref[:, :, c * LANES:(c + 1) * LANES] = jnp.where(
            lower_half, out[2 * c * n_blk:(2 * c + 1) * n_blk],
            out[(2 * c + 1) * n_blk:(2 * c + 2) * n_blk]).astype(ob_ref.dtype)

    q = (ca_ref[:, :, 0:CA_W] * (HEAD_DIM ** -0.5)).astype(BF16)
    k_new = ca_ref[:, :, CA_W:2 * CA_W]
    v_new = ca_ref[:, :, 2 * CA_W:CA_COLS]
    for p in range(CA_W // LANES):
        cols = slice(p * LANES, (p + 1) * LANES)
        kt_p = ckt_ref[:, cols, :].astype(BF16)
        vt_p = jnp.concatenate([cvt_ref[:, cols, :].astype(BF16), ones_rows(CA_ROWS)], axis=1)
        kn_p = k_new[:, :, cols].astype(BF16)
        vn_p = jnp.concatenate([v_new[:, :, cols].astype(BF16), ones_cols], axis=2)
        bias_c = [jnp.concatenate([bias_ref[2 * p + half, c, 0:t_len, :] for c in range(ca_tiles)],
                                  axis=-1) for half in range(2)]
        bias_n = [bias_ref[2 * p + half, ca_tiles, 0:t_len, 0:t_len] for half in range(2)]
        outs = [_heads_softmax_pv_cached(own_lanes(q[:, :, cols], half), kt_p, kn_p, vt_p, vn_p,
                                         bias_c[half], bias_n[half]) for half in range(2)]
        oc_ref[:, :, cols] = jnp.where(lower_half, outs[0], outs[1]).astype(oc_ref.dtype)


def _mixer_sample(ret, swa, ca, sinks, tables, rope, cache_skt, cache_svt, cache_ckt, cache_cvt,
                  state_ret, bias, layer, n_seq, t_len, n_blk):
    cos, sin = rope
    whole = lambda shape: pl.BlockSpec(shape, lambda b: (0,) * len(shape))
    seq_rows = lambda width: pl.BlockSpec((n_blk, t_len, width), lambda b: (b, 0, 0))
    cache = lambda feat, rows: pl.BlockSpec((None, n_blk, feat, rows), lambda b: (layer, b, 0, 0))
    per_stream = lambda a: a.reshape(n_seq, t_len, a.shape[-1])
    oa, ob, oc, s_new = pl.pallas_call(
        _mixer_sample_kernel,
        grid=(n_seq // n_blk,),
        in_specs=[
            pl.BlockSpec(memory_space=pltpu.SMEM),
            seq_rows(RET_COLS), seq_rows(SWA_COLS), seq_rows(CA_COLS),
            whole((t_len, RET_W)), whole((t_len, RET_W)),
            *[whole(a.shape) for a in tables],
            cache(SWA_KW, SWA_ROWS), cache(SWA_KW, SWA_ROWS),
            cache(CA_W, CA_ROWS), cache(CA_W, CA_ROWS),
            pl.BlockSpec((None, n_blk, H_RET, HEAD_DIM, HEAD_DIM), lambda b: (layer, b, 0, 0, 0)),
            pl.BlockSpec((None, H_CA, CA_EXT_TILES, QBLK, LANES), lambda b: (layer, 0, 0, 0, 0)),
        ],
        out_specs=[
            seq_rows(RET_W), seq_rows(SWA_QW), seq_rows(CA_W),
            pl.BlockSpec((n_blk, H_RET, HEAD_DIM, HEAD_DIM), lambda b: (b, 0, 0, 0)),
        ],
        out_shape=[
            jax.ShapeDtypeStruct((n_seq, t_len, RET_W), BF16),
            jax.ShapeDtypeStruct((n_seq, t_len, SWA_QW), BF16),
            jax.ShapeDtypeStruct((n_seq, t_len, CA_W), BF16),
            jax.ShapeDtypeStruct((n_seq, H_RET, HEAD_DIM, HEAD_DIM), F32),
        ],
        compiler_params=_params(("parallel",)),
        name="mixer_sample",
    )(sinks, per_stream(ret), per_stream(swa), per_stream(ca), cos, sin, *tables, cache_skt,
      cache_svt, cache_ckt, cache_cvt, state_ret, bias)
    flat = lambda a: a.reshape(n_seq * t_len, a.shape[-1])
    return flat(oa), flat(ob), flat(oc), s_new


def _append_rows_feature_major(cache_t, new_rows):
    n_feat, n_rows = cache_t.shape
    t_len = new_rows.shape[0]
    rolled = pltpu.roll(cache_t, n_rows - t_len, axis=1)
    t_idx = lax.broadcasted_iota(jnp.int32, (t_len, LANES), 0)
    x_idx = lax.broadcasted_iota(jnp.int32, (t_len, LANES), 1)
    place = jnp.where(x_idx == LANES - t_len + t_idx, 1.0, 0.0).astype(BF16)
    hi = new_rows.astype(BF16)
    rem = new_rows - hi.astype(F32)
    mid = rem.astype(BF16)
    lo = (rem - mid.astype(F32)).astype(BF16)
    placed = (_dot_tn(hi, place) + _dot_tn(mid, place)) + _dot_tn(lo, place)
    lane = lax.broadcasted_iota(jnp.int32, (n_feat, LANES), 1)
    last = jnp.where(lane < LANES - t_len, rolled[:, n_rows - LANES:], placed)
    if n_rows == LANES:
        return last
    return jnp.concatenate([rolled[:, :n_rows - LANES], last], axis=1)


def _cache_roll_kernel(k_ref, v_ref, nk_ref, nv_ref, ok_ref, ov_ref):
    for n in range(k_ref.shape[0]):
        ok_ref[n] = _append_rows_feature_major(k_ref[n], nk_ref[n])
        ov_ref[n] = _append_rows_feature_major(v_ref[n], nv_ref[n])


def _cache_roll(cache_kt, cache_vt, new_k, new_v, n_blk):
    depth, n_seq, n_feat, n_rows = cache_kt.shape
    t_len = new_k.shape[2]
    cache = pl.BlockSpec((None, n_blk, n_feat, n_rows), lambda l, b: (l, b, 0, 0))
    new = pl.BlockSpec((None, n_blk, t_len, n_feat), lambda l, b: (l, b, 0, 0))
    shape = jax.ShapeDtypeStruct(cache_kt.shape, F32)
    return pl.pallas_call(
        _cache_roll_kernel,
        grid=(depth, n_seq // n_blk),
        in_specs=[cache, cache, new, new],
        out_specs=[cache, cache],
        out_shape=[shape, shape],
        compiler_params=_params(("parallel", "parallel")),
        name="cache_roll",
    )(cache_kt, cache_vt, new_k, new_v)


def _ffn_kernel(*refs, tm, tf, pad, shift, tiles_per_seq, final_norm):
    if final_norm:
        (x_ref, a_ref, b_ref, c_ref, wo_ref, gain_ref, wg_ref, wu_ref, wd_ref, cw_ref, cb_ref,
         prev_ref, gf_ref, o_ref, st_ref, hn_scr, act_scr, hist_scr) = refs
    else:
        (x_ref, a_ref, b_ref, c_ref, wo_ref, gain_ref, wg_ref, wu_ref, wd_ref, cw_ref, cb_ref,
         prev_ref, o_ref, st_ref, hn_scr, act_scr, hist_scr) = refs
    x = x_ref[...] + _dot(a_ref[...], wo_ref[0:RET_W, :])
    x = x + _dot(b_ref[...], wo_ref[RET_W:RET_W + SWA_QW, :])
    x = x + _dot(c_ref[...], wo_ref[RET_W + SWA_QW:D_MODEL, :])
    o_ref[...] = x
    hn_scr[...] = _rms(x, gain_ref[...]).astype(BF16)
    if tiles_per_seq == 1:
        hist = prev_ref
    else:
        hist = hist_scr

        @pl.when((pl.program_id(0) % tiles_per_seq) == 0)
        def _():
            hist_scr[...] = prev_ref[...]

    for c in range(D_FF // tf):
        cols = slice(c * tf, (c + 1) * tf)
        hn = hn_scr[...]
        g = _dot(hn, wg_ref[:, cols])
        u = _dot(hn, wu_ref[:, cols])
        g_ext = jnp.concatenate([hist[:, cols], g], axis=0)
        gc = cb_ref[:, cols] + cw_ref[0:1, cols] * pltpu.roll(g_ext, 2 * shift, axis=0)[pad:]
        gc = gc + cw_ref[1:2, cols] * pltpu.roll(g_ext, shift, axis=0)[pad:]
        gc = gc + cw_ref[2:3, cols] * g
        act_scr[:, cols] = (jax.nn.gelu(gc) * u).astype(BF16)
        tail = g_ext[tm:tm + pad]
        if tiles_per_seq > 1:
            hist_scr[:, cols] = tail
        st_ref[:, cols] = tail

    y = o_ref[...] + _dot(act_scr[...], wd_ref[...])
    if final_norm:
        y = _rms(y, gf_ref[...])
    o_ref[...] = y


def _outproj_ffn(x, oa, ob, oc, w_out, gain, w_gate, w_up, w_down, conv_w, conv_b, prev, layer, tm,
                 tf, shift, tiles_per_seq, final_gain=None):
    m = x.shape[0]
    n_groups, pad, _ = prev.shape
    final_norm = final_gain is not None
    resident = lambda shape: pl.BlockSpec((None,) + shape, lambda i: (layer, 0, 0),
                                          pipeline_mode=pl.Buffered(1))
    state = pl.BlockSpec((None, pad, D_FF), lambda i: (i // tiles_per_seq, 0, 0))
    rows = lambda width: pl.BlockSpec((tm, width), lambda i: (i, 0))
    in_specs = [
        rows(D_MODEL), rows(RET_W), rows(SWA_QW), rows(CA_W),
        resident((D_MODEL, D_MODEL)),
        resident((1, D_MODEL)),
        resident((D_MODEL, D_FF)), resident((D_MODEL, D_FF)), resident((D_FF, D_MODEL)),
        resident((CONV_W, D_FF)), resident((1, D_FF)),
        state,
    ]
    args = [x, oa, ob, oc, w_out, gain, w_gate, w_up, w_down, conv_w, conv_b, prev]
    if final_norm:
        in_specs.append(pl.BlockSpec((1, D_MODEL), lambda i: (0, 0)))
        args.append(final_gain)
    return pl.pallas_call(
        functools.partial(_ffn_kernel, tm=tm, tf=tf, pad=pad, shift=shift,
                          tiles_per_seq=tiles_per_seq, final_norm=final_norm),
        grid=(m // tm,),
        in_specs=in_specs,
        out_specs=[rows(D_MODEL), state],
        out_shape=[
            jax.ShapeDtypeStruct((m, D_MODEL), F32),
            jax.ShapeDtypeStruct((n_groups, pad, D_FF), F32),
        ],
        scratch_shapes=[
            pltpu.VMEM((tm, D_MODEL), BF16),
            pltpu.VMEM((tm, D_FF), BF16),
            pltpu.VMEM((pad, D_FF), F32),
        ],
        compiler_params=_params(("arbitrary",)),
        name="outproj_ffn",
    )(*args)


def kernel(x_prompt, x_sample, cache_swa_k, cache_swa_v, cache_ca_k, cache_ca_v, state_ret,
           state_ffn_conv, w_in, w_out, attn_sinks, rel_bias_table, norm_mix, norm_ffn, w_gate,
           w_up, conv_w, conv_b, w_down, norm_final):
    batch, seq, _ = x_prompt.shape
    n_seq, t_len, _ = x_sample.shape
    depth = w_in.shape[0]
    assert seq % (max(SWA_QSTEP, CA_QSTEP) * QBLK) == 0 and seq >= CA_WIN and t_len % SUBLANES == 0
    assert cache_swa_k.shape[2] == SWA_ROWS and cache_ca_k.shape[2] == CA_ROWS

    w_in_b = w_in.astype(BF16)
    w_out_b = w_out.astype(BF16)
    w_gate_b = w_gate.astype(BF16)
    w_up_b = w_up.astype(BF16)
    w_down_b = w_down.astype(BF16)
    g_mix = norm_mix.reshape(depth, 1, D_MODEL)
    g_ffn = norm_ffn.reshape(depth, 1, D_MODEL)
    g_fin = norm_final.reshape(1, D_MODEL)
    cb = conv_b.reshape(depth, 1, D_FF)

    rope_p = _rope_tables(jnp.arange(seq))
    rope_s = _rope_tables(PAST_LEN + jnp.arange(t_len))
    tab_p = _retention_tables(QBLK)
    tab_s = _retention_tables(t_len)
    bias = _build_bias(rel_bias_table)

    feature_major = lambda c: jnp.transpose(c, (0, 1, 3, 4, 2)).reshape(depth, n_seq, -1, c.shape[2])
    cskt, csvt, cckt, ccvt = map(feature_major, (cache_swa_k, cache_swa_v, cache_ca_k, cache_ca_v))

    hp = x_prompt.reshape(batch * seq, D_MODEL)
    m_s = n_seq * t_len
    hs = x_sample.reshape(m_s, D_MODEL)

    pad_p = SUBLANES
    pad_s = (CONV_W - 1) * n_seq
    prev_p = jnp.zeros((batch, pad_p, D_FF), F32)
    tm_p = 512
    tiles_per_seq = seq // tm_p

    outs_p = [[] for _ in range(6)]
    outs_s = [[] for _ in range(6)]
    for l in range(depth):
        last = l == depth - 1
        ret, swa, ca = _inproj(hp, g_mix, w_in_b, l, tm=512)
        oa, s_ret = _ret_prompt(ret, batch, seq, tab_p, rope_p, n_blk=8)
        ob, *swa_tail = _swa_prompt(swa, attn_sinks[l], batch, seq)
        oc, *ca_tail = _ca_prompt(ca, bias, l, batch, seq)
        hp, conv_st = _outproj_ffn(hp, oa, ob, oc, w_out_b, g_ffn, w_gate_b, w_up_b, w_down_b, conv_w,
                                   cb, prev_p, l, tm=tm_p, tf=256, shift=1,
                                   tiles_per_seq=tiles_per_seq, final_gain=g_fin if last else None)
        for lst, val in zip(outs_p[:4], swa_tail + ca_tail):
            lst.append(val)
        outs_p[4].append(s_ret)
        outs_p[5].append(conv_st[:, pad_p - (CONV_W - 1):, :])

        ret, swa, ca = _inproj(hs, g_mix, w_in_b, l, tm=m_s // 2)
        oa, ob, oc, s_new = _mixer_sample(ret, swa, ca, attn_sinks[l], tab_s, rope_s, cskt, csvt,
                                          cckt, ccvt, state_ret, bias, l, n_seq, t_len, n_blk=8)
        time_major = lambda a: jnp.transpose(a.reshape(n_seq, t_len, -1), (1, 0, 2)).reshape(m_s, -1)
        prev_s = jnp.transpose(state_ffn_conv[l], (1, 0, 2)).reshape(1, pad_s, D_FF)
        hs_tm, conv_st = _outproj_ffn(time_major(hs), time_major(oa), time_major(ob), time_major(oc),
                                      w_out_b, g_ffn, w_gate_b, w_up_b, w_down_b, conv_w, cb, prev_s,
                                      l, tm=m_s, tf=256, shift=n_seq, tiles_per_seq=1,
                                      final_gain=g_fin if last else None)
        hs = jnp.transpose(hs_tm.reshape(t_len, n_seq, D_MODEL), (1, 0, 2)).reshape(m_s, D_MODEL)
        swa3 = swa.reshape(n_seq, t_len, SWA_COLS)
        ca3 = ca.reshape(n_seq, t_len, CA_COLS)
        outs_s[0].append(swa3[:, :, SWA_QW:SWA_QW + SWA_KW])
        outs_s[1].append(swa3[:, :, SWA_QW + SWA_KW:])
        outs_s[2].append(ca3[:, :, CA_W:2 * CA_W])
        outs_s[3].append(ca3[:, :, 2 * CA_W:])
        outs_s[4].append(s_new)
        outs_s[5].append(jnp.transpose(conv_st.reshape(CONV_W - 1, n_seq, D_FF), (1, 0, 2)))

    y_prompt = hp.reshape(batch, seq, D_MODEL)
    y_sample = hs.reshape(n_seq, t_len, D_MODEL)
    stack = lambda v: jnp.stack(v, axis=0)
    row_major = lambda c, heads: jnp.transpose(
        c.reshape(depth, c.shape[1], heads, HEAD_DIM, -1), (0, 1, 4, 2, 3))
    swa_k_p = row_major(stack(outs_p[0]), KV_SWA)
    swa_v_p = row_major(stack(outs_p[1]), KV_SWA)
    ca_k_p = row_major(stack(outs_p[2]), H_CA)
    ca_v_p = row_major(stack(outs_p[3]), H_CA)
    ret_p = stack(outs_p[4])
    conv_p = stack(outs_p[5])

    swa_kt, swa_vt = _cache_roll(cskt, csvt, stack(outs_s[0]), stack(outs_s[1]), n_blk=8)
    ca_kt, ca_vt = _cache_roll(cckt, ccvt, stack(outs_s[2]), stack(outs_s[3]), n_blk=4)
    swa_k_s = row_major(swa_kt, KV_SWA)
    swa_v_s = row_major(swa_vt, KV_SWA)
    ca_k_s = row_major(ca_kt, H_CA)
    ca_v_s = row_major(ca_vt, H_CA)
    ret_s = stack(outs_s[4])
    conv_s = stack(outs_s[5])
    return (y_prompt, y_sample, swa_k_p, swa_v_p, ca_k_p, ca_v_p, ret_p, conv_p,
            swa_k_s, swa_v_s, ca_k_s, ca_v_s, ret_s, conv_s)
```

```python
import functools

import jax
import jax.numpy as jnp
from jax import lax
from jax.experimental import pallas as pl
from jax.experimental.pallas import tpu as pltpu

F32 = jnp.float32
BF16 = jnp.bfloat16

D_MODEL = 1024
CHUNK = 64
HEAD_DIM = 64
H_RET = 6
H_SWA = 6
KV_SWA = 2
G_SWA = H_SWA // KV_SWA
H_CA = 4
SWA_ROWS = 128
CA_ROWS = 512
REL_CLIP = 128
N_REL = 2 * REL_CLIP + 1
D_FF = 2816
CONV_W = 3
EPS = 1e-6
ROPE_BASE = 10000.0
NEG_INF = -1e30
PAST_LEN = 4096

RET_W = H_RET * HEAD_DIM
RET_COLS = 4 * RET_W
SWA_QW = H_SWA * HEAD_DIM
SWA_KW = KV_SWA * HEAD_DIM
SWA_COLS = SWA_QW + 2 * SWA_KW
CA_W = H_CA * HEAD_DIM
CA_COLS = 3 * CA_W
IN_COLS = RET_COLS + SWA_COLS + CA_COLS

LANES = 128
SUBLANES = 8
VMEM_LIMIT_BYTES = 56 * 1024 * 1024

QBLK = 128
SWA_QSTEP = 8
CA_QSTEP = 16
SWA_WIN = QBLK + SWA_ROWS
CA_WIN = QBLK + CA_ROWS
CA_WIN_TILES = CA_WIN // LANES
CA_EXT_TILES = (CA_WIN + CA_ROWS) // LANES
BIAS_ROW = CA_EXT_TILES * LANES + LANES


def _params(semantics):
    return pltpu.CompilerParams(dimension_semantics=semantics,
                                vmem_limit_bytes=VMEM_LIMIT_BYTES)


def _rms(x, gain):
    ms = jnp.mean(x * x, axis=-1, keepdims=True)
    return (x * lax.rsqrt(ms + EPS)) * gain


def _dot(a, b):
    return jnp.dot(a, b, preferred_element_type=F32)


def _dot_nt(a, b):
    return lax.dot_general(a, b, (((1,), (1,)), ((), ())), preferred_element_type=F32)


def _dot_tn(a, b):
    return lax.dot_general(a, b, (((0,), (0,)), ((), ())), preferred_element_type=F32)


def _inproj_kernel(x_ref, g_ref, w_ref, ret_ref, swa_ref, ca_ref):
    h = _rms(x_ref[...], g_ref[...]).astype(BF16)
    ret_ref[...] = _dot(h, w_ref[:, 0:RET_COLS])
    swa_ref[...] = _dot(h, w_ref[:, RET_COLS:RET_COLS + SWA_COLS])
    ca_ref[...] = _dot(h, w_ref[:, RET_COLS + SWA_COLS:IN_COLS])


def _inproj(x, gain, w_in, layer, tm):
    m = x.shape[0]
    row = lambda i: (i, 0)
    return pl.pallas_call(
        _inproj_kernel,
        grid=(m // tm,),
        in_specs=[
            pl.BlockSpec((tm, D_MODEL), row),
            pl.BlockSpec((None, 1, D_MODEL), lambda i: (layer, 0, 0)),
            pl.BlockSpec((None, D_MODEL, IN_COLS), lambda i: (layer, 0, 0)),
        ],
        out_specs=[
            pl.BlockSpec((tm, RET_COLS), row),
            pl.BlockSpec((tm, SWA_COLS), row),
            pl.BlockSpec((tm, CA_COLS), row),
        ],
        out_shape=[
            jax.ShapeDtypeStruct((m, RET_COLS), F32),
            jax.ShapeDtypeStruct((m, SWA_COLS), F32),
            jax.ShapeDtypeStruct((m, CA_COLS), F32),
        ],
        compiler_params=_params(("parallel",)),
        name="inproj",
    )(x, gain, w_in)


def _ret_log_decay():
    return jnp.log(1.0 - 2.0 ** (-5.0 - jnp.arange(H_RET, dtype=F32)))


def _rope_tables(pos):
    half = HEAD_DIM // 2
    inv = 1.0 / (ROPE_BASE ** (jnp.arange(half, dtype=F32) / half))
    ang = pos.astype(F32)[:, None] * inv[None, :]
    cos = jnp.cos(ang)
    sin = jnp.sin(ang)
    cos_h = jnp.concatenate([cos, cos], axis=-1)
    sin_h = jnp.concatenate([-sin, sin], axis=-1)
    return jnp.tile(cos_h, (1, H_RET)), jnp.tile(sin_h, (1, H_RET))


def _retention_tables(t_len):
    log_g = _ret_log_decay()
    t = jnp.arange(t_len, dtype=F32)
    diff = t[:, None] - t[None, :]
    decay = jnp.where(diff >= 0, jnp.exp(jnp.maximum(diff, 0.0)[None] * log_g[:, None, None]), 0.0)
    q_decay = jnp.exp((t + 1.0)[:, None] * log_g[None, :])
    k_decay = jnp.exp((t_len - 1.0 - t)[:, None] * log_g[None, :])
    head = jnp.arange(LANES) // HEAD_DIM
    same_head = (head[:, None] == head[None, :]).astype(F32)
    blk = jnp.exp(t_len * log_g).reshape(RET_W // LANES, LANES // HEAD_DIM)
    state_decay = jnp.broadcast_to(blk[:, head][:, :, None], (RET_W // LANES, LANES, LANES))
    return (decay.reshape(H_RET * t_len, t_len), jnp.repeat(q_decay, HEAD_DIM, axis=1),
            jnp.repeat(k_decay, HEAD_DIM, axis=1), (same_head / HEAD_DIM).astype(BF16), same_head,
            state_decay)


def _rope(x, cos, sin):
    nb, t_len, n = x.shape
    x2 = x.reshape(nb * t_len, n)
    lane = lax.broadcasted_iota(jnp.int32, x2.shape, 1)
    first_half = (lane & (HEAD_DIM - 1)) < (HEAD_DIM // 2)
    partner = jnp.where(first_half,
                        pltpu.roll(x2, n - HEAD_DIM // 2, axis=1),
                        pltpu.roll(x2, HEAD_DIM // 2, axis=1))
    return x * cos + partner.reshape(x.shape) * sin


def _group_mean(x, head_avg, split=True):
    nb, t_len, n = x.shape
    hi = x.astype(BF16).reshape(nb * t_len, n)
    if not split:
        return _dot(hi, head_avg).reshape(x.shape)
    lo = (x - x.astype(BF16).astype(F32)).astype(BF16).reshape(nb * t_len, n)
    r = _dot(jnp.concatenate([hi, lo], axis=0), head_avg)
    return (r[:nb * t_len] + r[nb * t_len:]).reshape(x.shape)


def _bdot(spec, a, b):
    return jnp.einsum(spec, a, b, preferred_element_type=F32)


def _retention_block(q, k, v, gate, cos, sin, dec, q_decay, k_decay, head_avg, same_head,
                     state_decay, state):
    nb, t_len, _ = q.shape
    n_cols = RET_W // LANES
    q = _rope(q, cos, sin)
    k = _rope(k, cos, sin) * (HEAD_DIM ** -0.5)
    lower_half = lax.broadcasted_iota(jnp.int32, (t_len, LANES), 1) < HEAD_DIM
    to_cols = lambda x: jnp.concatenate([x[:, :, p * LANES:(p + 1) * LANES] for p in range(n_cols)],
                                        axis=0)
    per_col = lambda x, tabs: jnp.concatenate([x[p * nb:(p + 1) * nb] * tabs[p] for p in range(n_cols)],
                                              axis=0)
    col_tabs = lambda tab: [tab[:, p * LANES:(p + 1) * LANES] for p in range(n_cols)]
    qc = to_cols(q)
    kc = to_cols(k).astype(BF16)
    kdc = to_cols(k * k_decay).astype(BF16)
    vc = to_cols(v).astype(BF16)
    st = jnp.concatenate(state, axis=0)
    q_stack = jnp.concatenate([jnp.where(lower_half, qc, 0.0).astype(BF16),
                               jnp.where(lower_half, 0.0, qc).astype(BF16)], axis=1)
    scores = per_col(_bdot('bqd,bkd->bqk', q_stack, kc),
                     [dec[2 * p * t_len:(2 * p + 2) * t_len] for p in range(n_cols)])
    inner_all = _bdot('bqk,bkd->bqd', scores.astype(BF16), vc)
    inner = jnp.where(lower_half, inner_all[:, :t_len], inner_all[:, t_len:])
    cross = per_col(_bdot('bqd,bde->bqe', qc.astype(BF16), st.astype(BF16)), col_tabs(q_decay))
    o = inner + cross
    d = o - _group_mean(o, head_avg)
    var = _group_mean(d * d, head_avg, split=False)
    on_cols = d * lax.rsqrt(var + EPS)
    new_st = (per_col(st, [state_decay[p] for p in range(n_cols)])
              + same_head * _bdot('bkd,bke->bde', kdc, vc))
    on = jnp.concatenate([on_cols[p * nb:(p + 1) * nb] for p in range(n_cols)], axis=-1)
    new_state = [new_st[p * nb:(p + 1) * nb] for p in range(n_cols)]
    return on * (gate * jax.nn.sigmoid(gate)), new_state


def _ret_prompt_kernel(q_ref, k_ref, v_ref, g_ref, cos_ref, sin_ref, dec_ref, qd_ref, kd_ref,
                       avg_ref, same_ref, sdec_ref, o_ref, s_ref, state_scr):
    t = pl.program_id(1)
    n_cols = RET_W // LANES

    @pl.when(t == 0)
    def _():
        state_scr[...] = jnp.zeros_like(state_scr)

    out, new_state = _retention_block(q_ref[...], k_ref[...], v_ref[...], g_ref[...], cos_ref[...],
                                      sin_ref[...], dec_ref[...], qd_ref[...], kd_ref[...],
                                      avg_ref[...], same_ref[...], sdec_ref[...],
                                      [state_scr[p] for p in range(n_cols)])
    o_ref[...] = out.astype(o_ref.dtype)
    for p in range(n_cols):
        state_scr[p] = new_state[p]

    @pl.when(t == pl.num_programs(1) - 1)
    def _():
        for p in range(n_cols):
            s_ref[:, 2 * p] = state_scr[p, :, 0:HEAD_DIM, 0:HEAD_DIM]
            s_ref[:, 2 * p + 1] = state_scr[p, :, HEAD_DIM:, HEAD_DIM:]


def _ret_prompt(ret, batch, seq, tables, rope, n_blk):
    tb = QBLK
    cos, sin = rope
    ret3 = ret.reshape(batch, seq, RET_COLS)
    col = lambda c: pl.BlockSpec((n_blk, tb, RET_W), lambda b, t: (b, t, c))
    pos = pl.BlockSpec((tb, RET_W), lambda b, t: (t, 0))
    whole = lambda a: pl.BlockSpec(a.shape, lambda b, t: (0,) * a.ndim)
    out, s_fin = pl.pallas_call(
        _ret_prompt_kernel,
        grid=(batch // n_blk, seq // tb),
        in_specs=[col(0), col(1), col(2), col(3), pos, pos] + [whole(a) for a in tables],
        out_specs=[
            pl.BlockSpec((n_blk, tb, RET_W), lambda b, t: (b, t, 0)),
            pl.BlockSpec((n_blk, H_RET, HEAD_DIM, HEAD_DIM), lambda b, t: (b, 0, 0, 0)),
        ],
        out_shape=[
            jax.ShapeDtypeStruct((batch, seq, RET_W), BF16),
            jax.ShapeDtypeStruct((batch, H_RET, HEAD_DIM, HEAD_DIM), F32),
        ],
        scratch_shapes=[pltpu.VMEM((RET_W // LANES, n_blk, LANES, LANES), F32)],
        compiler_params=_params(("parallel", "arbitrary")),
        name="ret_prompt",
    )(ret3, ret3, ret3, ret3, cos, sin, *tables)
    return out.reshape(batch * seq, RET_W), s_fin


def _emit_cache_tail(i, k_ref, v_ref, kt_ref, vt_ref):
    @pl.when(i == pl.num_programs(1) - 1)
    def _():
        n_rows = kt_ref.shape[-1]
        seq = k_ref.shape[0]
        kt_ref[...] = k_ref[seq - n_rows:seq, :].T
        vt_ref[...] = v_ref[seq - n_rows:seq, :].T


def _pair_softmax_pv(q_pair, k_halves, v_halves, bias_halves, lower_half, sinks=None):
    outs = []
    for half in range(2):
        q_h = jnp.where(lower_half, q_pair, 0) if half == 0 else jnp.where(lower_half, 0, q_pair)
        s = _dot_nt(q_h, k_halves[half]) + bias_halves[half]
        m = jnp.max(s, axis=-1, keepdims=True)
        if sinks is not None:
            m = jnp.maximum(m, sinks[half])
        r = _dot(jnp.exp(s - m).astype(BF16), v_halves[half])
        denom = r[:, LANES:]
        if sinks is not None:
            denom = denom + jnp.exp(sinks[half] - m)
        outs.append(r[:, :LANES] / denom)
    return jnp.where(lower_half, outs[0], outs[1])


def _swa_prompt_kernel(sink_ref, q_ref, k_ref, v_ref, o_ref, kt_ref, vt_ref, k_scr, v_scr):
    i = pl.program_id(1)

    @pl.when(i == 0)
    def _():
        k = k_ref[...]
        v = v_ref[...]
        ones = jnp.ones((k.shape[0], LANES), BF16)
        for x, (kx, vx) in enumerate(((k, v), (pltpu.roll(k, HEAD_DIM, axis=1),
                                               pltpu.roll(v, HEAD_DIM, axis=1)))):
            k_scr[x] = kx.astype(BF16)
            v_scr[x, :, 0:LANES] = vx.astype(BF16)
            v_scr[x, :, LANES:] = ones

    row = lax.broadcasted_iota(jnp.int32, (QBLK, SWA_WIN), 0)
    col = lax.broadcasted_iota(jnp.int32, (QBLK, SWA_WIN), 1)
    n_prev = SWA_ROWS // CHUNK
    lower_half = lax.broadcasted_iota(jnp.int32, (QBLK, LANES), 1) < HEAD_DIM
    for u in range(SWA_QSTEP):
        blk = i * SWA_QSTEP + u
        rows = slice(u * QBLK, (u + 1) * QBLK)
        start = pl.multiple_of(jnp.maximum(blk * QBLK - SWA_ROWS, 0), QBLK)
        win = pl.ds(start, SWA_WIN)
        q = (q_ref[rows, :] * (HEAD_DIM ** -0.5)).astype(BF16)
        behind = (blk * (QBLK // CHUNK) + row // CHUNK) - (start // CHUNK + col // CHUNK)
        mask = jnp.where(jnp.abs(2 * behind - n_prev) <= n_prev, 0.0, NEG_INF)
        for c in range(SWA_QW // LANES):
            cols = slice(c * LANES, (c + 1) * LANES)
            heads = (2 * c, 2 * c + 1)
            swap = [(h // G_SWA) ^ half for half, h in enumerate(heads)]
            out = _pair_softmax_pv(q[:, cols], [k_scr[x, win, :] for x in swap],
                                   [v_scr[x, win, :] for x in swap], [mask, mask], lower_half,
                                   sinks=[sink_ref[h] for h in heads])
            o_ref[rows, cols] = out.astype(o_ref.dtype)

    _emit_cache_tail(i, k_ref, v_ref, kt_ref, vt_ref)


def _swa_prompt(swa, sinks, batch, seq):
    tq = SWA_QSTEP * QBLK
    nq = seq // tq
    k_col = SWA_QW // SWA_KW
    tail = pl.BlockSpec((None, SWA_KW, SWA_ROWS), lambda b, i: (b, 0, 0))
    tail_shape = jax.ShapeDtypeStruct((batch, SWA_KW, SWA_ROWS), F32)
    return pl.pallas_call(
        _swa_prompt_kernel,
        grid=(batch, nq),
        in_specs=[
            pl.BlockSpec(memory_space=pltpu.SMEM),
            pl.BlockSpec((tq, SWA_QW), lambda b, i: (b * nq + i, 0)),
            pl.BlockSpec((seq, SWA_KW), lambda b, i: (b, k_col)),
            pl.BlockSpec((seq, SWA_KW), lambda b, i: (b, k_col + 1)),
        ],
        out_specs=[pl.BlockSpec((tq, SWA_QW), lambda b, i: (b * nq + i, 0)), tail, tail],
        out_shape=[jax.ShapeDtypeStruct((batch * seq, SWA_QW), BF16), tail_shape, tail_shape],
        scratch_shapes=[
            pltpu.VMEM((2, seq, SWA_KW), BF16),
            pltpu.VMEM((2, seq, 2 * LANES), BF16),
        ],
        compiler_params=_params(("parallel", "arbitrary")),
        name="swa_prompt",
    )(sinks, swa, swa, swa)


def _bias_kernel(t_ref, o_ref):
    n_pad = t_ref.shape[-1]
    r = lax.broadcasted_iota(jnp.int32, (n_pad, BIAS_ROW), 0)
    w = lax.broadcasted_iota(jnp.int32, (n_pad, BIAS_ROW), 1)
    idx = jnp.clip(BIAS_ROW // 2 - 1 - w, -REL_CLIP, REL_CLIP) + REL_CLIP
    onehot = jnp.where(r == idx, 1.0, 0.0).astype(BF16)
    t = t_ref[...]
    t_hi = t.astype(BF16)
    rem = t - t_hi.astype(F32)
    t_mid = rem.astype(BF16)
    t_lo = (rem - t_mid.astype(F32)).astype(BF16)
    f = (_dot(t_hi, onehot) + _dot(t_mid, onehot)) + _dot(t_lo, onehot)
    ext = CA_EXT_TILES * LANES
    row = lax.broadcasted_iota(jnp.int32, (QBLK, ext), 0)
    colj = lax.broadcasted_iota(jnp.int32, (QBLK, ext), 1)
    ahead = colj // CHUNK - row // CHUNK
    n_prev = CA_ROWS // CHUNK
    valid = jnp.abs(2 * ahead - n_prev) <= n_prev
    for h in range(H_CA):
        x = jnp.broadcast_to(f[h:h + 1, :], (QBLK, BIAS_ROW))
        y = pltpu.roll(x, BIAS_ROW - (QBLK - 1), axis=1, stride=1, stride_axis=0)
        y = jnp.where(valid, y[:, 0:ext], NEG_INF)
        for c in range(CA_EXT_TILES):
            o_ref[h, c] = y[:, c * LANES:(c + 1) * LANES]


def _build_bias(rel_bias_table):
    depth = rel_bias_table.shape[0]
    n_pad = 3 * LANES
    t = jnp.transpose(rel_bias_table, (0, 2, 1))
    t = jnp.pad(t, ((0, 0), (0, SUBLANES - H_CA), (0, n_pad - N_REL)))
    return pl.pallas_call(
        _bias_kernel,
        grid=(depth,),
        in_specs=[pl.BlockSpec((None, SUBLANES, n_pad), lambda l: (l, 0, 0))],
        out_specs=pl.BlockSpec((None, H_CA, CA_EXT_TILES, QBLK, LANES), lambda l: (l, 0, 0, 0, 0)),
        out_shape=jax.ShapeDtypeStruct((depth, H_CA, CA_EXT_TILES, QBLK, LANES), F32),
        compiler_params=_params(("parallel",)),
        name="ca_bias",
    )(t)


def _ca_prompt_kernel(q_ref, k_ref, v_ref, bias_ref, o_ref, kt_ref, vt_ref, k_scr, v_scr):
    i = pl.program_id(1)
    n_pairs = CA_W // LANES

    @pl.when(i == 0)
    def _():
        k_scr[...] = k_ref[...].astype(BF16)
        for p in range(n_pairs):
            v_scr[p, :, 0:LANES] = v_ref[:, p * LANES:(p + 1) * LANES].astype(BF16)
            v_scr[p, :, LANES:] = jnp.ones((v_scr.shape[1], LANES), BF16)

    lower_half = lax.broadcasted_iota(jnp.int32, (QBLK, LANES), 1) < HEAD_DIM
    for u in range(CA_QSTEP):
        rows = slice(u * QBLK, (u + 1) * QBLK)
        ideal = (i * CA_QSTEP + u) * QBLK - CA_ROWS
        start = pl.multiple_of(jnp.maximum(ideal, 0), QBLK)
        off = (start - ideal) // LANES
        q = (q_ref[rows, :] * (HEAD_DIM ** -0.5)).astype(BF16)
        for p in range(n_pairs):
            cols = slice(p * LANES, (p + 1) * LANES)
            bias = [jnp.concatenate([bias_ref[2 * p + half, off + c] for c in range(CA_WIN_TILES)],
                                    axis=-1) for half in range(2)]
            k_win = k_scr[pl.ds(start, CA_WIN), cols]
            v_win = v_scr[p, pl.ds(start, CA_WIN), :]
            out = _pair_softmax_pv(q[:, cols], [k_win, k_win], [v_win, v_win], bias, lower_half)
            o_ref[rows, cols] = out.astype(o_ref.dtype)

    _emit_cache_tail(i, k_ref, v_ref, kt_ref, vt_ref)


def _ca_prompt(ca, bias, layer, batch, seq):
    tq = CA_QSTEP * QBLK
    nq = seq // tq
    tail = pl.BlockSpec((None, CA_W, CA_ROWS), lambda b, i: (b, 0, 0))
    tail_shape = jax.ShapeDtypeStruct((batch, CA_W, CA_ROWS), F32)
    return pl.pallas_call(
        _ca_prompt_kernel,
        grid=(batch, nq),
        in_specs=[
            pl.BlockSpec((tq, CA_W), lambda b, i: (b * nq + i, 0)),
            pl.BlockSpec((seq, CA_W), lambda b, i: (b, 1)),
            pl.BlockSpec((seq, CA_W), lambda b, i: (b, 2)),
            pl.BlockSpec((None, H_CA, CA_EXT_TILES, QBLK, LANES), lambda b, i: (layer, 0, 0, 0, 0)),
        ],
        out_specs=[pl.BlockSpec((tq, CA_W), lambda b, i: (b * nq + i, 0)), tail, tail],
        out_shape=[jax.ShapeDtypeStruct((batch * seq, CA_W), BF16), tail_shape, tail_shape],
        scratch_shapes=[
            pltpu.VMEM((seq, CA_W), BF16),
            pltpu.VMEM((CA_W // LANES, seq, 2 * LANES), BF16),
        ],
        compiler_params=_params(("parallel", "arbitrary")),
        name="ca_prompt",
    )(ca, ca, ca, bias)


def _block_diag(blocks):
    z = jnp.zeros_like(blocks[0])
    return jnp.concatenate([jnp.concatenate([b if j == i else z for j in range(len(blocks))], axis=2)
                            for i, b in enumerate(blocks)], axis=1)


def _swap_halves_rows(x):
    half = x.shape[1] // 2
    return jnp.concatenate([x[:, half:], x[:, :half]], axis=1)


def _heads_softmax_pv_cached(q_heads, kt, kn, vt, vn, bias_c, bias_n, sink=None):
    s_c = _bdot('bqd,bdk->bqk', q_heads, kt) + bias_c
    s_n = _bdot('bqd,bkd->bqk', q_heads, kn) + bias_n
    m = jnp.maximum(jnp.max(s_c, axis=-1, keepdims=True), jnp.max(s_n, axis=-1, keepdims=True))
    if sink is not None:
        m = jnp.maximum(m, sink)
    r = (_bdot('bqk,bfk->bqf', jnp.exp(s_c - m).astype(BF16), vt)
         + _bdot('bqk,bkf->bqf', jnp.exp(s_n - m).astype(BF16), vn))
    denom = r[:, :, LANES:]
    if sink is not None:
        denom = denom + jnp.exp(sink - m)
    return r[:, :, :LANES] / denom


def _mixer_sample_kernel(sink_ref, ret_ref, swa_ref, ca_ref, cos_ref, sin_ref, dec_ref, qd_ref,
                         kd_ref, avg_ref, same_ref, sdec_ref, skt_ref, svt_ref, ckt_ref, cvt_ref,
                         st_ref, bias_ref, oa_ref, ob_ref, oc_ref, snew_ref):
    n_blk, t_len, _ = ret_ref.shape
    lower_half = lax.broadcasted_iota(jnp.int32, (t_len, LANES), 1) < HEAD_DIM
    ones_rows = lambda n: jnp.ones((n_blk, LANES, n), BF16)
    ones_cols = jnp.ones((n_blk, t_len, LANES), BF16)
    ca_tiles = CA_ROWS // LANES

    col = lambda c: ret_ref[:, :, c * RET_W:(c + 1) * RET_W]
    pairs = [_block_diag([st_ref[:, 2 * p], st_ref[:, 2 * p + 1]]) for p in range(RET_W // LANES)]
    out, new_state = _retention_block(col(0), col(1), col(2), col(3), cos_ref[...], sin_ref[...],
                                      dec_ref[...], qd_ref[...], kd_ref[...], avg_ref[...],
                                      same_ref[...], sdec_ref[...], pairs)
    oa_ref[...] = out.astype(oa_ref.dtype)
    for p in range(RET_W // LANES):
        snew_ref[:, 2 * p] = new_state[p][:, 0:HEAD_DIM, 0:HEAD_DIM]
        snew_ref[:, 2 * p + 1] = new_state[p][:, HEAD_DIM:, HEAD_DIM:]

    q = (swa_ref[:, :, 0:SWA_QW] * (HEAD_DIM ** -0.5)).astype(BF16)
    k_new = swa_ref[:, :, SWA_QW:SWA_QW + SWA_KW]
    v_new = swa_ref[:, :, SWA_QW + SWA_KW:SWA_COLS]
    k_cache = skt_ref[...]
    v_cache = svt_ref[...]
    swap_lanes = lambda x: pltpu.roll(x.reshape(n_blk * t_len, SWA_KW), HEAD_DIM, axis=1).reshape(x.shape)
    kt = [k_cache.astype(BF16), _swap_halves_rows(k_cache).astype(BF16)]
    vt = [jnp.concatenate([x.astype(BF16), ones_rows(SWA_ROWS)], axis=1)
          for x in (v_cache, _swap_halves_rows(v_cache))]
    kn = [k_new.astype(BF16), swap_lanes(k_new).astype(BF16)]
    vn = [jnp.concatenate([x.astype(BF16), ones_cols], axis=2) for x in (v_new, swap_lanes(v_new))]
    own_lanes = lambda x, half: jnp.where(lower_half, x, 0) if half == 0 else jnp.where(lower_half, 0, x)
    stack = lambda parts: jnp.concatenate(parts, axis=0)
    swap = [(h // G_SWA) ^ (h % 2) for h in range(H_SWA)]
    out = _heads_softmax_pv_cached(
        stack([own_lanes(q[:, :, (h // 2) * LANES:(h // 2 + 1) * LANES], h % 2) for h in range(H_SWA)]),
        stack([kt[x] for x in swap]), stack([kn[x] for x in swap]),
        stack([vt[x] for x in swap]), stack([vn[x] for x in swap]), 0.0, 0.0,
        stack([jnp.full((n_blk, 1, 1), sink_ref[h], F32) for h in range(H_SWA)]))
    for c in range(SWA_QW // LANES):
        ob_ref[:, :, c * LANES:(c + 1) * LANES] = jnp.where(
            lower_half, out[2 * c * n_blk:(2 * c + 1) * n_blk],
            out[(2 * c + 1) * n_blk:(2 * c + 2) * n_blk]).astype(ob_ref.dtype)

    q = (ca_ref[:, :, 0:CA_W] * (HEAD_DIM ** -0.5)).astype(BF16)
    k_new = ca_ref[:, :, CA_W:2 * CA_W]
    v_new = ca_ref[:, :, 2 * CA_W:CA_COLS]
    for p in range(CA_W // LANES):
        cols = slice(p * LANES, (p + 1) * LANES)
        kt_p = ckt_ref[:, cols, :].astype(BF16)
        vt_p = jnp.concatenate([cvt_ref[:, cols, :].astype(BF16), ones_rows(CA_ROWS)], axis=1)
        kn_p = k_new[:, :, cols].astype(BF16)
        vn_p = jnp.concatenate([v_new[:, :, cols].astype(BF16), ones_cols], axis=2)
        bias_c = [jnp.concatenate([bias_ref[2 * p + half, c, 0:t_len, :] for c in range(ca_tiles)],
                                  axis=-1) for half in range(2)]
        bias_n = [bias_ref[2 * p + half, ca_tiles, 0:t_len, 0:t_len] for half in range(2)]
        outs = [_heads_softmax_pv_cached(own_lanes(q[:, :, cols], half), kt_p, kn_p, vt_p, vn_p,
                                         bias_c[half], bias_n[half]) for half in range(2)]
        oc_ref[:, :, cols] = jnp.where(lower_half, outs[0], outs[1]).astype(oc_ref.dtype)


def _mixer_sample(ret, swa, ca, sinks, tables, rope, cache_skt, cache_svt, cache_ckt, cache_cvt,
                  state_ret, bias, layer, n_seq, t_len, n_blk):
    cos, sin = rope
    whole = lambda shape: pl.BlockSpec(shape, lambda b: (0,) * len(shape))
    seq_rows = lambda width: pl.BlockSpec((n_blk, t_len, width), lambda b: (b, 0, 0))
    cache = lambda feat, rows: pl.BlockSpec((None, n_blk, feat, rows), lambda b: (layer, b, 0, 0))
    per_stream = lambda a: a.reshape(n_seq, t_len, a.shape[-1])
    oa, ob, oc, s_new = pl.pallas_call(
        _mixer_sample_kernel,
        grid=(n_seq // n_blk,),
        in_specs=[
            pl.BlockSpec(memory_space=pltpu.SMEM),
            seq_rows(RET_COLS), seq_rows(SWA_COLS), seq_rows(CA_COLS),
            whole((t_len, RET_W)), whole((t_len, RET_W)),
            *[whole(a.shape) for a in tables],
            cache(SWA_KW, SWA_ROWS), cache(SWA_KW, SWA_ROWS),
            cache(CA_W, CA_ROWS), cache(CA_W, CA_ROWS),
            pl.BlockSpec((None, n_blk, H_RET, HEAD_DIM, HEAD_DIM), lambda b: (layer, b, 0, 0, 0)),
            pl.BlockSpec((None, H_CA, CA_EXT_TILES, QBLK, LANES), lambda b: (layer, 0, 0, 0, 0)),
        ],
        out_specs=[
            seq_rows(RET_W), seq_rows(SWA_QW), seq_rows(CA_W),
            pl.BlockSpec((n_blk, H_RET, HEAD_DIM, HEAD_DIM), lambda b: (b, 0, 0, 0)),
        ],
        out_shape=[
            jax.ShapeDtypeStruct((n_seq, t_len, RET_W), BF16),
            jax.ShapeDtypeStruct((n_seq, t_len, SWA_QW), BF16),
            jax.ShapeDtypeStruct((n_seq, t_len, CA_W), BF16),
            jax.ShapeDtypeStruct((n_seq, H_RET, HEAD_DIM, HEAD_DIM), F32),
        ],
        compiler_params=_params(("parallel",)),
        name="mixer_sample",
    )(sinks, per_stream(ret), per_stream(swa), per_stream(ca), cos, sin, *tables, cache_skt,
      cache_svt, cache_ckt, cache_cvt, state_ret, bias)
    flat = lambda a: a.reshape(n_seq * t_len, a.shape[-1])
    return flat(oa), flat(ob), flat(oc), s_new


def _append_rows_feature_major(cache_t, new_rows):
    n_feat, n_rows = cache_t.shape
    t_len = new_rows.shape[0]
    rolled = pltpu.roll(cache_t, n_rows - t_len, axis=1)
    t_idx = lax.broadcasted_iota(jnp.int32, (t_len, LANES), 0)
    x_idx = lax.broadcasted_iota(jnp.int32, (t_len, LANES), 1)
    place = jnp.where(x_idx == LANES - t_len + t_idx, 1.0, 0.0).astype(BF16)
    hi = new_rows.astype(BF16)
    rem = new_rows - hi.astype(F32)
    mid = rem.astype(BF16)
    lo = (rem - mid.astype(F32)).astype(BF16)
    placed = (_dot_tn(hi, place) + _dot_tn(mid, place)) + _dot_tn(lo, place)
    lane = lax.broadcasted_iota(jnp.int32, (n_feat, LANES), 1)
    last = jnp.where(lane < LANES - t_len, rolled[:, n_rows - LANES:], placed)
    if n_rows == LANES:
        return last
    return jnp.concatenate([rolled[:, :n_rows - LANES], last], axis=1)


def _cache_roll_kernel(k_ref, v_ref, nk_ref, nv_ref, ok_ref, ov_ref):
    for n in range(k_ref.shape[0]):
        ok_ref[n] = _append_rows_feature_major(k_ref[n], nk_ref[n])
        ov_ref[n] = _append_rows_feature_major(v_ref[n], nv_ref[n])


def _cache_roll(cache_kt, cache_vt, new_k, new_v, n_blk):
    depth, n_seq, n_feat, n_rows = cache_kt.shape
    t_len = new_k.shape[2]
    cache = pl.BlockSpec((None, n_blk, n_feat, n_rows), lambda l, b: (l, b, 0, 0))
    new = pl.BlockSpec((None, n_blk, t_len, n_feat), lambda l, b: (l, b, 0, 0))
    shape = jax.ShapeDtypeStruct(cache_kt.shape, F32)
    return pl.pallas_call(
        _cache_roll_kernel,
        grid=(depth, n_seq // n_blk),
        in_specs=[cache, cache, new, new],
        out_specs=[cache, cache],
        out_shape=[shape, shape],
        compiler_params=_params(("parallel", "parallel")),
        name="cache_roll",
    )(cache_kt, cache_vt, new_k, new_v)


def _ffn_kernel(*refs, tm, tf, pad, shift, tiles_per_seq, final_norm, roll):
    refs = list(refs)
    (x_ref, a_ref, b_ref, c_ref, wo_ref, gain_ref, wg_ref, wu_ref, wd_ref, cw_ref, cb_ref,
     prev_ref) = refs[:12]
    del refs[:12]
    gf_ref = refs.pop(0) if final_norm else None
    if roll:
        ck_ref, cv_ref, nk_ref, nv_ref = refs[:4]
        del refs[:4]
    o_ref, st_ref = refs[:2]
    del refs[:2]
    if roll:
        ok_ref, ov_ref = refs[:2]
        del refs[:2]
        for l in range(ck_ref.shape[0]):
            ok_ref[l, 0] = _append_rows_feature_major(ck_ref[l, 0], nk_ref[l, 0])
            ov_ref[l, 0] = _append_rows_feature_major(cv_ref[l, 0], nv_ref[l, 0])
    hn_scr, act_scr, hist_scr = refs
    x = x_ref[...] + _dot(a_ref[...], wo_ref[0:RET_W, :])
    x = x + _dot(b_ref[...], wo_ref[RET_W:RET_W + SWA_QW, :])
    x = x + _dot(c_ref[...], wo_ref[RET_W + SWA_QW:D_MODEL, :])
    o_ref[...] = x
    hn_scr[...] = _rms(x, gain_ref[...]).astype(BF16)
    if tiles_per_seq == 1:
        hist = prev_ref
    else:
        hist = hist_scr

        @pl.when((pl.program_id(0) % tiles_per_seq) == 0)
        def _():
            hist_scr[...] = prev_ref[...]

    for c in range(D_FF // tf):
        cols = slice(c * tf, (c + 1) * tf)
        hn = hn_scr[...]
        g = _dot(hn, wg_ref[:, cols])
        u = _dot(hn, wu_ref[:, cols])
        g_ext = jnp.concatenate([hist[:, cols], g], axis=0)
        gc = cb_ref[:, cols] + cw_ref[0:1, cols] * pltpu.roll(g_ext, 2 * shift, axis=0)[pad:]
        gc = gc + cw_ref[1:2, cols] * pltpu.roll(g_ext, shift, axis=0)[pad:]
        gc = gc + cw_ref[2:3, cols] * g
        act_scr[:, cols] = (jax.nn.gelu(gc) * u).astype(BF16)
        tail = g_ext[tm:tm + pad]
        if tiles_per_seq > 1:
            hist_scr[:, cols] = tail
        st_ref[:, cols] = tail

    y = o_ref[...] + _dot(act_scr[...], wd_ref[...])
    if final_norm:
        y = _rms(y, gf_ref[...])
    o_ref[...] = y


def _outproj_ffn(x, oa, ob, oc, w_out, gain, w_gate, w_up, w_down, conv_w, conv_b, prev, layer, tm,
                 tf, shift, tiles_per_seq, final_gain=None, roll=None):
    m = x.shape[0]
    n_groups, pad, _ = prev.shape
    final_norm = final_gain is not None
    resident = lambda shape: pl.BlockSpec((None,) + shape, lambda i: (layer, 0, 0),
                                          pipeline_mode=pl.Buffered(1))
    state = pl.BlockSpec((None, pad, D_FF), lambda i: (i // tiles_per_seq, 0, 0))
    rows = lambda width: pl.BlockSpec((tm, width), lambda i: (i, 0))
    in_specs = [
        rows(D_MODEL), rows(RET_W), rows(SWA_QW), rows(CA_W),
        resident((D_MODEL, D_MODEL)),
        resident((1, D_MODEL)),
        resident((D_MODEL, D_FF)), resident((D_MODEL, D_FF)), resident((D_FF, D_MODEL)),
        resident((CONV_W, D_FF)), resident((1, D_FF)),
        state,
    ]
    args = [x, oa, ob, oc, w_out, gain, w_gate, w_up, w_down, conv_w, conv_b, prev]
    if final_norm:
        in_specs.append(pl.BlockSpec((1, D_MODEL), lambda i: (0, 0)))
        args.append(final_gain)
    out_specs = [rows(D_MODEL), state]
    out_shape = [jax.ShapeDtypeStruct((m, D_MODEL), F32),
                 jax.ShapeDtypeStruct((n_groups, pad, D_FF), F32)]
    if roll is not None:
        depth, n_seq, n_feat, n_rows = roll[0].shape
        assert n_seq == m // tm
        cache = pl.BlockSpec((depth, 1, n_feat, n_rows), lambda i: (0, i, 0, 0))
        new = pl.BlockSpec((depth, 1, roll[2].shape[2], n_feat), lambda i: (0, i, 0, 0))
        in_specs += [cache, cache, new, new]
        args += list(roll)
        out_specs += [cache, cache]
        out_shape += [jax.ShapeDtypeStruct(roll[0].shape, F32)] * 2
    return pl.pallas_call(
        functools.partial(_ffn_kernel, tm=tm, tf=tf, pad=pad, shift=shift,
                          tiles_per_seq=tiles_per_seq, final_norm=final_norm,
                          roll=roll is not None),
        grid=(m // tm,),
        in_specs=in_specs,
        out_specs=out_specs,
        out_shape=out_shape,
        scratch_shapes=[
            pltpu.VMEM((tm, D_MODEL), BF16),
            pltpu.VMEM((tm, D_FF), BF16),
            pltpu.VMEM((pad, D_FF), F32),
        ],
        compiler_params=_params(("arbitrary",)),
        name="outproj_ffn",
    )(*args)


def kernel(x_prompt, x_sample, cache_swa_k, cache_swa_v, cache_ca_k, cache_ca_v, state_ret,
           state_ffn_conv, w_in, w_out, attn_sinks, rel_bias_table, norm_mix, norm_ffn, w_gate,
           w_up, conv_w, conv_b, w_down, norm_final):
    batch, seq, _ = x_prompt.shape
    n_seq, t_len, _ = x_sample.shape
    depth = w_in.shape[0]
    assert seq % (max(SWA_QSTEP, CA_QSTEP) * QBLK) == 0 and seq >= CA_WIN and t_len % SUBLANES == 0
    assert cache_swa_k.shape[2] == SWA_ROWS and cache_ca_k.shape[2] == CA_ROWS

    w_in_b = w_in.astype(BF16)
    w_out_b = w_out.astype(BF16)
    w_gate_b = w_gate.astype(BF16)
    w_up_b = w_up.astype(BF16)
    w_down_b = w_down.astype(BF16)
    g_mix = norm_mix.reshape(depth, 1, D_MODEL)
    g_ffn = norm_ffn.reshape(depth, 1, D_MODEL)
    g_fin = norm_final.reshape(1, D_MODEL)
    cb = conv_b.reshape(depth, 1, D_FF)

    rope_p = _rope_tables(jnp.arange(seq))
    rope_s = _rope_tables(PAST_LEN + jnp.arange(t_len))
    tab_p = _retention_tables(QBLK)
    tab_s = _retention_tables(t_len)
    bias = _build_bias(rel_bias_table)

    feature_major = lambda c: jnp.transpose(c, (0, 1, 3, 4, 2)).reshape(depth, n_seq, -1, c.shape[2])
    cskt, csvt, cckt, ccvt = map(feature_major, (cache_swa_k, cache_swa_v, cache_ca_k, cache_ca_v))

    hp = x_prompt.reshape(batch * seq, D_MODEL)
    m_s = n_seq * t_len
    hs = x_sample.reshape(m_s, D_MODEL)

    pad_p = SUBLANES
    pad_s = (CONV_W - 1) * n_seq
    prev_p = jnp.zeros((batch, pad_p, D_FF), F32)
    tm_p = 512
    tiles_per_seq = seq // tm_p

    stack = lambda v: jnp.stack(v, axis=0)
    outs_p = [[] for _ in range(6)]
    outs_s = [[] for _ in range(6)]
    for l in range(depth):
        last = l == depth - 1
        ret, swa, ca = _inproj(hs, g_mix, w_in_b, l, tm=m_s // 2)
        oa, ob, oc, s_new = _mixer_sample(ret, swa, ca, attn_sinks[l], tab_s, rope_s, cskt, csvt,
                                          cckt, ccvt, state_ret, bias, l, n_seq, t_len, n_blk=8)
        time_major = lambda a: jnp.transpose(a.reshape(n_seq, t_len, -1), (1, 0, 2)).reshape(m_s, -1)
        prev_s = jnp.transpose(state_ffn_conv[l], (1, 0, 2)).reshape(1, pad_s, D_FF)
        hs_tm, conv_st = _outproj_ffn(time_major(hs), time_major(oa), time_major(ob), time_major(oc),
                                      w_out_b, g_ffn, w_gate_b, w_up_b, w_down_b, conv_w, cb, prev_s,
                                      l, tm=m_s, tf=256, shift=n_seq, tiles_per_seq=1,
                                      final_gain=g_fin if last else None)
        hs = jnp.transpose(hs_tm.reshape(t_len, n_seq, D_MODEL), (1, 0, 2)).reshape(m_s, D_MODEL)
        swa3 = swa.reshape(n_seq, t_len, SWA_COLS)
        ca3 = ca.reshape(n_seq, t_len, CA_COLS)
        outs_s[0].append(swa3[:, :, SWA_QW:SWA_QW + SWA_KW])
        outs_s[1].append(swa3[:, :, SWA_QW + SWA_KW:])
        outs_s[2].append(ca3[:, :, CA_W:2 * CA_W])
        outs_s[3].append(ca3[:, :, 2 * CA_W:])
        outs_s[4].append(s_new)
        outs_s[5].append(jnp.transpose(conv_st.reshape(CONV_W - 1, n_seq, D_FF), (1, 0, 2)))

        ret, swa, ca = _inproj(hp, g_mix, w_in_b, l, tm=512)
        oa, s_ret = _ret_prompt(ret, batch, seq, tab_p, rope_p, n_blk=8)
        ob, *swa_tail = _swa_prompt(swa, attn_sinks[l], batch, seq)
        oc, *ca_tail = _ca_prompt(ca, bias, l, batch, seq)
        roll = (cckt, ccvt, stack(outs_s[2]), stack(outs_s[3])) if last else None
        hp, conv_st, *ca_rolled = _outproj_ffn(
            hp, oa, ob, oc, w_out_b, g_ffn, w_gate_b, w_up_b, w_down_b, conv_w, cb, prev_p, l, tm=tm_p,
            tf=256, shift=1, tiles_per_seq=tiles_per_seq, final_gain=g_fin if last else None,
            roll=roll)
        for lst, val in zip(outs_p[:4], swa_tail + ca_tail):
            lst.append(val)
        outs_p[4].append(s_ret)
        outs_p[5].append(conv_st[:, pad_p - (CONV_W - 1):, :])

    y_prompt = hp.reshape(batch, seq, D_MODEL)
    y_sample = hs.reshape(n_seq, t_len, D_MODEL)
    row_major = lambda c, heads: jnp.transpose(
        c.reshape(depth, c.shape[1], heads, HEAD_DIM, -1), (0, 1, 4, 2, 3))
    swa_k_p = row_major(stack(outs_p[0]), KV_SWA)
    swa_v_p = row_major(stack(outs_p[1]), KV_SWA)
    ca_k_p = row_major(stack(outs_p[2]), H_CA)
    ca_v_p = row_major(stack(outs_p[3]), H_CA)
    ret_p = stack(outs_p[4])
    conv_p = stack(outs_p[5])

    swa_kt, swa_vt = _cache_roll(cskt, csvt, stack(outs_s[0]), stack(outs_s[1]), n_blk=8)
    ca_kt, ca_vt = ca_rolled
    swa_k_s = row_major(swa_kt, KV_SWA)
    swa_v_s = row_major(swa_vt, KV_SWA)
    ca_k_s = row_major(ca_kt, H_CA)
    ca_v_s = row_major(ca_vt, H_CA)
    ret_s = stack(outs_s[4])
    conv_s = stack(outs_s[5])
    return (y_prompt, y_sample, swa_k_p, swa_v_p, ca_k_p, ca_v_p, ret_p, conv_p,
            swa_k_s, swa_v_s, ca_k_s, ca_v_s, ret_s, conv_s)
```

```python
import functools

import jax
import jax.numpy as jnp
from jax import lax
from jax.experimental import pallas as pl
from jax.experimental.pallas import tpu as pltpu

F32 = jnp.float32
BF16 = jnp.bfloat16

D_MODEL = 1024
CHUNK = 64
HEAD_DIM = 64
H_RET = 6
H_SWA = 6
KV_SWA = 2
G_SWA = H_SWA // KV_SWA
H_CA = 4
SWA_ROWS = 128
CA_ROWS = 512
REL_CLIP = 128
N_REL = 2 * REL_CLIP + 1
D_FF = 2816
CONV_W = 3
EPS = 1e-6
ROPE_BASE = 10000.0
NEG_INF = -1e30
PAST_LEN = 4096

RET_W = H_RET * HEAD_DIM
RET_COLS = 4 * RET_W
SWA_QW = H_SWA * HEAD_DIM
SWA_KW = KV_SWA * HEAD_DIM
SWA_COLS = SWA_QW + 2 * SWA_KW
CA_W = H_CA * HEAD_DIM
CA_COLS = 3 * CA_W
IN_COLS = RET_COLS + SWA_COLS + CA_COLS

LANES = 128
SUBLANES = 8
VMEM_LIMIT_BYTES = 56 * 1024 * 1024

QBLK = 128
SWA_QSTEP = 8
CA_QSTEP = 16
SWA_WIN = QBLK + SWA_ROWS
CA_WIN = QBLK + CA_ROWS
CA_WIN_TILES = CA_WIN // LANES
CA_EXT_TILES = (CA_WIN + CA_ROWS) // LANES
BIAS_ROW = CA_EXT_TILES * LANES + LANES


def _params(semantics):
    return pltpu.CompilerParams(dimension_semantics=semantics,
                                vmem_limit_bytes=VMEM_LIMIT_BYTES)


def _rms(x, gain):
    ms = jnp.mean(x * x, axis=-1, keepdims=True)
    return (x * lax.rsqrt(ms + EPS)) * gain


def _dot(a, b):
    return jnp.dot(a, b, preferred_element_type=F32)


def _dot_nt(a, b):
    return lax.dot_general(a, b, (((1,), (1,)), ((), ())), preferred_element_type=F32)


def _dot_tn(a, b):
    return lax.dot_general(a, b, (((0,), (0,)), ((), ())), preferred_element_type=F32)


def _inproj_kernel(x_ref, g_ref, w_ref, ret_ref, swa_ref, ca_ref):
    h = _rms(x_ref[...], g_ref[...]).astype(BF16)
    ret_ref[...] = _dot(h, w_ref[:, 0:RET_COLS])
    swa_ref[...] = _dot(h, w_ref[:, RET_COLS:RET_COLS + SWA_COLS])
    ca_ref[...] = _dot(h, w_ref[:, RET_COLS + SWA_COLS:IN_COLS])


def _inproj(x, gain, w_in, layer, w_layer, tm):
    m = x.shape[0]
    row = lambda i: (i, 0)
    return pl.pallas_call(
        _inproj_kernel,
        grid=(m // tm,),
        in_specs=[
            pl.BlockSpec((tm, D_MODEL), row),
            pl.BlockSpec((None, 1, D_MODEL), lambda i: (layer, 0, 0)),
            pl.BlockSpec((None, D_MODEL, IN_COLS), lambda i: (w_layer, 0, 0)),
        ],
        out_specs=[
            pl.BlockSpec((tm, RET_COLS), row),
            pl.BlockSpec((tm, SWA_COLS), row),
            pl.BlockSpec((tm, CA_COLS), row),
        ],
        out_shape=[
            jax.ShapeDtypeStruct((m, RET_COLS), F32),
            jax.ShapeDtypeStruct((m, SWA_COLS), F32),
            jax.ShapeDtypeStruct((m, CA_COLS), F32),
        ],
        compiler_params=_params(("parallel",)),
        name="inproj",
    )(x, gain, w_in)


def _ret_log_decay():
    return jnp.log(1.0 - 2.0 ** (-5.0 - jnp.arange(H_RET, dtype=F32)))


def _rope_tables(pos):
    half = HEAD_DIM // 2
    inv = 1.0 / (ROPE_BASE ** (jnp.arange(half, dtype=F32) / half))
    ang = pos.astype(F32)[:, None] * inv[None, :]
    cos = jnp.cos(ang)
    sin = jnp.sin(ang)
    cos_h = jnp.concatenate([cos, cos], axis=-1)
    sin_h = jnp.concatenate([-sin, sin], axis=-1)
    return jnp.tile(cos_h, (1, H_RET)), jnp.tile(sin_h, (1, H_RET))


def _retention_tables(t_len):
    log_g = _ret_log_decay()
    t = jnp.arange(t_len, dtype=F32)
    diff = t[:, None] - t[None, :]
    decay = jnp.where(diff >= 0, jnp.exp(jnp.maximum(diff, 0.0)[None] * log_g[:, None, None]), 0.0)
    q_decay = jnp.exp((t + 1.0)[:, None] * log_g[None, :])
    k_decay = jnp.exp((t_len - 1.0 - t)[:, None] * log_g[None, :])
    head = jnp.arange(LANES) // HEAD_DIM
    same_head = (head[:, None] == head[None, :]).astype(F32)
    blk = jnp.exp(t_len * log_g).reshape(RET_W // LANES, LANES // HEAD_DIM)
    state_decay = jnp.broadcast_to(blk[:, head][:, :, None], (RET_W // LANES, LANES, LANES))
    return (decay.reshape(H_RET * t_len, t_len), jnp.repeat(q_decay, HEAD_DIM, axis=1),
            jnp.repeat(k_decay, HEAD_DIM, axis=1), (same_head / HEAD_DIM).astype(BF16), same_head,
            state_decay)


def _rope(x, cos, sin):
    nb, t_len, n = x.shape
    x2 = x.reshape(nb * t_len, n)
    lane = lax.broadcasted_iota(jnp.int32, x2.shape, 1)
    first_half = (lane & (HEAD_DIM - 1)) < (HEAD_DIM // 2)
    partner = jnp.where(first_half,
                        pltpu.roll(x2, n - HEAD_DIM // 2, axis=1),
                        pltpu.roll(x2, HEAD_DIM // 2, axis=1))
    return x * cos + partner.reshape(x.shape) * sin


def _group_mean(x, head_avg, split=True):
    nb, t_len, n = x.shape
    hi = x.astype(BF16).reshape(nb * t_len, n)
    if not split:
        return _dot(hi, head_avg).reshape(x.shape)
    lo = (x - x.astype(BF16).astype(F32)).astype(BF16).reshape(nb * t_len, n)
    r = _dot(jnp.concatenate([hi, lo], axis=0), head_avg)
    return (r[:nb * t_len] + r[nb * t_len:]).reshape(x.shape)


def _bdot(spec, a, b):
    return jnp.einsum(spec, a, b, preferred_element_type=F32)


def _retention_block(q, k, v, gate, cos, sin, dec, q_decay, k_decay, head_avg, same_head,
                     state_decay, state):
    nb, t_len, _ = q.shape
    n_cols = RET_W // LANES
    q = _rope(q, cos, sin)
    k = _rope(k, cos, sin) * (HEAD_DIM ** -0.5)
    lower_half = lax.broadcasted_iota(jnp.int32, (t_len, LANES), 1) < HEAD_DIM
    to_cols = lambda x: jnp.concatenate([x[:, :, p * LANES:(p + 1) * LANES] for p in range(n_cols)],
                                        axis=0)
    per_col = lambda x, tabs: jnp.concatenate([x[p * nb:(p + 1) * nb] * tabs[p] for p in range(n_cols)],
                                              axis=0)
    col_tabs = lambda tab: [tab[:, p * LANES:(p + 1) * LANES] for p in range(n_cols)]
    qc = to_cols(q)
    kc = to_cols(k).astype(BF16)
    kdc = to_cols(k * k_decay).astype(BF16)
    vc = to_cols(v).astype(BF16)
    st = jnp.concatenate(state, axis=0)
    q_stack = jnp.concatenate([jnp.where(lower_half, qc, 0.0).astype(BF16),
                               jnp.where(lower_half, 0.0, qc).astype(BF16)], axis=1)
    scores = per_col(_bdot('bqd,bkd->bqk', q_stack, kc),
                     [dec[2 * p * t_len:(2 * p + 2) * t_len] for p in range(n_cols)])
    inner_all = _bdot('bqk,bkd->bqd', scores.astype(BF16), vc)
    inner = jnp.where(lower_half, inner_all[:, :t_len], inner_all[:, t_len:])
    cross = per_col(_bdot('bqd,bde->bqe', qc.astype(BF16), st.astype(BF16)), col_tabs(q_decay))
    o = inner + cross
    d = o - _group_mean(o, head_avg)
    var = _group_mean(d * d, head_avg, split=False)
    on_cols = d * lax.rsqrt(var + EPS)
    new_st = (per_col(st, [state_decay[p] for p in range(n_cols)])
              + same_head * _bdot('bkd,bke->bde', kdc, vc))
    on = jnp.concatenate([on_cols[p * nb:(p + 1) * nb] for p in range(n_cols)], axis=-1)
    new_state = [new_st[p * nb:(p + 1) * nb] for p in range(n_cols)]
    return on * (gate * jax.nn.sigmoid(gate)), new_state


def _ret_prompt_kernel(q_ref, k_ref, v_ref, g_ref, cos_ref, sin_ref, dec_ref, qd_ref, kd_ref,
                       avg_ref, same_ref, sdec_ref, o_ref, s_ref, state_scr):
    t = pl.program_id(1)
    n_cols = RET_W // LANES

    @pl.when(t == 0)
    def _():
        state_scr[...] = jnp.zeros_like(state_scr)

    out, new_state = _retention_block(q_ref[...], k_ref[...], v_ref[...], g_ref[...], cos_ref[...],
                                      sin_ref[...], dec_ref[...], qd_ref[...], kd_ref[...],
                                      avg_ref[...], same_ref[...], sdec_ref[...],
                                      [state_scr[p] for p in range(n_cols)])
    o_ref[...] = out.astype(o_ref.dtype)
    for p in range(n_cols):
        state_scr[p] = new_state[p]

    @pl.when(t == pl.num_programs(1) - 1)
    def _():
        for p in range(n_cols):
            s_ref[:, 2 * p] = state_scr[p, :, 0:HEAD_DIM, 0:HEAD_DIM]
            s_ref[:, 2 * p + 1] = state_scr[p, :, HEAD_DIM:, HEAD_DIM:]


def _ret_prompt(ret, batch, seq, tables, rope, n_blk):
    tb = QBLK
    cos, sin = rope
    ret3 = ret.reshape(batch, seq, RET_COLS)
    col = lambda c: pl.BlockSpec((n_blk, tb, RET_W), lambda b, t: (b, t, c))
    pos = pl.BlockSpec((tb, RET_W), lambda b, t: (t, 0))
    whole = lambda a: pl.BlockSpec(a.shape, lambda b, t: (0,) * a.ndim)
    out, s_fin = pl.pallas_call(
        _ret_prompt_kernel,
        grid=(batch // n_blk, seq // tb),
        in_specs=[col(0), col(1), col(2), col(3), pos, pos] + [whole(a) for a in tables],
        out_specs=[
            pl.BlockSpec((n_blk, tb, RET_W), lambda b, t: (b, t, 0)),
            pl.BlockSpec((n_blk, H_RET, HEAD_DIM, HEAD_DIM), lambda b, t: (b, 0, 0, 0)),
        ],
        out_shape=[
            jax.ShapeDtypeStruct((batch, seq, RET_W), BF16),
            jax.ShapeDtypeStruct((batch, H_RET, HEAD_DIM, HEAD_DIM), F32),
        ],
        scratch_shapes=[pltpu.VMEM((RET_W // LANES, n_blk, LANES, LANES), F32)],
        compiler_params=_params(("parallel", "arbitrary")),
        name="ret_prompt",
    )(ret3, ret3, ret3, ret3, cos, sin, *tables)
    return out.reshape(batch * seq, RET_W), s_fin


def _emit_cache_tail(i, k_ref, v_ref, kt_ref, vt_ref):
    @pl.when(i == pl.num_programs(1) - 1)
    def _():
        n_rows = kt_ref.shape[-1]
        seq = k_ref.shape[0]
        kt_ref[...] = k_ref[seq - n_rows:seq, :].T
        vt_ref[...] = v_ref[seq - n_rows:seq, :].T


def _pair_softmax_pv(q_pair, k_halves, v_halves, bias_halves, lower_half, sinks=None):
    outs = []
    for half in range(2):
        q_h = jnp.where(lower_half, q_pair, 0) if half == 0 else jnp.where(lower_half, 0, q_pair)
        s = _dot_nt(q_h, k_halves[half]) + bias_halves[half]
        m = jnp.max(s, axis=-1, keepdims=True)
        if sinks is not None:
            m = jnp.maximum(m, sinks[half])
        r = _dot(jnp.exp(s - m).astype(BF16), v_halves[half])
        denom = r[:, LANES:]
        if sinks is not None:
            denom = denom + jnp.exp(sinks[half] - m)
        outs.append(r[:, :LANES] / denom)
    return jnp.where(lower_half, outs[0], outs[1])


def _swa_prompt_kernel(sink_ref, q_ref, k_ref, v_ref, o_ref, kt_ref, vt_ref, k_scr, v_scr):
    i = pl.program_id(1)

    @pl.when(i == 0)
    def _():
        k = k_ref[...]
        v = v_ref[...]
        ones = jnp.ones((k.shape[0], LANES), BF16)
        for x, (kx, vx) in enumerate(((k, v), (pltpu.roll(k, HEAD_DIM, axis=1),
                                               pltpu.roll(v, HEAD_DIM, axis=1)))):
            k_scr[x] = kx.astype(BF16)
            v_scr[x, :, 0:LANES] = vx.astype(BF16)
            v_scr[x, :, LANES:] = ones

    row = lax.broadcasted_iota(jnp.int32, (QBLK, SWA_WIN), 0)
    col = lax.broadcasted_iota(jnp.int32, (QBLK, SWA_WIN), 1)
    n_prev = SWA_ROWS // CHUNK
    lower_half = lax.broadcasted_iota(jnp.int32, (QBLK, LANES), 1) < HEAD_DIM
    for u in range(SWA_QSTEP):
        blk = i * SWA_QSTEP + u
        rows = slice(u * QBLK, (u + 1) * QBLK)
        start = pl.multiple_of(jnp.maximum(blk * QBLK - SWA_ROWS, 0), QBLK)
        win = pl.ds(start, SWA_WIN)
        q = (q_ref[rows, :] * (HEAD_DIM ** -0.5)).astype(BF16)
        behind = (blk * (QBLK // CHUNK) + row // CHUNK) - (start // CHUNK + col // CHUNK)
        mask = jnp.where(jnp.abs(2 * behind - n_prev) <= n_prev, 0.0, NEG_INF)
        for c in range(SWA_QW // LANES):
            cols = slice(c * LANES, (c + 1) * LANES)
            heads = (2 * c, 2 * c + 1)
            swap = [(h // G_SWA) ^ half for half, h in enumerate(heads)]
            out = _pair_softmax_pv(q[:, cols], [k_scr[x, win, :] for x in swap],
                                   [v_scr[x, win, :] for x in swap], [mask, mask], lower_half,
                                   sinks=[sink_ref[h] for h in heads])
            o_ref[rows, cols] = out.astype(o_ref.dtype)

    _emit_cache_tail(i, k_ref, v_ref, kt_ref, vt_ref)


def _swa_prompt(swa, sinks, batch, seq):
    tq = SWA_QSTEP * QBLK
    nq = seq // tq
    k_col = SWA_QW // SWA_KW
    tail = pl.BlockSpec((None, SWA_KW, SWA_ROWS), lambda b, i: (b, 0, 0))
    tail_shape = jax.ShapeDtypeStruct((batch, SWA_KW, SWA_ROWS), F32)
    return pl.pallas_call(
        _swa_prompt_kernel,
        grid=(batch, nq),
        in_specs=[
            pl.BlockSpec(memory_space=pltpu.SMEM),
            pl.BlockSpec((tq, SWA_QW), lambda b, i: (b * nq + i, 0)),
            pl.BlockSpec((seq, SWA_KW), lambda b, i: (b, k_col)),
            pl.BlockSpec((seq, SWA_KW), lambda b, i: (b, k_col + 1)),
        ],
        out_specs=[pl.BlockSpec((tq, SWA_QW), lambda b, i: (b * nq + i, 0)), tail, tail],
        out_shape=[jax.ShapeDtypeStruct((batch * seq, SWA_QW), BF16), tail_shape, tail_shape],
        scratch_shapes=[
            pltpu.VMEM((2, seq, SWA_KW), BF16),
            pltpu.VMEM((2, seq, 2 * LANES), BF16),
        ],
        compiler_params=_params(("parallel", "arbitrary")),
        name="swa_prompt",
    )(sinks, swa, swa, swa)


def _bias_kernel(t_ref, o_ref):
    n_pad = t_ref.shape[-1]
    r = lax.broadcasted_iota(jnp.int32, (n_pad, BIAS_ROW), 0)
    w = lax.broadcasted_iota(jnp.int32, (n_pad, BIAS_ROW), 1)
    idx = jnp.clip(BIAS_ROW // 2 - 1 - w, -REL_CLIP, REL_CLIP) + REL_CLIP
    onehot = jnp.where(r == idx, 1.0, 0.0).astype(BF16)
    t = t_ref[...]
    t_hi = t.astype(BF16)
    rem = t - t_hi.astype(F32)
    t_mid = rem.astype(BF16)
    t_lo = (rem - t_mid.astype(F32)).astype(BF16)
    f = (_dot(t_hi, onehot) + _dot(t_mid, onehot)) + _dot(t_lo, onehot)
    ext = CA_EXT_TILES * LANES
    row = lax.broadcasted_iota(jnp.int32, (QBLK, ext), 0)
    colj = lax.broadcasted_iota(jnp.int32, (QBLK, ext), 1)
    ahead = colj // CHUNK - row // CHUNK
    n_prev = CA_ROWS // CHUNK
    valid = jnp.abs(2 * ahead - n_prev) <= n_prev
    for h in range(H_CA):
        x = jnp.broadcast_to(f[h:h + 1, :], (QBLK, BIAS_ROW))
        y = pltpu.roll(x, BIAS_ROW - (QBLK - 1), axis=1, stride=1, stride_axis=0)
        y = jnp.where(valid, y[:, 0:ext], NEG_INF)
        for c in range(CA_EXT_TILES):
            o_ref[h, c] = y[:, c * LANES:(c + 1) * LANES]


def _build_bias(rel_bias_table):
    depth = rel_bias_table.shape[0]
    n_pad = 3 * LANES
    t = jnp.transpose(rel_bias_table, (0, 2, 1))
    t = jnp.pad(t, ((0, 0), (0, SUBLANES - H_CA), (0, n_pad - N_REL)))
    return pl.pallas_call(
        _bias_kernel,
        grid=(depth,),
        in_specs=[pl.BlockSpec((None, SUBLANES, n_pad), lambda l: (l, 0, 0))],
        out_specs=pl.BlockSpec((None, H_CA, CA_EXT_TILES, QBLK, LANES), lambda l: (l, 0, 0, 0, 0)),
        out_shape=jax.ShapeDtypeStruct((depth, H_CA, CA_EXT_TILES, QBLK, LANES), F32),
        compiler_params=_params(("parallel",)),
        name="ca_bias",
    )(t)


def _ca_prompt_kernel(q_ref, k_ref, v_ref, bias_ref, o_ref, kt_ref, vt_ref, k_scr, v_scr):
    i = pl.program_id(1)
    n_pairs = CA_W // LANES

    @pl.when(i == 0)
    def _():
        k_scr[...] = k_ref[...].astype(BF16)
        for p in range(n_pairs):
            v_scr[p, :, 0:LANES] = v_ref[:, p * LANES:(p + 1) * LANES].astype(BF16)
            v_scr[p, :, LANES:] = jnp.ones((v_scr.shape[1], LANES), BF16)

    lower_half = lax.broadcasted_iota(jnp.int32, (QBLK, LANES), 1) < HEAD_DIM
    for u in range(CA_QSTEP):
        rows = slice(u * QBLK, (u + 1) * QBLK)
        ideal = (i * CA_QSTEP + u) * QBLK - CA_ROWS
        start = pl.multiple_of(jnp.maximum(ideal, 0), QBLK)
        off = (start - ideal) // LANES
        q = (q_ref[rows, :] * (HEAD_DIM ** -0.5)).astype(BF16)
        for p in range(n_pairs):
            cols = slice(p * LANES, (p + 1) * LANES)
            bias = [jnp.concatenate([bias_ref[2 * p + half, off + c] for c in range(CA_WIN_TILES)],
                                    axis=-1) for half in range(2)]
            k_win = k_scr[pl.ds(start, CA_WIN), cols]
            v_win = v_scr[p, pl.ds(start, CA_WIN), :]
            out = _pair_softmax_pv(q[:, cols], [k_win, k_win], [v_win, v_win], bias, lower_half)
            o_ref[rows, cols] = out.astype(o_ref.dtype)

    _emit_cache_tail(i, k_ref, v_ref, kt_ref, vt_ref)


def _ca_prompt(ca, bias, layer, batch, seq):
    tq = CA_QSTEP * QBLK
    nq = seq // tq
    tail = pl.BlockSpec((None, CA_W, CA_ROWS), lambda b, i: (b, 0, 0))
    tail_shape = jax.ShapeDtypeStruct((batch, CA_W, CA_ROWS), F32)
    return pl.pallas_call(
        _ca_prompt_kernel,
        grid=(batch, nq),
        in_specs=[
            pl.BlockSpec((tq, CA_W), lambda b, i: (b * nq + i, 0)),
            pl.BlockSpec((seq, CA_W), lambda b, i: (b, 1)),
            pl.BlockSpec((seq, CA_W), lambda b, i: (b, 2)),
            pl.BlockSpec((None, H_CA, CA_EXT_TILES, QBLK, LANES), lambda b, i: (layer, 0, 0, 0, 0)),
        ],
        out_specs=[pl.BlockSpec((tq, CA_W), lambda b, i: (b * nq + i, 0)), tail, tail],
        out_shape=[jax.ShapeDtypeStruct((batch * seq, CA_W), BF16), tail_shape, tail_shape],
        scratch_shapes=[
            pltpu.VMEM((seq, CA_W), BF16),
            pltpu.VMEM((CA_W // LANES, seq, 2 * LANES), BF16),
        ],
        compiler_params=_params(("parallel", "arbitrary")),
        name="ca_prompt",
    )(ca, ca, ca, bias)


def _block_diag(blocks):
    z = jnp.zeros_like(blocks[0])
    return jnp.concatenate([jnp.concatenate([b if j == i else z for j in range(len(blocks))], axis=2)
                            for i, b in enumerate(blocks)], axis=1)


def _swap_halves_rows(x):
    half = x.shape[1] // 2
    return jnp.concatenate([x[:, half:], x[:, :half]], axis=1)


def _heads_softmax_pv_cached(q_heads, kt, kn, vt, vn, bias_c, bias_n, sink=None):
    s_c = _bdot('bqd,bdk->bqk', q_heads, kt) + bias_c
    s_n = _bdot('bqd,bkd->bqk', q_heads, kn) + bias_n
    m = jnp.maximum(jnp.max(s_c, axis=-1, keepdims=True), jnp.max(s_n, axis=-1, keepdims=True))
    if sink is not None:
        m = jnp.maximum(m, sink)
    r = (_bdot('bqk,bfk->bqf', jnp.exp(s_c - m).astype(BF16), vt)
         + _bdot('bqk,bkf->bqf', jnp.exp(s_n - m).astype(BF16), vn))
    denom = r[:, :, LANES:]
    if sink is not None:
        denom = denom + jnp.exp(sink - m)
    return r[:, :, :LANES] / denom


def _mixer_sample_kernel(sink_ref, ret_ref, swa_ref, ca_ref, cos_ref, sin_ref, dec_ref, qd_ref,
                         kd_ref, avg_ref, same_ref, sdec_ref, skt_ref, svt_ref, ckt_ref, cvt_ref,
                         st_ref, bias_ref, oa_ref, ob_ref, oc_ref, snew_ref):
    n_blk, t_len, _ = ret_ref.shape
    lower_half = lax.broadcasted_iota(jnp.int32, (t_len, LANES), 1) < HEAD_DIM
    ones_rows = lambda n: jnp.ones((n_blk, LANES, n), BF16)
    ones_cols = jnp.ones((n_blk, t_len, LANES), BF16)
    ca_tiles = CA_ROWS // LANES

    col = lambda c: ret_ref[:, :, c * RET_W:(c + 1) * RET_W]
    pairs = [_block_diag([st_ref[:, 2 * p], st_ref[:, 2 * p + 1]]) for p in range(RET_W // LANES)]
    out, new_state = _retention_block(col(0), col(1), col(2), col(3), cos_ref[...], sin_ref[...],
                                      dec_ref[...], qd_ref[...], kd_ref[...], avg_ref[...],
                                      same_ref[...], sdec_ref[...], pairs)
    oa_ref[...] = out.astype(oa_ref.dtype)
    for p in range(RET_W // LANES):
        snew_ref[:, 2 * p] = new_state[p][:, 0:HEAD_DIM, 0:HEAD_DIM]
        snew_ref[:, 2 * p + 1] = new_state[p][:, HEAD_DIM:, HEAD_DIM:]

    q = (swa_ref[:, :, 0:SWA_QW] * (HEAD_DIM ** -0.5)).astype(BF16)
    k_new = swa_ref[:, :, SWA_QW:SWA_QW + SWA_KW]
    v_new = swa_ref[:, :, SWA_QW + SWA_KW:SWA_COLS]
    k_cache = skt_ref[...]
    v_cache = svt_ref[...]
    swap_lanes = lambda x: pltpu.roll(x.reshape(n_blk * t_len, SWA_KW), HEAD_DIM, axis=1).reshape(x.shape)
    kt = [k_cache.astype(BF16), _swap_halves_rows(k_cache).astype(BF16)]
    vt = [jnp.concatenate([x.astype(BF16), ones_rows(SWA_ROWS)], axis=1)
          for x in (v_cache, _swap_halves_rows(v_cache))]
    kn = [k_new.astype(BF16), swap_lanes(k_new).astype(BF16)]
    vn = [jnp.concatenate([x.astype(BF16), ones_cols], axis=2) for x in (v_new, swap_lanes(v_new))]
    own_lanes = lambda x, half: jnp.where(lower_half, x, 0) if half == 0 else jnp.where(lower_half, 0, x)
    stack = lambda parts: jnp.concatenate(parts, axis=0)
    swap = [(h // G_SWA) ^ (h % 2) for h in range(H_SWA)]
    out = _heads_softmax_pv_cached(
        stack([own_lanes(q[:, :, (h // 2) * LANES:(h // 2 + 1) * LANES], h % 2) for h in range(H_SWA)]),
        stack([kt[x] for x in swap]), stack([kn[x] for x in swap]),
        stack([vt[x] for x in swap]), stack([vn[x] for x in swap]), 0.0, 0.0,
        stack([jnp.full((n_blk, 1, 1), sink_ref[h], F32) for h in range(H_SWA)]))
    for c in range(SWA_QW // LANES):
        ob_ref[:, :, c * LANES:(c + 1) * LANES] = jnp.where(
            lower_half, out[2 * c * n_blk:(2 * c + 1) * n_blk],
            out[(2 * c + 1) * n_blk:(2 * c + 2) * n_blk]).astype(ob_ref.dtype)

    q = (ca_ref[:, :, 0:CA_W] * (HEAD_DIM ** -0.5)).astype(BF16)
    k_new = ca_ref[:, :, CA_W:2 * CA_W]
    v_new = ca_ref[:, :, 2 * CA_W:CA_COLS]
    for p in range(CA_W // LANES):
        cols = slice(p * LANES, (p + 1) * LANES)
        kt_p = ckt_ref[:, cols, :].astype(BF16)
        vt_p = jnp.concatenate([cvt_ref[:, cols, :].astype(BF16), ones_rows(CA_ROWS)], axis=1)
        kn_p = k_new[:, :, cols].astype(BF16)
        vn_p = jnp.concatenate([v_new[:, :, cols].astype(BF16), ones_cols], axis=2)
        bias_c = [jnp.concatenate([bias_ref[2 * p + half, c, 0:t_len, :] for c in range(ca_tiles)],
                                  axis=-1) for half in range(2)]
        bias_n = [bias_ref[2 * p + half, ca_tiles, 0:t_len, 0:t_len] for half in range(2)]
        outs = [_heads_softmax_pv_cached(own_lanes(q[:, :, cols], half), kt_p, kn_p, vt_p, vn_p,
                                         bias_c[half], bias_n[half]) for half in range(2)]
        oc_ref[:, :, cols] = jnp.where(lower_half, outs[0], outs[1]).astype(oc_ref.dtype)


def _mixer_sample(ret, swa, ca, sinks, tables, rope, cache_skt, cache_svt, cache_ckt, cache_cvt,
                  state_ret, bias, layer, n_seq, t_len, n_blk):
    cos, sin = rope
    whole = lambda shape: pl.BlockSpec(shape, lambda b: (0,) * len(shape))
    seq_rows = lambda width: pl.BlockSpec((n_blk, t_len, width), lambda b: (b, 0, 0))
    cache = lambda feat, rows: pl.BlockSpec((None, n_blk, feat, rows), lambda b: (layer, b, 0, 0))
    per_stream = lambda a: a.reshape(n_seq, t_len, a.shape[-1])
    oa, ob, oc, s_new = pl.pallas_call(
        _mixer_sample_kernel,
        grid=(n_seq // n_blk,),
        in_specs=[
            pl.BlockSpec(memory_space=pltpu.SMEM),
            seq_rows(RET_COLS), seq_rows(SWA_COLS), seq_rows(CA_COLS),
            whole((t_len, RET_W)), whole((t_len, RET_W)),
            *[whole(a.shape) for a in tables],
            cache(SWA_KW, SWA_ROWS), cache(SWA_KW, SWA_ROWS),
            cache(CA_W, CA_ROWS), cache(CA_W, CA_ROWS),
            pl.BlockSpec((None, n_blk, H_RET, HEAD_DIM, HEAD_DIM), lambda b: (layer, b, 0, 0, 0)),
            pl.BlockSpec((None, H_CA, CA_EXT_TILES, QBLK, LANES), lambda b: (layer, 0, 0, 0, 0)),
        ],
        out_specs=[
            seq_rows(RET_W), seq_rows(SWA_QW), seq_rows(CA_W),
            pl.BlockSpec((n_blk, H_RET, HEAD_DIM, HEAD_DIM), lambda b: (b, 0, 0, 0)),
        ],
        out_shape=[
            jax.ShapeDtypeStruct((n_seq, t_len, RET_W), BF16),
            jax.ShapeDtypeStruct((n_seq, t_len, SWA_QW), BF16),
            jax.ShapeDtypeStruct((n_seq, t_len, CA_W), BF16),
            jax.ShapeDtypeStruct((n_seq, H_RET, HEAD_DIM, HEAD_DIM), F32),
        ],
        compiler_params=_params(("parallel",)),
        name="mixer_sample",
    )(sinks, per_stream(ret), per_stream(swa), per_stream(ca), cos, sin, *tables, cache_skt,
      cache_svt, cache_ckt, cache_cvt, state_ret, bias)
    flat = lambda a: a.reshape(n_seq * t_len, a.shape[-1])
    return flat(oa), flat(ob), flat(oc), s_new


def _append_rows_feature_major(cache_t, new_rows):
    n_feat, n_rows = cache_t.shape
    t_len = new_rows.shape[0]
    rolled = pltpu.roll(cache_t, n_rows - t_len, axis=1)
    t_idx = lax.broadcasted_iota(jnp.int32, (t_len, LANES), 0)
    x_idx = lax.broadcasted_iota(jnp.int32, (t_len, LANES), 1)
    place = jnp.where(x_idx == LANES - t_len + t_idx, 1.0, 0.0).astype(BF16)
    hi = new_rows.astype(BF16)
    rem = new_rows - hi.astype(F32)
    mid = rem.astype(BF16)
    lo = (rem - mid.astype(F32)).astype(BF16)
    placed = (_dot_tn(hi, place) + _dot_tn(mid, place)) + _dot_tn(lo, place)
    lane = lax.broadcasted_iota(jnp.int32, (n_feat, LANES), 1)
    last = jnp.where(lane < LANES - t_len, rolled[:, n_rows - LANES:], placed)
    if n_rows == LANES:
        return last
    return jnp.concatenate([rolled[:, :n_rows - LANES], last], axis=1)


def _ffn_kernel(*refs, tm, tf, pad, shift, tiles_per_seq, final_norm, n_rolls, n_casts):
    refs = list(refs)
    (x_ref, a_ref, b_ref, c_ref, wo_ref, gain_ref, wg_ref, wu_ref, wd_ref, cw_ref, cb_ref,
     prev_ref) = refs[:12]
    del refs[:12]
    gf_ref = refs.pop(0) if final_norm else None
    roll_in = [refs.pop(0) for _ in range(4 * n_rolls)]
    cast_in = [refs.pop(0) for _ in range(n_casts)]
    o_ref, st_ref = refs.pop(0), refs.pop(0)
    roll_out = [refs.pop(0) for _ in range(2 * n_rolls)]
    cast_out = [refs.pop(0) for _ in range(n_casts)]
    hn_scr, act_scr, hist_scr = refs
    for g in range(n_rolls):
        ck_ref, cv_ref, nk_ref, nv_ref = roll_in[4 * g:4 * g + 4]
        for l in range(ck_ref.shape[0]):
            roll_out[2 * g][l, 0] = _append_rows_feature_major(ck_ref[l, 0], nk_ref[l, 0])
            roll_out[2 * g + 1][l, 0] = _append_rows_feature_major(cv_ref[l, 0], nv_ref[l, 0])
    for src_ref, dst_ref in zip(cast_in, cast_out):
        dst_ref[...] = src_ref[...].astype(BF16)
    x = x_ref[...] + _dot(a_ref[...], wo_ref[0:RET_W, :])
    x = x + _dot(b_ref[...], wo_ref[RET_W:RET_W + SWA_QW, :])
    x = x + _dot(c_ref[...], wo_ref[RET_W + SWA_QW:D_MODEL, :])
    o_ref[...] = x
    hn_scr[...] = _rms(x, gain_ref[...]).astype(BF16)
    if tiles_per_seq == 1:
        hist = prev_ref
    else:
        hist = hist_scr

        @pl.when((pl.program_id(0) % tiles_per_seq) == 0)
        def _():
            hist_scr[...] = prev_ref[...]

    for c in range(D_FF // tf):
        cols = slice(c * tf, (c + 1) * tf)
        hn = hn_scr[...]
        g = _dot(hn, wg_ref[:, cols])
        u = _dot(hn, wu_ref[:, cols])
        g_ext = jnp.concatenate([hist[:, cols], g], axis=0)
        gc = cb_ref[:, cols] + cw_ref[0:1, cols] * pltpu.roll(g_ext, 2 * shift, axis=0)[pad:]
        gc = gc + cw_ref[1:2, cols] * pltpu.roll(g_ext, shift, axis=0)[pad:]
        gc = gc + cw_ref[2:3, cols] * g
        act_scr[:, cols] = (jax.nn.gelu(gc) * u).astype(BF16)
        tail = g_ext[tm:tm + pad]
        if tiles_per_seq > 1:
            hist_scr[:, cols] = tail
        st_ref[:, cols] = tail

    y = o_ref[...] + _dot(act_scr[...], wd_ref[...])
    if final_norm:
        y = _rms(y, gf_ref[...])
    o_ref[...] = y


def _outproj_ffn(x, oa, ob, oc, w_out, gain, w_gate, w_up, w_down, conv_w, conv_b, prev, layer,
                 w_layer, tm, tf, shift, tiles_per_seq, final_gain=None, rolls=(), casts=()):
    m = x.shape[0]
    n_groups, pad, _ = prev.shape
    final_norm = final_gain is not None
    resident = lambda shape, at=layer: pl.BlockSpec((None,) + shape, lambda i: (at, 0, 0),
                                                    pipeline_mode=pl.Buffered(1))
    weight = lambda shape: resident(shape, w_layer)
    state = pl.BlockSpec((None, pad, D_FF), lambda i: (i // tiles_per_seq, 0, 0))
    rows = lambda width: pl.BlockSpec((tm, width), lambda i: (i, 0))
    in_specs = [
        rows(D_MODEL), rows(RET_W), rows(SWA_QW), rows(CA_W),
        weight((D_MODEL, D_MODEL)),
        resident((1, D_MODEL)),
        weight((D_MODEL, D_FF)), weight((D_MODEL, D_FF)), weight((D_FF, D_MODEL)),
        resident((CONV_W, D_FF)), resident((1, D_FF)),
        state,
    ]
    args = [x, oa, ob, oc, w_out, gain, w_gate, w_up, w_down, conv_w, conv_b, prev]
    if final_norm:
        in_specs.append(pl.BlockSpec((1, D_MODEL), lambda i: (0, 0)))
        args.append(final_gain)
    out_specs = [rows(D_MODEL), state]
    out_shape = [jax.ShapeDtypeStruct((m, D_MODEL), F32),
                 jax.ShapeDtypeStruct((n_groups, pad, D_FF), F32)]
    n_steps = m // tm
    roll_outs = []
    for roll in rolls:
        depth, n_seq, n_feat, n_rows = roll[0].shape
        assert n_seq == n_steps
        cache = pl.BlockSpec((depth, 1, n_feat, n_rows), lambda i: (0, i, 0, 0))
        new = pl.BlockSpec((depth, 1, roll[2].shape[2], n_feat), lambda i: (0, i, 0, 0))
        in_specs += [cache, cache, new, new]
        args += list(roll)
        roll_outs += [(cache, jax.ShapeDtypeStruct(roll[0].shape, F32))] * 2
    cast_outs = []
    for w, cast_layer in casts:
        _, n_rows, n_cols = w.shape
        slab = min(d for d in range(16, n_rows + 1, 16)
                   if n_rows % d == 0 and n_steps % (n_rows // d) == 0)
        per_slab = n_steps // (n_rows // slab)
        in_specs.append(pl.BlockSpec((None, slab, n_cols),
                                     lambda i, at=cast_layer, per_slab=per_slab: (at, i // per_slab, 0)))
        args.append(w)
        cast_outs.append((pl.BlockSpec((None, slab, n_cols),
                                       lambda i, per_slab=per_slab: (0, i // per_slab, 0)),
                          jax.ShapeDtypeStruct((1, n_rows, n_cols), BF16)))
    for spec, shape in roll_outs + cast_outs:
        out_specs.append(spec)
        out_shape.append(shape)
    return pl.pallas_call(
        functools.partial(_ffn_kernel, tm=tm, tf=tf, pad=pad, shift=shift,
                          tiles_per_seq=tiles_per_seq, final_norm=final_norm,
                          n_rolls=len(rolls), n_casts=len(casts)),
        grid=(n_steps,),
        in_specs=in_specs,
        out_specs=out_specs,
        out_shape=out_shape,
        scratch_shapes=[
            pltpu.VMEM((tm, D_MODEL), BF16),
            pltpu.VMEM((tm, D_FF), BF16),
            pltpu.VMEM((pad, D_FF), F32),
        ],
        compiler_params=_params(("arbitrary",)),
        name="outproj_ffn",
    )(*args)


def kernel(x_prompt, x_sample, cache_swa_k, cache_swa_v, cache_ca_k, cache_ca_v, state_ret,
           state_ffn_conv, w_in, w_out, attn_sinks, rel_bias_table, norm_mix, norm_ffn, w_gate,
           w_up, conv_w, conv_b, w_down, norm_final):
    batch, seq, _ = x_prompt.shape
    n_seq, t_len, _ = x_sample.shape
    depth = w_in.shape[0]
    assert seq % (max(SWA_QSTEP, CA_QSTEP) * QBLK) == 0 and seq >= CA_WIN and t_len % SUBLANES == 0
    assert cache_swa_k.shape[2] == SWA_ROWS and cache_ca_k.shape[2] == CA_ROWS

    weights = (w_in, w_out, w_gate, w_up, w_down)
    w_in_b, w_out_b, w_gate_b, w_up_b, w_down_b = [w[0:1].astype(BF16) for w in weights]
    g_mix = norm_mix.reshape(depth, 1, D_MODEL)
    g_ffn = norm_ffn.reshape(depth, 1, D_MODEL)
    g_fin = norm_final.reshape(1, D_MODEL)
    cb = conv_b.reshape(depth, 1, D_FF)

    rope_p = _rope_tables(jnp.arange(seq))
    rope_s = _rope_tables(PAST_LEN + jnp.arange(t_len))
    tab_p = _retention_tables(QBLK)
    tab_s = _retention_tables(t_len)
    bias = _build_bias(rel_bias_table)

    feature_major = lambda c: jnp.transpose(c, (0, 1, 3, 4, 2)).reshape(depth, n_seq, -1, c.shape[2])
    cskt, csvt, cckt, ccvt = map(feature_major, (cache_swa_k, cache_swa_v, cache_ca_k, cache_ca_v))

    hp = x_prompt.reshape(batch * seq, D_MODEL)
    m_s = n_seq * t_len
    hs = x_sample.reshape(m_s, D_MODEL)

    pad_p = SUBLANES
    pad_s = (CONV_W - 1) * n_seq
    prev_p = jnp.zeros((batch, pad_p, D_FF), F32)
    tm_p = 512
    tiles_per_seq = seq // tm_p

    stack = lambda v: jnp.stack(v, axis=0)
    outs_p = [[] for _ in range(6)]
    outs_s = [[] for _ in range(6)]
    for l in range(depth):
        last = l == depth - 1
        ret, swa, ca = _inproj(hs, g_mix, w_in_b, l, 0, tm=m_s // 2)
        oa, ob, oc, s_new = _mixer_sample(ret, swa, ca, attn_sinks[l], tab_s, rope_s, cskt, csvt,
                                          cckt, ccvt, state_ret, bias, l, n_seq, t_len, n_blk=8)
        time_major = lambda a: jnp.transpose(a.reshape(n_seq, t_len, -1), (1, 0, 2)).reshape(m_s, -1)
        prev_s = jnp.transpose(state_ffn_conv[l], (1, 0, 2)).reshape(1, pad_s, D_FF)
        hs_tm, conv_st = _outproj_ffn(time_major(hs), time_major(oa), time_major(ob), time_major(oc),
                                      w_out_b, g_ffn, w_gate_b, w_up_b, w_down_b, conv_w, cb, prev_s,
                                      l, 0, tm=m_s, tf=256, shift=n_seq, tiles_per_seq=1,
                                      final_gain=g_fin if last else None)
        hs = jnp.transpose(hs_tm.reshape(t_len, n_seq, D_MODEL), (1, 0, 2)).reshape(m_s, D_MODEL)
        swa3 = swa.reshape(n_seq, t_len, SWA_COLS)
        ca3 = ca.reshape(n_seq, t_len, CA_COLS)
        outs_s[0].append(swa3[:, :, SWA_QW:SWA_QW + SWA_KW])
        outs_s[1].append(swa3[:, :, SWA_QW + SWA_KW:])
        outs_s[2].append(ca3[:, :, CA_W:2 * CA_W])
        outs_s[3].append(ca3[:, :, 2 * CA_W:])
        outs_s[4].append(s_new)
        outs_s[5].append(jnp.transpose(conv_st.reshape(CONV_W - 1, n_seq, D_FF), (1, 0, 2)))

        ret, swa, ca = _inproj(hp, g_mix, w_in_b, l, 0, tm=512)
        oa, s_ret = _ret_prompt(ret, batch, seq, tab_p, rope_p, n_blk=8)
        ob, *swa_tail = _swa_prompt(swa, attn_sinks[l], batch, seq)
        oc, *ca_tail = _ca_prompt(ca, bias, l, batch, seq)
        rolls = [(cskt, csvt, stack(outs_s[0]), stack(outs_s[1])),
                 (cckt, ccvt, stack(outs_s[2]), stack(outs_s[3]))] if last else []
        casts = [] if last else [(w, l + 1) for w in weights]
        hp, conv_st, *riders = _outproj_ffn(
            hp, oa, ob, oc, w_out_b, g_ffn, w_gate_b, w_up_b, w_down_b, conv_w, cb, prev_p, l, 0,
            tm=tm_p, tf=256, shift=1, tiles_per_seq=tiles_per_seq,
            final_gain=g_fin if last else None, rolls=rolls, casts=casts)
        if not last:
            w_in_b, w_out_b, w_gate_b, w_up_b, w_down_b = riders
        for lst, val in zip(outs_p[:4], swa_tail + ca_tail):
            lst.append(val)
        outs_p[4].append(s_ret)
        outs_p[5].append(conv_st[:, pad_p - (CONV_W - 1):, :])

    y_prompt = hp.reshape(batch, seq, D_MODEL)
    y_sample = hs.reshape(n_seq, t_len, D_MODEL)
    row_major = lambda c, heads: jnp.transpose(
        c.reshape(depth, c.shape[1], heads, HEAD_DIM, -1), (0, 1, 4, 2, 3))
    swa_k_p = row_major(stack(outs_p[0]), KV_SWA)
    swa_v_p = row_major(stack(outs_p[1]), KV_SWA)
    ca_k_p = row_major(stack(outs_p[2]), H_CA)
    ca_v_p = row_major(stack(outs_p[3]), H_CA)
    ret_p = stack(outs_p[4])
    conv_p = stack(outs_p[5])

    swa_kt, swa_vt, ca_kt, ca_vt = riders
    swa_k_s = row_major(swa_kt, KV_SWA)
    swa_v_s = row_major(swa_vt, KV_SWA)
    ca_k_s = row_major(ca_kt, H_CA)
    ca_v_s = row_major(ca_vt, H_CA)
    ret_s = stack(outs_s[4])
    conv_s = stack(outs_s[5])
    return (y_prompt, y_sample, swa_k_p, swa_v_p, ca_k_p, ca_v_p, ret_p, conv_p,
            swa_k_s, swa_v_s, ca_k_s, ca_v_s, ret_s, conv_s)
```

```python
import functools

import jax
import jax.numpy as jnp
from jax import lax
from jax.experimental import pallas as pl
from jax.experimental.pallas import tpu as pltpu

F32 = jnp.float32
BF16 = jnp.bfloat16

D_MODEL = 1024
CHUNK = 64
HEAD_DIM = 64
H_RET = 6
H_SWA = 6
KV_SWA = 2
G_SWA = H_SWA // KV_SWA
H_CA = 4
SWA_ROWS = 128
CA_ROWS = 512
REL_CLIP = 128
N_REL = 2 * REL_CLIP + 1
D_FF = 2816
CONV_W = 3
EPS = 1e-6
ROPE_BASE = 10000.0
NEG_INF = -1e30
PAST_LEN = 4096

RET_W = H_RET * HEAD_DIM
RET_COLS = 4 * RET_W
SWA_QW = H_SWA * HEAD_DIM
SWA_KW = KV_SWA * HEAD_DIM
SWA_COLS = SWA_QW + 2 * SWA_KW
CA_W = H_CA * HEAD_DIM
CA_COLS = 3 * CA_W
IN_COLS = RET_COLS + SWA_COLS + CA_COLS

LANES = 128
SUBLANES = 8
VMEM_LIMIT_BYTES = 56 * 1024 * 1024

QBLK = 128
SWA_QSTEP = 8
CA_QSTEP = 16
SWA_WIN = QBLK + SWA_ROWS
CA_WIN = QBLK + CA_ROWS
CA_WIN_TILES = CA_WIN // LANES
CA_EXT_TILES = (CA_WIN + CA_ROWS) // LANES
BIAS_ROW = CA_EXT_TILES * LANES + LANES


def _params(semantics):
    return pltpu.CompilerParams(dimension_semantics=semantics,
                                vmem_limit_bytes=VMEM_LIMIT_BYTES)


def _rms(x, gain):
    ms = jnp.mean(x * x, axis=-1, keepdims=True)
    return (x * lax.rsqrt(ms + EPS)) * gain


def _dot(a, b):
    return jnp.dot(a, b, preferred_element_type=F32)


def _dot_nt(a, b):
    return lax.dot_general(a, b, (((1,), (1,)), ((), ())), preferred_element_type=F32)


def _dot_tn(a, b):
    return lax.dot_general(a, b, (((0,), (0,)), ((), ())), preferred_element_type=F32)


def _inproj_kernel(x_ref, g_ref, w_ref, ret_ref, swa_ref, ca_ref):
    h = _rms(x_ref[...], g_ref[...]).astype(BF16)
    ret_ref[...] = _dot(h, w_ref[:, 0:RET_COLS])
    swa_ref[...] = _dot(h, w_ref[:, RET_COLS:RET_COLS + SWA_COLS])
    ca_ref[...] = _dot(h, w_ref[:, RET_COLS + SWA_COLS:IN_COLS])


def _inproj(x, gain, w_in, layer, w_layer, tm):
    m = x.shape[0]
    row = lambda i: (i, 0)
    return pl.pallas_call(
        _inproj_kernel,
        grid=(m // tm,),
        in_specs=[
            pl.BlockSpec((tm, D_MODEL), row),
            pl.BlockSpec((None, 1, D_MODEL), lambda i: (layer, 0, 0)),
            pl.BlockSpec((None, D_MODEL, IN_COLS), lambda i: (w_layer, 0, 0)),
        ],
        out_specs=[
            pl.BlockSpec((tm, RET_COLS), row),
            pl.BlockSpec((tm, SWA_COLS), row),
            pl.BlockSpec((tm, CA_COLS), row),
        ],
        out_shape=[
            jax.ShapeDtypeStruct((m, RET_COLS), F32),
            jax.ShapeDtypeStruct((m, SWA_COLS), F32),
            jax.ShapeDtypeStruct((m, CA_COLS), F32),
        ],
        compiler_params=_params(("parallel",)),
        name="inproj",
    )(x, gain, w_in)


def _ret_log_decay():
    return jnp.log(1.0 - 2.0 ** (-5.0 - jnp.arange(H_RET, dtype=F32)))


def _rope_tables(pos):
    half = HEAD_DIM // 2
    inv = 1.0 / (ROPE_BASE ** (jnp.arange(half, dtype=F32) / half))
    ang = pos.astype(F32)[:, None] * inv[None, :]
    cos = jnp.cos(ang)
    sin = jnp.sin(ang)
    cos_h = jnp.concatenate([cos, cos], axis=-1)
    sin_h = jnp.concatenate([-sin, sin], axis=-1)
    return jnp.tile(cos_h, (1, H_RET)), jnp.tile(sin_h, (1, H_RET))


def _retention_tables(t_len):
    log_g = _ret_log_decay()
    t = jnp.arange(t_len, dtype=F32)
    diff = t[:, None] - t[None, :]
    decay = jnp.where(diff >= 0, jnp.exp(jnp.maximum(diff, 0.0)[None] * log_g[:, None, None]), 0.0)
    q_decay = jnp.exp((t + 1.0)[:, None] * log_g[None, :])
    k_decay = jnp.exp((t_len - 1.0 - t)[:, None] * log_g[None, :])
    head = jnp.arange(LANES) // HEAD_DIM
    same_head = (head[:, None] == head[None, :]).astype(F32)
    blk = jnp.exp(t_len * log_g).reshape(RET_W // LANES, LANES // HEAD_DIM)
    state_decay = jnp.broadcast_to(blk[:, head][:, :, None], (RET_W // LANES, LANES, LANES))
    return (decay.reshape(H_RET * t_len, t_len), jnp.repeat(q_decay, HEAD_DIM, axis=1),
            jnp.repeat(k_decay, HEAD_DIM, axis=1), (same_head / HEAD_DIM).astype(BF16), same_head,
            state_decay)


def _rope(x, cos, sin):
    nb, t_len, n = x.shape
    x2 = x.reshape(nb * t_len, n)
    lane = lax.broadcasted_iota(jnp.int32, x2.shape, 1)
    first_half = (lane & (HEAD_DIM - 1)) < (HEAD_DIM // 2)
    partner = jnp.where(first_half,
                        pltpu.roll(x2, n - HEAD_DIM // 2, axis=1),
                        pltpu.roll(x2, HEAD_DIM // 2, axis=1))
    return x * cos + partner.reshape(x.shape) * sin


def _group_mean(x, head_avg, split=True):
    nb, t_len, n = x.shape
    hi = x.astype(BF16).reshape(nb * t_len, n)
    if not split:
        return _dot(hi, head_avg).reshape(x.shape)
    lo = (x - x.astype(BF16).astype(F32)).astype(BF16).reshape(nb * t_len, n)
    r = _dot(jnp.concatenate([hi, lo], axis=0), head_avg)
    return (r[:nb * t_len] + r[nb * t_len:]).reshape(x.shape)


def _bdot(spec, a, b):
    return jnp.einsum(spec, a, b, preferred_element_type=F32)


def _retention_block(q, k, v, gate, cos, sin, dec, q_decay, k_decay, head_avg, same_head,
                     state_decay, state):
    nb, t_len, _ = q.shape
    n_cols = RET_W // LANES
    q = _rope(q, cos, sin)
    k = _rope(k, cos, sin) * (HEAD_DIM ** -0.5)
    lower_half = lax.broadcasted_iota(jnp.int32, (t_len, LANES), 1) < HEAD_DIM
    to_cols = lambda x: jnp.concatenate([x[:, :, p * LANES:(p + 1) * LANES] for p in range(n_cols)],
                                        axis=0)
    per_col = lambda x, tabs: jnp.concatenate([x[p * nb:(p + 1) * nb] * tabs[p] for p in range(n_cols)],
                                              axis=0)
    col_tabs = lambda tab: [tab[:, p * LANES:(p + 1) * LANES] for p in range(n_cols)]
    qc = to_cols(q)
    kc = to_cols(k).astype(BF16)
    kdc = to_cols(k * k_decay).astype(BF16)
    vc = to_cols(v).astype(BF16)
    st = jnp.concatenate(state, axis=0)
    q_stack = jnp.concatenate([jnp.where(lower_half, qc, 0.0).astype(BF16),
                               jnp.where(lower_half, 0.0, qc).astype(BF16)], axis=1)
    scores = per_col(_bdot('bqd,bkd->bqk', q_stack, kc),
                     [dec[2 * p * t_len:(2 * p + 2) * t_len] for p in range(n_cols)])
    inner_all = _bdot('bqk,bkd->bqd', scores.astype(BF16), vc)
    inner = jnp.where(lower_half, inner_all[:, :t_len], inner_all[:, t_len:])
    cross = per_col(_bdot('bqd,bde->bqe', qc.astype(BF16), st.astype(BF16)), col_tabs(q_decay))
    o = inner + cross
    d = o - _group_mean(o, head_avg)
    var = _group_mean(d * d, head_avg, split=False)
    on_cols = d * lax.rsqrt(var + EPS)
    new_st = (per_col(st, [state_decay[p] for p in range(n_cols)])
              + same_head * _bdot('bkd,bke->bde', kdc, vc))
    on = jnp.concatenate([on_cols[p * nb:(p + 1) * nb] for p in range(n_cols)], axis=-1)
    new_state = [new_st[p * nb:(p + 1) * nb] for p in range(n_cols)]
    return on * (gate * jax.nn.sigmoid(gate)), new_state


def _ret_prompt_kernel(q_ref, k_ref, v_ref, g_ref, cos_ref, sin_ref, dec_ref, qd_ref, kd_ref,
                       avg_ref, same_ref, sdec_ref, o_ref, s_ref, state_scr):
    t = pl.program_id(1)
    n_cols = RET_W // LANES

    @pl.when(t == 0)
    def _():
        state_scr[...] = jnp.zeros_like(state_scr)

    out, new_state = _retention_block(q_ref[...], k_ref[...], v_ref[...], g_ref[...], cos_ref[...],
                                      sin_ref[...], dec_ref[...], qd_ref[...], kd_ref[...],
                                      avg_ref[...], same_ref[...], sdec_ref[...],
                                      [state_scr[p] for p in range(n_cols)])
    o_ref[...] = out.astype(o_ref.dtype)
    for p in range(n_cols):
        state_scr[p] = new_state[p]

    @pl.when(t == pl.num_programs(1) - 1)
    def _():
        for p in range(n_cols):
            s_ref[:, 2 * p] = state_scr[p, :, 0:HEAD_DIM, 0:HEAD_DIM]
            s_ref[:, 2 * p + 1] = state_scr[p, :, HEAD_DIM:, HEAD_DIM:]


def _ret_prompt(ret, batch, seq, tables, rope, n_blk):
    tb = QBLK
    cos, sin = rope
    ret3 = ret.reshape(batch, seq, RET_COLS)
    col = lambda c: pl.BlockSpec((n_blk, tb, RET_W), lambda b, t: (b, t, c))
    pos = pl.BlockSpec((tb, RET_W), lambda b, t: (t, 0))
    whole = lambda a: pl.BlockSpec(a.shape, lambda b, t: (0,) * a.ndim)
    out, s_fin = pl.pallas_call(
        _ret_prompt_kernel,
        grid=(batch // n_blk, seq // tb),
        in_specs=[col(0), col(1), col(2), col(3), pos, pos] + [whole(a) for a in tables],
        out_specs=[
            pl.BlockSpec((n_blk, tb, RET_W), lambda b, t: (b, t, 0)),
            pl.BlockSpec((n_blk, H_RET, HEAD_DIM, HEAD_DIM), lambda b, t: (b, 0, 0, 0)),
        ],
        out_shape=[
            jax.ShapeDtypeStruct((batch, seq, RET_W), BF16),
            jax.ShapeDtypeStruct((batch, H_RET, HEAD_DIM, HEAD_DIM), F32),
        ],
        scratch_shapes=[pltpu.VMEM((RET_W // LANES, n_blk, LANES, LANES), F32)],
        compiler_params=_params(("parallel", "arbitrary")),
        name="ret_prompt",
    )(ret3, ret3, ret3, ret3, cos, sin, *tables)
    return out.reshape(batch * seq, RET_W), s_fin


def _emit_cache_tail(i, k_ref, v_ref, kt_ref, vt_ref):
    @pl.when(i == pl.num_programs(1) - 1)
    def _():
        n_rows = kt_ref.shape[-1]
        seq = k_ref.shape[0]
        kt_ref[...] = k_ref[seq - n_rows:seq, :].T
        vt_ref[...] = v_ref[seq - n_rows:seq, :].T


def _pair_softmax_pv(q_pair, k_halves, v_halves, bias_halves, lower_half, sinks=None):
    outs = []
    for half in range(2):
        q_h = jnp.where(lower_half, q_pair, 0) if half == 0 else jnp.where(lower_half, 0, q_pair)
        s = _dot_nt(q_h, k_halves[half]) + bias_halves[half]
        m = jnp.max(s, axis=-1, keepdims=True)
        if sinks is not None:
            m = jnp.maximum(m, sinks[half])
        r = _dot(jnp.exp(s - m).astype(BF16), v_halves[half])
        denom = r[:, LANES:]
        if sinks is not None:
            denom = denom + jnp.exp(sinks[half] - m)
        outs.append(r[:, :LANES] / denom)
    return jnp.where(lower_half, outs[0], outs[1])


def _swa_prompt_kernel(sink_ref, q_ref, k_ref, v_ref, o_ref, kt_ref, vt_ref, k_scr, v_scr):
    i = pl.program_id(1)

    @pl.when(i == 0)
    def _():
        k = k_ref[...]
        v = v_ref[...]
        ones = jnp.ones((k.shape[0], LANES), BF16)
        for x, (kx, vx) in enumerate(((k, v), (pltpu.roll(k, HEAD_DIM, axis=1),
                                               pltpu.roll(v, HEAD_DIM, axis=1)))):
            k_scr[x] = kx.astype(BF16)
            v_scr[x, :, 0:LANES] = vx.astype(BF16)
            v_scr[x, :, LANES:] = ones

    row = lax.broadcasted_iota(jnp.int32, (QBLK, SWA_WIN), 0)
    col = lax.broadcasted_iota(jnp.int32, (QBLK, SWA_WIN), 1)
    n_prev = SWA_ROWS // CHUNK
    lower_half = lax.broadcasted_iota(jnp.int32, (QBLK, LANES), 1) < HEAD_DIM
    for u in range(SWA_QSTEP):
        blk = i * SWA_QSTEP + u
        rows = slice(u * QBLK, (u + 1) * QBLK)
        start = pl.multiple_of(jnp.maximum(blk * QBLK - SWA_ROWS, 0), QBLK)
        win = pl.ds(start, SWA_WIN)
        q = (q_ref[rows, :] * (HEAD_DIM ** -0.5)).astype(BF16)
        behind = (blk * (QBLK // CHUNK) + row // CHUNK) - (start // CHUNK + col // CHUNK)
        mask = jnp.where(jnp.abs(2 * behind - n_prev) <= n_prev, 0.0, NEG_INF)
        for c in range(SWA_QW // LANES):
            cols = slice(c * LANES, (c + 1) * LANES)
            heads = (2 * c, 2 * c + 1)
            swap = [(h // G_SWA) ^ half for half, h in enumerate(heads)]
            out = _pair_softmax_pv(q[:, cols], [k_scr[x, win, :] for x in swap],
                                   [v_scr[x, win, :] for x in swap], [mask, mask], lower_half,
                                   sinks=[sink_ref[h] for h in heads])
            o_ref[rows, cols] = out.astype(o_ref.dtype)

    _emit_cache_tail(i, k_ref, v_ref, kt_ref, vt_ref)


def _swa_prompt(swa, sinks, batch, seq):
    tq = SWA_QSTEP * QBLK
    nq = seq // tq
    k_col = SWA_QW // SWA_KW
    tail = pl.BlockSpec((None, SWA_KW, SWA_ROWS), lambda b, i: (b, 0, 0))
    tail_shape = jax.ShapeDtypeStruct((batch, SWA_KW, SWA_ROWS), F32)
    return pl.pallas_call(
        _swa_prompt_kernel,
        grid=(batch, nq),
        in_specs=[
            pl.BlockSpec(memory_space=pltpu.SMEM),
            pl.BlockSpec((tq, SWA_QW), lambda b, i: (b * nq + i, 0)),
            pl.BlockSpec((seq, SWA_KW), lambda b, i: (b, k_col)),
            pl.BlockSpec((seq, SWA_KW), lambda b, i: (b, k_col + 1)),
        ],
        out_specs=[pl.BlockSpec((tq, SWA_QW), lambda b, i: (b * nq + i, 0)), tail, tail],
        out_shape=[jax.ShapeDtypeStruct((batch * seq, SWA_QW), BF16), tail_shape, tail_shape],
        scratch_shapes=[
            pltpu.VMEM((2, seq, SWA_KW), BF16),
            pltpu.VMEM((2, seq, 2 * LANES), BF16),
        ],
        compiler_params=_params(("parallel", "arbitrary")),
        name="swa_prompt",
    )(sinks, swa, swa, swa)


def _bias_kernel(t_ref, o_ref):
    n_pad = t_ref.shape[-1]
    r = lax.broadcasted_iota(jnp.int32, (n_pad, BIAS_ROW), 0)
    w = lax.broadcasted_iota(jnp.int32, (n_pad, BIAS_ROW), 1)
    idx = jnp.clip(BIAS_ROW // 2 - 1 - w, -REL_CLIP, REL_CLIP) + REL_CLIP
    onehot = jnp.where(r == idx, 1.0, 0.0).astype(BF16)
    t = t_ref[...]
    t_hi = t.astype(BF16)
    rem = t - t_hi.astype(F32)
    t_mid = rem.astype(BF16)
    t_lo = (rem - t_mid.astype(F32)).astype(BF16)
    f = (_dot(t_hi, onehot) + _dot(t_mid, onehot)) + _dot(t_lo, onehot)
    ext = CA_EXT_TILES * LANES
    row = lax.broadcasted_iota(jnp.int32, (QBLK, ext), 0)
    colj = lax.broadcasted_iota(jnp.int32, (QBLK, ext), 1)
    ahead = colj // CHUNK - row // CHUNK
    n_prev = CA_ROWS // CHUNK
    valid = jnp.abs(2 * ahead - n_prev) <= n_prev
    for h in range(H_CA):
        x = jnp.broadcast_to(f[h:h + 1, :], (QBLK, BIAS_ROW))
        y = pltpu.roll(x, BIAS_ROW - (QBLK - 1), axis=1, stride=1, stride_axis=0)
        y = jnp.where(valid, y[:, 0:ext], NEG_INF)
        for c in range(CA_EXT_TILES):
            o_ref[h, c] = y[:, c * LANES:(c + 1) * LANES]


def _build_bias(rel_bias_table):
    depth = rel_bias_table.shape[0]
    n_pad = 3 * LANES
    t = jnp.transpose(rel_bias_table, (0, 2, 1))
    t = jnp.pad(t, ((0, 0), (0, SUBLANES - H_CA), (0, n_pad - N_REL)))
    return pl.pallas_call(
        _bias_kernel,
        grid=(depth,),
        in_specs=[pl.BlockSpec((None, SUBLANES, n_pad), lambda l: (l, 0, 0))],
        out_specs=pl.BlockSpec((None, H_CA, CA_EXT_TILES, QBLK, LANES), lambda l: (l, 0, 0, 0, 0)),
        out_shape=jax.ShapeDtypeStruct((depth, H_CA, CA_EXT_TILES, QBLK, LANES), F32),
        compiler_params=_params(("parallel",)),
        name="ca_bias",
    )(t)


def _ca_prompt_kernel(q_ref, k_ref, v_ref, bias_ref, o_ref, kt_ref, vt_ref, k_scr, v_scr):
    i = pl.program_id(1)
    n_pairs = CA_W // LANES

    @pl.when(i == 0)
    def _():
        k_scr[...] = k_ref[...].astype(BF16)
        for p in range(n_pairs):
            v_scr[p, :, 0:LANES] = v_ref[:, p * LANES:(p + 1) * LANES].astype(BF16)
            v_scr[p, :, LANES:] = jnp.ones((v_scr.shape[1], LANES), BF16)

    lower_half = lax.broadcasted_iota(jnp.int32, (QBLK, LANES), 1) < HEAD_DIM
    for u in range(CA_QSTEP):
        rows = slice(u * QBLK, (u + 1) * QBLK)
        ideal = (i * CA_QSTEP + u) * QBLK - CA_ROWS
        start = pl.multiple_of(jnp.maximum(ideal, 0), QBLK)
        off = (start - ideal) // LANES
        q = (q_ref[rows, :] * (HEAD_DIM ** -0.5)).astype(BF16)
        for p in range(n_pairs):
            cols = slice(p * LANES, (p + 1) * LANES)
            bias = [jnp.concatenate([bias_ref[2 * p + half, off + c] for c in range(CA_WIN_TILES)],
                                    axis=-1) for half in range(2)]
            k_win = k_scr[pl.ds(start, CA_WIN), cols]
            v_win = v_scr[p, pl.ds(start, CA_WIN), :]
            out = _pair_softmax_pv(q[:, cols], [k_win, k_win], [v_win, v_win], bias, lower_half)
            o_ref[rows, cols] = out.astype(o_ref.dtype)

    _emit_cache_tail(i, k_ref, v_ref, kt_ref, vt_ref)


def _ca_prompt(ca, bias, layer, batch, seq):
    tq = CA_QSTEP * QBLK
    nq = seq // tq
    tail = pl.BlockSpec((None, CA_W, CA_ROWS), lambda b, i: (b, 0, 0))
    tail_shape = jax.ShapeDtypeStruct((batch, CA_W, CA_ROWS), F32)
    return pl.pallas_call(
        _ca_prompt_kernel,
        grid=(batch, nq),
        in_specs=[
            pl.BlockSpec((tq, CA_W), lambda b, i: (b * nq + i, 0)),
            pl.BlockSpec((seq, CA_W), lambda b, i: (b, 1)),
            pl.BlockSpec((seq, CA_W), lambda b, i: (b, 2)),
            pl.BlockSpec((None, H_CA, CA_EXT_TILES, QBLK, LANES), lambda b, i: (layer, 0, 0, 0, 0)),
        ],
        out_specs=[pl.BlockSpec((tq, CA_W), lambda b, i: (b * nq + i, 0)), tail, tail],
        out_shape=[jax.ShapeDtypeStruct((batch * seq, CA_W), BF16), tail_shape, tail_shape],
        scratch_shapes=[
            pltpu.VMEM((seq, CA_W), BF16),
            pltpu.VMEM((CA_W // LANES, seq, 2 * LANES), BF16),
        ],
        compiler_params=_params(("parallel", "arbitrary")),
        name="ca_prompt",
    )(ca, ca, ca, bias)


def _block_diag(blocks):
    z = jnp.zeros_like(blocks[0])
    return jnp.concatenate([jnp.concatenate([b if j == i else z for j in range(len(blocks))], axis=2)
                            for i, b in enumerate(blocks)], axis=1)


def _heads_softmax_pv_cached(q_heads, kt, kn, vt, vn, bias_c, bias_n, sink=None):
    s_c = _bdot('bqd,bdk->bqk', q_heads, kt) + bias_c
    s_n = _bdot('bqd,bkd->bqk', q_heads, kn) + bias_n
    m = jnp.maximum(jnp.max(s_c, axis=-1, keepdims=True), jnp.max(s_n, axis=-1, keepdims=True))
    if sink is not None:
        m = jnp.maximum(m, sink)
    r = (_bdot('bqk,bfk->bqf', jnp.exp(s_c - m).astype(BF16), vt)
         + _bdot('bqk,bkf->bqf', jnp.exp(s_n - m).astype(BF16), vn))
    denom = r[:, :, LANES:]
    if sink is not None:
        denom = denom + jnp.exp(sink - m)
    return r[:, :, :LANES] / denom


def _mixer_sample_kernel(sink_ref, ret_ref, swa_ref, ca_ref, cos_ref, sin_ref, dec_ref, qd_ref,
                         kd_ref, avg_ref, same_ref, sdec_ref, skt_ref, svt_ref, ckt_ref, cvt_ref,
                         st_ref, bias_ref, oa_ref, ob_ref, oc_ref, snew_ref):
    n_blk, t_len, _ = ret_ref.shape
    lower_half = lax.broadcasted_iota(jnp.int32, (t_len, LANES), 1) < HEAD_DIM
    ones_rows = lambda n: jnp.ones((n_blk, LANES, n), BF16)
    ones_cols = jnp.ones((n_blk, t_len, LANES), BF16)
    ca_tiles = CA_ROWS // LANES

    col = lambda c: ret_ref[:, :, c * RET_W:(c + 1) * RET_W]
    pairs = [_block_diag([st_ref[:, 2 * p], st_ref[:, 2 * p + 1]]) for p in range(RET_W // LANES)]
    out, new_state = _retention_block(col(0), col(1), col(2), col(3), cos_ref[...], sin_ref[...],
                                      dec_ref[...], qd_ref[...], kd_ref[...], avg_ref[...],
                                      same_ref[...], sdec_ref[...], pairs)
    oa_ref[...] = out.astype(oa_ref.dtype)
    for p in range(RET_W // LANES):
        snew_ref[:, 2 * p] = new_state[p][:, 0:HEAD_DIM, 0:HEAD_DIM]
        snew_ref[:, 2 * p + 1] = new_state[p][:, HEAD_DIM:, HEAD_DIM:]

    q = swa_ref[:, :, 0:SWA_QW] * (HEAD_DIM ** -0.5)
    kt = skt_ref[...].astype(BF16)
    vt = jnp.concatenate([svt_ref[...].astype(BF16), ones_rows(SWA_ROWS)], axis=1)
    kn = swa_ref[:, :, SWA_QW:SWA_QW + SWA_KW].astype(BF16)
    vn = jnp.concatenate([swa_ref[:, :, SWA_QW + SWA_KW:SWA_COLS].astype(BF16), ones_cols], axis=2)
    swap_halves = lambda x: pltpu.roll(x.reshape(n_blk * t_len, LANES), HEAD_DIM, axis=1).reshape(x.shape)
    own_lanes = lambda x, half: jnp.where(lower_half, x, 0) if half == 0 else jnp.where(lower_half, 0, x)
    group_row = lax.broadcasted_iota(jnp.int32, (G_SWA * t_len, 1), 0)
    head_out = []
    for g in range(KV_SWA):
        heads = range(g * G_SWA, (g + 1) * G_SWA)
        parts = []
        for h in heads:
            q_col = q[:, :, (h // 2) * LANES:(h // 2 + 1) * LANES]
            parts.append(own_lanes(q_col if h % 2 == g else swap_halves(q_col), g).astype(BF16))
        sink = jnp.full((G_SWA * t_len, 1), sink_ref[heads[-1]], F32)
        for j in range(G_SWA - 2, -1, -1):
            sink = jnp.where(group_row < (j + 1) * t_len, sink_ref[heads[j]], sink)
        out = _heads_softmax_pv_cached(jnp.concatenate(parts, axis=1), kt, kn, vt, vn, 0.0, 0.0, sink)
        for j, h in enumerate(heads):
            out_h = out[:, j * t_len:(j + 1) * t_len]
            head_out.append(out_h if h % 2 == g else swap_halves(out_h))
    for c in range(SWA_QW // LANES):
        ob_ref[:, :, c * LANES:(c + 1) * LANES] = jnp.where(
            lower_half, head_out[2 * c], head_out[2 * c + 1]).astype(ob_ref.dtype)

    q = (ca_ref[:, :, 0:CA_W] * (HEAD_DIM ** -0.5)).astype(BF16)
    k_new = ca_ref[:, :, CA_W:2 * CA_W]
    v_new = ca_ref[:, :, 2 * CA_W:CA_COLS]
    for p in range(CA_W // LANES):
        cols = slice(p * LANES, (p + 1) * LANES)
        kt_p = ckt_ref[:, cols, :].astype(BF16)
        vt_p = jnp.concatenate([cvt_ref[:, cols, :].astype(BF16), ones_rows(CA_ROWS)], axis=1)
        kn_p = k_new[:, :, cols].astype(BF16)
        vn_p = jnp.concatenate([v_new[:, :, cols].astype(BF16), ones_cols], axis=2)
        bias_c = jnp.concatenate(
            [jnp.concatenate([bias_ref[2 * p + half, c, 0:t_len, :] for c in range(ca_tiles)], axis=-1)
             for half in range(2)], axis=0)
        bias_n = jnp.concatenate([bias_ref[2 * p + half, ca_tiles, 0:t_len, 0:t_len]
                                  for half in range(2)], axis=0)
        q_stack = jnp.concatenate([own_lanes(q[:, :, cols], half) for half in range(2)], axis=1)
        out = _heads_softmax_pv_cached(q_stack, kt_p, kn_p, vt_p, vn_p, bias_c, bias_n)
        oc_ref[:, :, cols] = jnp.where(lower_half, out[:, :t_len], out[:, t_len:]).astype(oc_ref.dtype)


def _mixer_sample(ret, swa, ca, sinks, tables, rope, cache_skt, cache_svt, cache_ckt, cache_cvt,
                  state_ret, bias, layer, n_seq, t_len, n_blk):
    cos, sin = rope
    whole = lambda shape: pl.BlockSpec(shape, lambda b: (0,) * len(shape))
    seq_rows = lambda width: pl.BlockSpec((n_blk, t_len, width), lambda b: (b, 0, 0))
    cache = lambda feat, rows: pl.BlockSpec((None, n_blk, feat, rows), lambda b: (layer, b, 0, 0))
    per_stream = lambda a: a.reshape(n_seq, t_len, a.shape[-1])
    oa, ob, oc, s_new = pl.pallas_call(
        _mixer_sample_kernel,
        grid=(n_seq // n_blk,),
        in_specs=[
            pl.BlockSpec(memory_space=pltpu.SMEM),
            seq_rows(RET_COLS), seq_rows(SWA_COLS), seq_rows(CA_COLS),
            whole((t_len, RET_W)), whole((t_len, RET_W)),
            *[whole(a.shape) for a in tables],
            cache(SWA_KW, SWA_ROWS), cache(SWA_KW, SWA_ROWS),
            cache(CA_W, CA_ROWS), cache(CA_W, CA_ROWS),
            pl.BlockSpec((None, n_blk, H_RET, HEAD_DIM, HEAD_DIM), lambda b: (layer, b, 0, 0, 0)),
            pl.BlockSpec((None, H_CA, CA_EXT_TILES, QBLK, LANES), lambda b: (layer, 0, 0, 0, 0)),
        ],
        out_specs=[
            seq_rows(RET_W), seq_rows(SWA_QW), seq_rows(CA_W),
            pl.BlockSpec((n_blk, H_RET, HEAD_DIM, HEAD_DIM), lambda b: (b, 0, 0, 0)),
        ],
        out_shape=[
            jax.ShapeDtypeStruct((n_seq, t_len, RET_W), BF16),
            jax.ShapeDtypeStruct((n_seq, t_len, SWA_QW), BF16),
            jax.ShapeDtypeStruct((n_seq, t_len, CA_W), BF16),
            jax.ShapeDtypeStruct((n_seq, H_RET, HEAD_DIM, HEAD_DIM), F32),
        ],
        compiler_params=_params(("parallel",)),
        name="mixer_sample",
    )(sinks, per_stream(ret), per_stream(swa), per_stream(ca), cos, sin, *tables, cache_skt,
      cache_svt, cache_ckt, cache_cvt, state_ret, bias)
    flat = lambda a: a.reshape(n_seq * t_len, a.shape[-1])
    return flat(oa), flat(ob), flat(oc), s_new


def _append_rows_feature_major(cache_t, new_rows):
    n_feat, n_rows = cache_t.shape
    t_len = new_rows.shape[0]
    rolled = pltpu.roll(cache_t, n_rows - t_len, axis=1)
    t_idx = lax.broadcasted_iota(jnp.int32, (t_len, LANES), 0)
    x_idx = lax.broadcasted_iota(jnp.int32, (t_len, LANES), 1)
    place = jnp.where(x_idx == LANES - t_len + t_idx, 1.0, 0.0).astype(BF16)
    hi = new_rows.astype(BF16)
    rem = new_rows - hi.astype(F32)
    mid = rem.astype(BF16)
    lo = (rem - mid.astype(F32)).astype(BF16)
    placed = (_dot_tn(hi, place) + _dot_tn(mid, place)) + _dot_tn(lo, place)
    lane = lax.broadcasted_iota(jnp.int32, (n_feat, LANES), 1)
    last = jnp.where(lane < LANES - t_len, rolled[:, n_rows - LANES:], placed)
    if n_rows == LANES:
        return last
    return jnp.concatenate([rolled[:, :n_rows - LANES], last], axis=1)


def _ffn_kernel(*refs, tm, tf, pad, shift, tiles_per_seq, final_norm, n_rolls, n_casts):
    refs = list(refs)
    (x_ref, a_ref, b_ref, c_ref, wo_ref, gain_ref, wg_ref, wu_ref, wd_ref, cw_ref, cb_ref,
     prev_ref) = refs[:12]
    del refs[:12]
    gf_ref = refs.pop(0) if final_norm else None
    roll_in = [refs.pop(0) for _ in range(4 * n_rolls)]
    cast_in = [refs.pop(0) for _ in range(n_casts)]
    o_ref, st_ref = refs.pop(0), refs.pop(0)
    roll_out = [refs.pop(0) for _ in range(2 * n_rolls)]
    cast_out = [refs.pop(0) for _ in range(n_casts)]
    hn_scr, act_scr, hist_scr = refs
    for g in range(n_rolls):
        ck_ref, cv_ref, nk_ref, nv_ref = roll_in[4 * g:4 * g + 4]
        for l in range(ck_ref.shape[0]):
            roll_out[2 * g][l, 0] = _append_rows_feature_major(ck_ref[l, 0], nk_ref[l, 0])
            roll_out[2 * g + 1][l, 0] = _append_rows_feature_major(cv_ref[l, 0], nv_ref[l, 0])
    for src_ref, dst_ref in zip(cast_in, cast_out):
        dst_ref[...] = src_ref[...].astype(BF16)
    x = x_ref[...] + _dot(a_ref[...], wo_ref[0:RET_W, :])
    x = x + _dot(b_ref[...], wo_ref[RET_W:RET_W + SWA_QW, :])
    x = x + _dot(c_ref[...], wo_ref[RET_W + SWA_QW:D_MODEL, :])
    o_ref[...] = x
    hn_scr[...] = _rms(x, gain_ref[...]).astype(BF16)
    if tiles_per_seq == 1:
        hist = prev_ref
    else:
        hist = hist_scr

        @pl.when((pl.program_id(0) % tiles_per_seq) == 0)
        def _():
            hist_scr[...] = prev_ref[...]

    for c in range(D_FF // tf):
        cols = slice(c * tf, (c + 1) * tf)
        hn = hn_scr[...]
        g = _dot(hn, wg_ref[:, cols])
        u = _dot(hn, wu_ref[:, cols])
        g_ext = jnp.concatenate([hist[:, cols], g], axis=0)
        gc = cb_ref[:, cols] + cw_ref[0:1, cols] * pltpu.roll(g_ext, 2 * shift, axis=0)[pad:]
        gc = gc + cw_ref[1:2, cols] * pltpu.roll(g_ext, shift, axis=0)[pad:]
        gc = gc + cw_ref[2:3, cols] * g
        act_scr[:, cols] = (jax.nn.gelu(gc) * u).astype(BF16)
        tail = g_ext[tm:tm + pad]
        if tiles_per_seq > 1:
            hist_scr[:, cols] = tail
        st_ref[:, cols] = tail

    y = o_ref[...] + _dot(act_scr[...], wd_ref[...])
    if final_norm:
        y = _rms(y, gf_ref[...])
    o_ref[...] = y


def _outproj_ffn(x, oa, ob, oc, w_out, gain, w_gate, w_up, w_down, conv_w, conv_b, prev, layer,
                 w_layer, tm, tf, shift, tiles_per_seq, final_gain=None, rolls=(), casts=()):
    m = x.shape[0]
    n_groups, pad, _ = prev.shape
    final_norm = final_gain is not None
    resident = lambda shape, at=layer: pl.BlockSpec((None,) + shape, lambda i: (at, 0, 0),
                                                    pipeline_mode=pl.Buffered(1))
    weight = lambda shape: resident(shape, w_layer)
    state = pl.BlockSpec((None, pad, D_FF), lambda i: (i // tiles_per_seq, 0, 0))
    rows = lambda width: pl.BlockSpec((tm, width), lambda i: (i, 0))
    in_specs = [
        rows(D_MODEL), rows(RET_W), rows(SWA_QW), rows(CA_W),
        weight((D_MODEL, D_MODEL)),
        resident((1, D_MODEL)),
        weight((D_MODEL, D_FF)), weight((D_MODEL, D_FF)), weight((D_FF, D_MODEL)),
        resident((CONV_W, D_FF)), resident((1, D_FF)),
        state,
    ]
    args = [x, oa, ob, oc, w_out, gain, w_gate, w_up, w_down, conv_w, conv_b, prev]
    if final_norm:
        in_specs.append(pl.BlockSpec((1, D_MODEL), lambda i: (0, 0)))
        args.append(final_gain)
    out_specs = [rows(D_MODEL), state]
    out_shape = [jax.ShapeDtypeStruct((m, D_MODEL), F32),
                 jax.ShapeDtypeStruct((n_groups, pad, D_FF), F32)]
    n_steps = m // tm
    roll_outs = []
    for roll in rolls:
        depth, n_seq, n_feat, n_rows = roll[0].shape
        assert n_seq == n_steps
        cache = pl.BlockSpec((depth, 1, n_feat, n_rows), lambda i: (0, i, 0, 0))
        new = pl.BlockSpec((depth, 1, roll[2].shape[2], n_feat), lambda i: (0, i, 0, 0))
        in_specs += [cache, cache, new, new]
        args += list(roll)
        roll_outs += [(cache, jax.ShapeDtypeStruct(roll[0].shape, F32))] * 2
    cast_outs = []
    for w, cast_layer in casts:
        _, n_rows, n_cols = w.shape
        slab = min(d for d in range(16, n_rows + 1, 16)
                   if n_rows % d == 0 and n_steps % (n_rows // d) == 0)
        per_slab = n_steps // (n_rows // slab)
        in_specs.append(pl.BlockSpec((None, slab, n_cols),
                                     lambda i, at=cast_layer, per_slab=per_slab: (at, i // per_slab, 0)))
        args.append(w)
        cast_outs.append((pl.BlockSpec((None, slab, n_cols),
                                       lambda i, per_slab=per_slab: (0, i // per_slab, 0)),
                          jax.ShapeDtypeStruct((1, n_rows, n_cols), BF16)))
    for spec, shape in roll_outs + cast_outs:
        out_specs.append(spec)
        out_shape.append(shape)
    return pl.pallas_call(
        functools.partial(_ffn_kernel, tm=tm, tf=tf, pad=pad, shift=shift,
                          tiles_per_seq=tiles_per_seq, final_norm=final_norm,
                          n_rolls=len(rolls), n_casts=len(casts)),
        grid=(n_steps,),
        in_specs=in_specs,
        out_specs=out_specs,
        out_shape=out_shape,
        scratch_shapes=[
            pltpu.VMEM((tm, D_MODEL), BF16),
            pltpu.VMEM((tm, D_FF), BF16),
            pltpu.VMEM((pad, D_FF), F32),
        ],
        compiler_params=_params(("arbitrary",)),
        name="outproj_ffn",
    )(*args)


def kernel(x_prompt, x_sample, cache_swa_k, cache_swa_v, cache_ca_k, cache_ca_v, state_ret,
           state_ffn_conv, w_in, w_out, attn_sinks, rel_bias_table, norm_mix, norm_ffn, w_gate,
           w_up, conv_w, conv_b, w_down, norm_final):
    batch, seq, _ = x_prompt.shape
    n_seq, t_len, _ = x_sample.shape
    depth = w_in.shape[0]
    assert seq % (max(SWA_QSTEP, CA_QSTEP) * QBLK) == 0 and seq >= CA_WIN and t_len % SUBLANES == 0
    assert cache_swa_k.shape[2] == SWA_ROWS and cache_ca_k.shape[2] == CA_ROWS

    weights = (w_in, w_out, w_gate, w_up, w_down)
    w_in_b, w_out_b, w_gate_b, w_up_b, w_down_b = [w[0:1].astype(BF16) for w in weights]
    g_mix = norm_mix.reshape(depth, 1, D_MODEL)
    g_ffn = norm_ffn.reshape(depth, 1, D_MODEL)
    g_fin = norm_final.reshape(1, D_MODEL)
    cb = conv_b.reshape(depth, 1, D_FF)

    rope_p = _rope_tables(jnp.arange(seq))
    rope_s = _rope_tables(PAST_LEN + jnp.arange(t_len))
    tab_p = _retention_tables(QBLK)
    tab_s = _retention_tables(t_len)
    bias = _build_bias(rel_bias_table)

    feature_major = lambda c: jnp.transpose(c, (0, 1, 3, 4, 2)).reshape(depth, n_seq, -1, c.shape[2])
    cskt, csvt, cckt, ccvt = map(feature_major, (cache_swa_k, cache_swa_v, cache_ca_k, cache_ca_v))

    hp = x_prompt.reshape(batch * seq, D_MODEL)
    m_s = n_seq * t_len
    hs = x_sample.reshape(m_s, D_MODEL)

    pad_p = SUBLANES
    pad_s = (CONV_W - 1) * n_seq
    prev_p = jnp.zeros((batch, pad_p, D_FF), F32)
    tm_p = 512
    tiles_per_seq = seq // tm_p

    stack = lambda v: jnp.stack(v, axis=0)
    outs_p = [[] for _ in range(6)]
    outs_s = [[] for _ in range(6)]
    for l in range(depth):
        last = l == depth - 1
        ret, swa, ca = _inproj(hs, g_mix, w_in_b, l, 0, tm=m_s // 2)
        oa, ob, oc, s_new = _mixer_sample(ret, swa, ca, attn_sinks[l], tab_s, rope_s, cskt, csvt,
                                          cckt, ccvt, state_ret, bias, l, n_seq, t_len, n_blk=8)
        time_major = lambda a: jnp.transpose(a.reshape(n_seq, t_len, -1), (1, 0, 2)).reshape(m_s, -1)
        prev_s = jnp.transpose(state_ffn_conv[l], (1, 0, 2)).reshape(1, pad_s, D_FF)
        hs_tm, conv_st = _outproj_ffn(time_major(hs), time_major(oa), time_major(ob), time_major(oc),
                                      w_out_b, g_ffn, w_gate_b, w_up_b, w_down_b, conv_w, cb, prev_s,
                                      l, 0, tm=m_s, tf=256, shift=n_seq, tiles_per_seq=1,
                                      final_gain=g_fin if last else None)
        hs = jnp.transpose(hs_tm.reshape(t_len, n_seq, D_MODEL), (1, 0, 2)).reshape(m_s, D_MODEL)
        swa3 = swa.reshape(n_seq, t_len, SWA_COLS)
        ca3 = ca.reshape(n_seq, t_len, CA_COLS)
        outs_s[0].append(swa3[:, :, SWA_QW:SWA_QW + SWA_KW])
        outs_s[1].append(swa3[:, :, SWA_QW + SWA_KW:])
        outs_s[2].append(ca3[:, :, CA_W:2 * CA_W])
        outs_s[3].append(ca3[:, :, 2 * CA_W:])
        outs_s[4].append(s_new)
        outs_s[5].append(jnp.transpose(conv_st.reshape(CONV_W - 1, n_seq, D_FF), (1, 0, 2)))

        ret, swa, ca = _inproj(hp, g_mix, w_in_b, l, 0, tm=512)
        oa, s_ret = _ret_prompt(ret, batch, seq, tab_p, rope_p, n_blk=8)
        ob, *swa_tail = _swa_prompt(swa, attn_sinks[l], batch, seq)
        oc, *ca_tail = _ca_prompt(ca, bias, l, batch, seq)
        rolls = [(cskt, csvt, stack(outs_s[0]), stack(outs_s[1])),
                 (cckt, ccvt, stack(outs_s[2]), stack(outs_s[3]))] if last else []
        casts = [] if last else [(w, l + 1) for w in weights]
        hp, conv_st, *riders = _outproj_ffn(
            hp, oa, ob, oc, w_out_b, g_ffn, w_gate_b, w_up_b, w_down_b, conv_w, cb, prev_p, l, 0,
            tm=tm_p, tf=256, shift=1, tiles_per_seq=tiles_per_seq,
            final_gain=g_fin if last else None, rolls=rolls, casts=casts)
        if not last:
            w_in_b, w_out_b, w_gate_b, w_up_b, w_down_b = riders
        for lst, val in zip(outs_p[:4], swa_tail + ca_tail):
            lst.append(val)
        outs_p[4].append(s_ret)
        outs_p[5].append(conv_st[:, pad_p - (CONV_W - 1):, :])

    y_prompt = hp.reshape(batch, seq, D_MODEL)
    y_sample = hs.reshape(n_seq, t_len, D_MODEL)
    row_major = lambda c, heads: jnp.transpose(
        c.reshape(depth, c.shape[1], heads, HEAD_DIM, -1), (0, 1, 4, 2, 3))
    swa_k_p = row_major(stack(outs_p[0]), KV_SWA)
    swa_v_p = row_major(stack(outs_p[1]), KV_SWA)
    ca_k_p = row_major(stack(outs_p[2]), H_CA)
    ca_v_p = row_major(stack(outs_p[3]), H_CA)
    ret_p = stack(outs_p[4])
    conv_p = stack(outs_p[5])

    swa_kt, swa_vt, ca_kt, ca_vt = riders
    swa_k_s = row_major(swa_kt, KV_SWA)
    swa_v_s = row_major(swa_vt, KV_SWA)
    ca_k_s = row_major(ca_kt, H_CA)
    ca_v_s = row_major(ca_vt, H_CA)
    ret_s = stack(outs_s[4])
    conv_s = stack(outs_s[5])
    return (y_prompt, y_sample, swa_k_p, swa_v_p, ca_k_p, ca_v_p, ret_p, conv_p,
            swa_k_s, swa_v_s, ca_k_s, ca_v_s, ret_s, conv_s)
```
